```python
import jax, jax.numpy as jnp
from jax import lax
import numpy as np

D_MODEL = 1024
BATCH = 32
SEQ = 256
DEPTH = 2
DEC_BATCH = 4
DEC_SEQ = 1024
PAST_LEN = 512

GRID_W = 64
N_MIXERS = 2
N_ATTN_LAYERS = (DEPTH + 1) // 2
N_GMLP_LAYERS = DEPTH // 2
N_HEADS = 16
HEAD_DIM = D_MODEL // N_HEADS
NA_ROWS = 8
NA_COLS = 16
Q_BLOCK = 128
CHUNK = 128
GMLP_WIDTH = 2 * D_MODEL
GMLP_GROUPS = 16
GMLP_GROUP_DIM = GMLP_WIDTH // GMLP_GROUPS
FFN_DIM = 2816
CONV_W = 3
EPS = 1e-6

kernel_name = 'hybrid_natten_gmlp_diffusion_step'


def rmsnorm(x, g):
    xf = x.astype(jnp.float32)
    y = xf * lax.rsqrt(jnp.mean(xf * xf, axis=-1, keepdims=True) + EPS)
    return (y * g.astype(jnp.float32)).astype(x.dtype)


def layernorm(x, g):
    xf = x.astype(jnp.float32)
    xc = xf - jnp.mean(xf, axis=-1, keepdims=True)
    y = xc * lax.rsqrt(jnp.mean(xc * xc, axis=-1, keepdims=True) + EPS)
    return (y * g.astype(jnp.float32)).astype(x.dtype)


def adaln(cond, w_ada, b_ada):
    mod = jax.nn.silu(cond) @ w_ada + b_ada
    return jnp.split(mod[:, None, :], 6, axis=-1)


def modulate(h, shift, scale):
    return h * (1 + scale) + shift


def qkv_proj(h, w_qkv):
    B, T, _ = h.shape
    qkv = (h @ w_qkv).reshape(B, T, 3, N_HEADS, HEAD_DIM)
    qkv = jnp.transpose(qkv, (2, 0, 3, 1, 4))
    return qkv[0], qkv[1], qkv[2]


def merge_heads(o):
    B, H, T, Dh = o.shape
    return jnp.transpose(o, (0, 2, 1, 3)).reshape(B, T, H * Dh)


def context_attention(q, k, v):
    B, H, S, Dh = q.shape
    nb = S // Q_BLOCK
    scale = HEAD_DIM ** -0.5
    qb = jnp.moveaxis(q.reshape(B, H, nb, Q_BLOCK, Dh), 2, 0)

    def block(qi):
        s = jnp.einsum('bhqd,bhkd->bhqk', qi, k).astype(jnp.float32) * scale
        p = jax.nn.softmax(s, axis=-1).astype(v.dtype)
        return jnp.einsum('bhqk,bhkd->bhqd', p, v)

    o = lax.map(block, qb)
    return jnp.moveaxis(o, 0, 2).reshape(B, H, S, Dh)


def neighbourhood_attention(q, k, v, k_ctx, v_ctx, rpb):
    B, H, T, Dh = q.shape
    rows = T // GRID_W
    kh = min(NA_ROWS, rows)
    n_win = kh * NA_COLS
    scale = HEAD_DIM ** -0.5
    qg = q.reshape(B, H, rows, GRID_W, Dh)
    kg = k.reshape(B, H, rows, GRID_W, Dh)
    vg = v.reshape(B, H, rows, GRID_W, Dh)
    r_idx = jnp.arange(rows)
    row_start = jnp.clip(r_idx - kh // 2, 0, rows - kh)
    cols = jnp.arange(GRID_W)
    col_start = jnp.clip(cols - NA_COLS // 2, 0, GRID_W - NA_COLS)
    col_idx = col_start[:, None] + jnp.arange(NA_COLS)[None, :]
    col_off = col_idx - cols[:, None] + (NA_COLS - 1)

    def row_block(args):
        q_r, r, rs = args
        k_blk = lax.dynamic_slice_in_dim(kg, rs, kh, axis=2)
        v_blk = lax.dynamic_slice_in_dim(vg, rs, kh, axis=2)
        k_win = k_blk[:, :, :, col_idx, :]
        v_win = v_blk[:, :, :, col_idx, :]
        row_off = rs + jnp.arange(kh) - r + (NA_ROWS - 1)
        bias = jnp.transpose(rpb[:, row_off][:, :, col_off], (0, 2, 1, 3))
        s_win = (jnp.einsum('bhqd,bhiqjd->bhqij', q_r, k_win).astype(jnp.float32) * scale
                 + bias.astype(jnp.float32)).reshape(B, H, GRID_W, n_win)
        s_ctx = jnp.einsum('bhqd,bhcd->bhqc', q_r, k_ctx).astype(jnp.float32) * scale
        p = jax.nn.softmax(jnp.concatenate([s_win, s_ctx], axis=-1), axis=-1).astype(v.dtype)
        p_win = p[..., :n_win].reshape(B, H, GRID_W, kh, NA_COLS)
        p_ctx = p[..., n_win:]
        return (jnp.einsum('bhqij,bhiqjd->bhqd', p_win, v_win)
                + jnp.einsum('bhqc,bhcd->bhqd', p_ctx, v_ctx))

    o = lax.map(row_block, (jnp.moveaxis(qg, 2, 0), r_idx, row_start))
    return jnp.moveaxis(o, 0, 2).reshape(B, H, T, Dh)


def chunk_gmlp(h, w_in, g_v, w_s, b_s, w_out):
    B, T, _ = h.shape
    nc = T // CHUNK
    z = jax.nn.gelu(h @ w_in)
    u, vv = jnp.split(z, 2, axis=-1)
    vv = layernorm(vv, g_v).reshape(B, nc, CHUNK, GMLP_GROUPS, GMLP_GROUP_DIM)
    mixed = jnp.einsum('gpq,bnqgc->bnpgc', w_s, vv) + jnp.transpose(b_s)[:, :, None]
    return (u * mixed.reshape(B, T, GMLP_WIDTH)) @ w_out


def conv_ffn(h, w_up, w_conv, b_conv, w_down):
    T = h.shape[1]
    a = h @ w_up
    pad = CONV_W // 2
    ap = jnp.pad(a, ((0, 0), (pad, pad), (0, 0)))
    a = sum(ap[:, j:j + T] * w_conv[j] for j in range(CONV_W)) + b_conv
    g, val = jnp.split(a, 2, axis=-1)
    return (jax.nn.silu(g) * val) @ w_down


def setup_inputs(seed: int = 0) -> dict:
    key = jax.random.key(seed)
    ks = jax.random.split(key, 24)
    D, E, F, G = D_MODEL, GMLP_WIDTH, FFN_DIM, GMLP_GROUPS
    nrm = jax.random.normal
    return {
        'x_prompt': nrm(ks[0], (BATCH, SEQ, D), jnp.float32),
        'x_sample': nrm(ks[1], (DEC_BATCH, DEC_SEQ, D), jnp.float32),
        'cache_k': nrm(ks[2], (DEC_BATCH, N_ATTN_LAYERS, N_HEADS, PAST_LEN, HEAD_DIM), jnp.float32),
        'cache_v': nrm(ks[3], (DEC_BATCH, N_ATTN_LAYERS, N_HEADS, PAST_LEN, HEAD_DIM), jnp.float32),
        'c': nrm(ks[4], (DEC_BATCH, D), jnp.float32),
        'c_ctx': nrm(ks[5], (D,), jnp.float32),
        'w_ada': nrm(ks[6], (DEPTH, D, 6 * D), jnp.float32) * D ** -0.5,
        'b_ada': nrm(ks[7], (DEPTH, 6 * D), jnp.float32) * 0.01,
        'norm_mix_g': 1.0 + 0.02 * nrm(ks[8], (DEPTH, D), jnp.float32),
        'norm_ffn_g': 1.0 + 0.02 * nrm(ks[9], (DEPTH, D), jnp.float32),
        'norm_final_g': 1.0 + 0.02 * nrm(ks[10], (D,), jnp.float32),
        'w_qkv': nrm(ks[11], (N_ATTN_LAYERS, D, 3 * D), jnp.float32) * D ** -0.5,
        'w_attn_out': nrm(ks[12], (N_ATTN_LAYERS, D, D), jnp.float32) * D ** -0.5,
        'rpb': 0.2 * nrm(ks[13], (N_ATTN_LAYERS, N_HEADS, 2 * NA_ROWS - 1, 2 * NA_COLS - 1), jnp.float32),
        'w_gmlp_in': nrm(ks[14], (N_GMLP_LAYERS, D, 2 * E), jnp.float32) * D ** -0.5,
        'g_gmlp_v': 1.0 + 0.02 * nrm(ks[15], (N_GMLP_LAYERS, E), jnp.float32),
        'w_spatial': nrm(ks[16], (N_GMLP_LAYERS, G, CHUNK, CHUNK), jnp.float32) * CHUNK ** -0.5,
        'b_spatial': 1.0 + 0.01 * nrm(ks[17], (N_GMLP_LAYERS, G, CHUNK), jnp.float32),
        'w_gmlp_out': nrm(ks[18], (N_GMLP_LAYERS, E, D), jnp.float32) * E ** -0.5,
        'w_ffn_up': nrm(ks[19], (DEPTH, D, 2 * F), jnp.float32) * D ** -0.5,
        'w_ffn_conv': nrm(ks[20], (DEPTH, CONV_W, 2 * F), jnp.float32) * CONV_W ** -0.5,
        'b_ffn_conv': 0.01 * nrm(ks[21], (DEPTH, 2 * F), jnp.float32),
        'w_ffn_down': nrm(ks[22], (DEPTH, F, D), jnp.float32) * F ** -0.5,
    }


def reference(x_prompt, x_sample, cache_k, cache_v, c, c_ctx, w_ada, b_ada, norm_mix_g,
              norm_ffn_g, norm_final_g, w_qkv, w_attn_out, rpb, w_gmlp_in, g_gmlp_v,
              w_spatial, b_spatial, w_gmlp_out, w_ffn_up, w_ffn_conv, b_ffn_conv, w_ffn_down):
    x = x_prompt
    ks_new, vs_new = [], []
    for l in range(DEPTH):
        sm, cm, gm, sf, cf, gf = adaln(c_ctx[None, :], w_ada[l], b_ada[l])
        h = modulate(rmsnorm(x, norm_mix_g[l]), sm, cm)
        j = l // N_MIXERS
        if l % N_MIXERS == 0:
            q, k, v = qkv_proj(h, w_qkv[j])
            mix = merge_heads(context_attention(q, k, v)) @ w_attn_out[j]
            ks_new.append(k)
            vs_new.append(v)
        else:
            mix = chunk_gmlp(h, w_gmlp_in[j], g_gmlp_v[j], w_spatial[j], b_spatial[j], w_gmlp_out[j])
        x = x + gm * mix
        h = modulate(rmsnorm(x, norm_ffn_g[l]), sf, cf)
        x = x + gf * conv_ffn(h, w_ffn_up[l], w_ffn_conv[l], b_ffn_conv[l], w_ffn_down[l])
    y_prompt = rmsnorm(x, norm_final_g)
    state_k = jnp.stack(ks_new, axis=1)
    state_v = jnp.stack(vs_new, axis=1)

    x = x_sample
    for l in range(DEPTH):
        sm, cm, gm, sf, cf, gf = adaln(c, w_ada[l], b_ada[l])
        h = modulate(rmsnorm(x, norm_mix_g[l]), sm, cm)
        j = l // N_MIXERS
        if l % N_MIXERS == 0:
            q, k, v = qkv_proj(h, w_qkv[j])
            o = neighbourhood_attention(q, k, v, cache_k[:, j], cache_v[:, j], rpb[j])
            mix = merge_heads(o) @ w_attn_out[j]
        else:
            mix = chunk_gmlp(h, w_gmlp_in[j], g_gmlp_v[j], w_spatial[j], b_spatial[j], w_gmlp_out[j])
        x = x + gm * mix
        h = modulate(rmsnorm(x, norm_ffn_g[l]), sf, cf)
        x = x + gf * conv_ffn(h, w_ffn_up[l], w_ffn_conv[l], b_ffn_conv[l], w_ffn_down[l])
    y_sample = rmsnorm(x, norm_final_g)

    return (y_prompt, y_sample, state_k, state_v)
```

```python
import functools

import numpy as np
import jax
import jax.numpy as jnp
from jax import lax
from jax.experimental import pallas as pl
from jax.experimental.pallas import tpu as pltpu

D_MODEL = 1024
N_HEADS = 16
HEAD_DIM = D_MODEL // N_HEADS
GRID_W = 64
NA_ROWS = 8
NA_COLS = 16
CHUNK = 128
GMLP_WIDTH = 2 * D_MODEL
GMLP_GROUPS = 16
FFN_DIM = 2816
CONV_W = 3
EPS = 1e-6
ATTN_SCALE = HEAD_DIM ** -0.5

LANES = 128
HEADS_PER_BLOCK = LANES // HEAD_DIM
MOD_ROWS = 8
HALO = 16
FFN_TILE = 256
NA_QROWS = 4
NEG_BIAS = -1e30
VMEM_LIMIT = 56 * 1024 * 1024

BF16 = jnp.bfloat16
F32 = jnp.float32


def _params(n_axes=1):
    return pltpu.CompilerParams(
        dimension_semantics=("arbitrary",) * n_axes, vmem_limit_bytes=VMEM_LIMIT)


def _resident(shape):
    nd = len(shape)
    return pl.BlockSpec(shape, lambda *_: (0,) * nd, pipeline_mode=pl.Buffered(1))


def _rms_mod(x, g, shift, scale):
    y = x * lax.rsqrt(jnp.mean(x * x, axis=-1, keepdims=True) + EPS)
    return (y * g) * (1 + scale) + shift


def _mod_chunk(mod, k):
    return mod[:, k * D_MODEL:(k + 1) * D_MODEL]


def _ada_kernel(cond_ref, w_ref, b_ref, o_ref):
    s = jax.nn.silu(cond_ref[...]).astype(BF16)
    o_ref[0] = jnp.dot(s, w_ref[0].astype(BF16), preferred_element_type=F32) + b_ref[0]


def _ada(cond, w_ada, b_ada):
    depth, d, n = w_ada.shape
    tn = 1536
    return pl.pallas_call(
        _ada_kernel,
        grid=(depth, n // tn),
        in_specs=[
            pl.BlockSpec((MOD_ROWS, d), lambda l, j: (0, 0)),
            pl.BlockSpec((1, d, tn), lambda l, j: (l, 0, j)),
            pl.BlockSpec((1, 1, tn), lambda l, j: (l, 0, j)),
        ],
        out_specs=pl.BlockSpec((1, MOD_ROWS, tn), lambda l, j: (l, 0, j)),
        out_shape=jax.ShapeDtypeStruct((depth, MOD_ROWS, n), F32),
        compiler_params=_params(2),
        name="ada",
    )(cond, w_ada, b_ada.reshape(depth, 1, n))


def _softmax_pv(s, v):
    p = jnp.exp(s - jnp.max(s, axis=-1, keepdims=True))
    l = jnp.sum(p, axis=-1, keepdims=True)
    return jnp.dot(p.astype(BF16), v, preferred_element_type=F32) / l


def _ctx_attn_kernel(x_ref, mod_ref, g_ref, w_ref, o_ref, sk_ref, sv_ref, qkv_scr, *, nb, seq):
    mod = mod_ref[0]
    h = _rms_mod(x_ref[...], g_ref[...], _mod_chunk(mod, 0), _mod_chunk(mod, 1)).astype(BF16)
    qkv_scr[...] = jnp.dot(h, w_ref[...], preferred_element_type=F32)
    lo_half = lax.broadcasted_iota(jnp.int32, (1, LANES), 1) < HEAD_DIM
    for b in range(nb):
        rows = slice(b * seq, (b + 1) * seq)
        for hh in range(N_HEADS):
            sk_ref[b, 0, hh] = qkv_scr[rows, D_MODEL + hh * HEAD_DIM:D_MODEL + (hh + 1) * HEAD_DIM]
            sv_ref[b, 0, hh] = qkv_scr[rows, 2 * D_MODEL + hh * HEAD_DIM:2 * D_MODEL + (hh + 1) * HEAD_DIM]
        for hp in range(N_HEADS // HEADS_PER_BLOCK):
            cols = slice(hp * LANES, (hp + 1) * LANES)
            q2 = qkv_scr[rows, cols].astype(BF16)
            k2 = qkv_scr[rows, D_MODEL + hp * LANES:D_MODEL + (hp + 1) * LANES].astype(BF16)
            v2 = qkv_scr[rows, 2 * D_MODEL + hp * LANES:2 * D_MODEL + (hp + 1) * LANES].astype(BF16)
            o2 = None
            for e in range(HEADS_PER_BLOCK):
                msk = lo_half if e == 0 else jnp.logical_not(lo_half)
                qe = jnp.where(msk, q2, jnp.zeros_like(q2))
                ve = jnp.where(msk, v2, jnp.zeros_like(v2))
                s = lax.dot_general(qe, k2, (((1,), (1,)), ((), ())),
                                    preferred_element_type=F32) * ATTN_SCALE
                oe = _softmax_pv(s, ve)
                o2 = oe if o2 is None else o2 + oe
            o_ref[rows, cols] = o2.astype(BF16)


def _ctx_attn(x, mods3, mod_row, g, w_qkv, *, batch, seq):
    n, d = x.shape
    nb = 2
    tm = nb * seq
    kern = functools.partial(_ctx_attn_kernel, nb=nb, seq=seq)
    state = jax.ShapeDtypeStruct((batch, 1, N_HEADS, seq, HEAD_DIM), F32)
    state_spec = pl.BlockSpec((nb, 1, N_HEADS, seq, HEAD_DIM), lambda i: (i, 0, 0, 0, 0))
    return pl.pallas_call(
        kern,
        grid=(n // tm,),
        in_specs=[
            pl.BlockSpec((tm, d), lambda i: (i, 0)),
            pl.BlockSpec((1, 1, 6 * d), lambda i: (mod_row, 0, 0)),
            _resident((1, d)),
            _resident((d, 3 * d)),
        ],
        out_specs=[pl.BlockSpec((tm, d), lambda i: (i, 0)), state_spec, state_spec],
        out_shape=[jax.ShapeDtypeStruct((n, d), BF16), state, state],
        scratch_shapes=[pltpu.VMEM((tm, 3 * d), F32)],
        compiler_params=_params(),
        name="ctx_attn",
    )(x, mods3, g, w_qkv)


def _qkv_kernel(x_ref, mod_ref, g_ref, w_ref, o_ref, *, tn):
    mod = mod_ref[0]
    h = _rms_mod(x_ref[...], g_ref[...], _mod_chunk(mod, 0), _mod_chunk(mod, 1)).astype(BF16)
    for c in range(w_ref.shape[1] // tn):
        cols = slice(c * tn, (c + 1) * tn)
        o_ref[:, cols] = jnp.dot(h, w_ref[:, cols], preferred_element_type=F32).astype(BF16)


def _qkv(x, mods3, mod_row_fn, g, w_qkv, *, tm):
    n, d = x.shape
    nout = w_qkv.shape[1]
    return pl.pallas_call(
        functools.partial(_qkv_kernel, tn=512),
        grid=(n // tm,),
        in_specs=[
            pl.BlockSpec((tm, d), lambda i: (i, 0)),
            pl.BlockSpec((1, 1, 6 * d), lambda i: (mod_row_fn(i), 0, 0)),
            _resident((1, d)),
            _resident((d, nout)),
        ],
        out_specs=pl.BlockSpec((tm, nout), lambda i: (i, 0)),
        out_shape=jax.ShapeDtypeStruct((n, nout), BF16),
        compiler_params=_params(),
        name="qkv",
    )(x, mods3, g, w_qkv)


def _na_groups(rows):
    kh = min(NA_ROWS, rows)
    row_start = [min(max(r - kh // 2, 0), rows - kh) for r in range(rows)]
    groups, off = [], 0
    for r0 in range(0, rows, NA_QROWS):
        lo = min(row_start[r0:r0 + NA_QROWS])
        hi = max(row_start[r0:r0 + NA_QROWS]) + kh
        if (hi - lo) * GRID_W % LANES:
            if hi < rows:
                hi += 1
            else:
                lo -= 1
        nk = (hi - lo) * GRID_W
        groups.append((r0 * GRID_W, lo * GRID_W, nk, off))
        off += nk
    return tuple(groups), row_start, kh


def _na_bias(rpb, rows):
    groups, row_start, kh = _na_groups(rows)
    cols = np.arange(GRID_W)
    col_start = np.clip(cols - NA_COLS // 2, 0, GRID_W - NA_COLS)
    ro_idx, co_idx, valid = [], [], []
    for (q0, k0, nk, _) in groups:
        qr = q0 // GRID_W + np.arange(NA_QROWS * GRID_W) // GRID_W
        qc = np.arange(NA_QROWS * GRID_W) % GRID_W
        kr = k0 // GRID_W + np.arange(nk) // GRID_W
        kc = np.arange(nk) % GRID_W
        rs = np.asarray(row_start)[qr]
        cs = col_start[qc]
        ok = ((kr[None, :] >= rs[:, None]) & (kr[None, :] < rs[:, None] + kh)
              & (kc[None, :] >= cs[:, None]) & (kc[None, :] < cs[:, None] + NA_COLS))
        ro_idx.append(np.clip(kr[None, :] - qr[:, None] + NA_ROWS - 1, 0, 2 * NA_ROWS - 2))
        co_idx.append(np.clip(kc[None, :] - qc[:, None] + NA_COLS - 1, 0, 2 * NA_COLS - 2))
        valid.append(ok)
    ro = np.concatenate(ro_idx, axis=1)
    co = np.concatenate(co_idx, axis=1)
    ok = np.concatenate(valid, axis=1)
    return jnp.where(ok[None], rpb[:, ro, co], NEG_BIAS).astype(F32)


def _nattn_kernel(q_ref, k_ref, v_ref, ck_ref, cv_ref, bias_ref, o_ref, *, groups):
    q2, k2, v2 = q_ref[0], k_ref[0], v_ref[0]
    kc2 = jnp.concatenate([ck_ref[0, 0, 0], ck_ref[0, 0, 1]], axis=-1).astype(BF16)
    vc2 = jnp.concatenate([cv_ref[0, 0, 0], cv_ref[0, 0, 1]], axis=-1).astype(BF16)
    lo_half = lax.broadcasted_iota(jnp.int32, (1, LANES), 1) < HEAD_DIM
    nq = NA_QROWS * GRID_W
    for (q0, k0, nk, boff) in groups:
        o2 = None
        for e in range(HEADS_PER_BLOCK):
            msk = lo_half if e == 0 else jnp.logical_not(lo_half)
            qe = jnp.where(msk, q2[q0:q0 + nq], jnp.zeros((nq, LANES), BF16))
            ve = jnp.where(msk, v2[k0:k0 + nk], jnp.zeros((nk, LANES), BF16))
            vce = jnp.where(msk, vc2, jnp.zeros_like(vc2))
            dn = (((1,), (1,)), ((), ()))
            s_w = (lax.dot_general(qe, k2[k0:k0 + nk], dn, preferred_element_type=F32) * ATTN_SCALE
                   + bias_ref[e, :, boff:boff + nk])
            s_c = lax.dot_general(qe, kc2, dn, preferred_element_type=F32) * ATTN_SCALE
            mx = jnp.maximum(jnp.max(s_w, axis=-1, keepdims=True),
                             jnp.max(s_c, axis=-1, keepdims=True))
            p_w = jnp.exp(s_w - mx)
            p_c = jnp.exp(s_c - mx)
            l = jnp.sum(p_w, axis=-1, keepdims=True) + jnp.sum(p_c, axis=-1, keepdims=True)
            oe = (jnp.dot(p_w.astype(BF16), ve, preferred_element_type=F32)
                  + jnp.dot(p_c.astype(BF16), vce, preferred_element_type=F32)) / l
            o2 = oe if o2 is None else o2 + oe
        o_ref[0, q0:q0 + nq, :] = o2.astype(BF16)


def _nattn(qkv, cache_k, cache_v, bias, *, layer):
    b, t, _ = qkv.shape
    past = cache_k.shape[3]
    groups, _, _ = _na_groups(t // GRID_W)
    nhp = N_HEADS // HEADS_PER_BLOCK
    nq, nbias = bias.shape[1], bias.shape[2]
    cache_spec = pl.BlockSpec((1, 1, HEADS_PER_BLOCK, past, HEAD_DIM),
                              lambda hp, bi: (bi, layer, hp, 0, 0))
    return pl.pallas_call(
        functools.partial(_nattn_kernel, groups=groups),
        grid=(nhp, b),
        in_specs=[
            pl.BlockSpec((1, t, LANES), lambda hp, bi: (bi, 0, hp)),
            pl.BlockSpec((1, t, LANES), lambda hp, bi: (bi, 0, nhp + hp)),
            pl.BlockSpec((1, t, LANES), lambda hp, bi: (bi, 0, 2 * nhp + hp)),
            cache_spec,
            cache_spec,
            pl.BlockSpec((HEADS_PER_BLOCK, nq, nbias), lambda hp, bi: (hp, 0, 0)),
        ],
        out_specs=pl.BlockSpec((1, t, LANES), lambda hp, bi: (bi, 0, hp)),
        out_shape=jax.ShapeDtypeStruct((b, t, D_MODEL), BF16),
        compiler_params=_params(2),
        name="nattn",
    )(qkv, qkv, qkv, cache_k, cache_v, bias)


def _gmlp_kernel(x_ref, mod_ref, g_ref, win_ref, gv_ref, ws_ref, bs_ref, o_ref, vv_scr, *, tm):
    e = GMLP_WIDTH
    gd = e // GMLP_GROUPS
    mod = mod_ref[0]
    h = _rms_mod(x_ref[...], g_ref[...], _mod_chunk(mod, 0), _mod_chunk(mod, 1)).astype(BF16)
    zv = jax.nn.gelu(jnp.dot(h, win_ref[:, e:2 * e], preferred_element_type=F32))
    xc = zv - jnp.mean(zv, axis=-1, keepdims=True)
    vv = xc * lax.rsqrt(jnp.mean(xc * xc, axis=-1, keepdims=True) + EPS) * gv_ref[...]
    vv_scr[...] = vv.astype(BF16)
    pair = 2 * gd
    for gp in range(e // pair):
        u2 = jax.nn.gelu(jnp.dot(h, win_ref[:, gp * pair:(gp + 1) * pair],
                                 preferred_element_type=F32))
        for gi in range(2):
            g = 2 * gp + gi
            cols = slice(g * gd, (g + 1) * gd)
            for c in range(tm // CHUNK):
                rows = slice(c * CHUNK, (c + 1) * CHUNK)
                mixed = jnp.dot(ws_ref[g], vv_scr[rows, cols],
                                preferred_element_type=F32) + bs_ref[g]
                o_ref[rows, cols] = (u2[rows, gi * gd:(gi + 1) * gd] * mixed).astype(BF16)


def _gmlp(x, mods3, mod_row_fn, g, w_in, g_v, w_s, b_s_full, *, tm):
    n, d = x.shape
    e = GMLP_WIDTH
    return pl.pallas_call(
        functools.partial(_gmlp_kernel, tm=tm),
        grid=(n // tm,),
        in_specs=[
            pl.BlockSpec((tm, d), lambda i: (i, 0)),
            pl.BlockSpec((1, 1, 6 * d), lambda i: (mod_row_fn(i), 0, 0)),
            _resident((1, d)),
            _resident((d, 2 * e)),
            _resident((1, e)),
            _resident(w_s.shape),
            _resident(b_s_full.shape),
        ],
        out_specs=pl.BlockSpec((tm, e), lambda i: (i, 0)),
        out_shape=jax.ShapeDtypeStruct((n, e), BF16),
        scratch_shapes=[pltpu.VMEM((tm, e), BF16)],
        compiler_params=_params(),
        name="gmlp",
    )(x, mods3, g, w_in, g_v, w_s, b_s_full)


def _mix_ffn_kernel(x_ref, xp_ref, xn_ref, m_ref, mp_ref, mn_ref, mod_ref, gffn_ref, gfin_ref,
                    wmo_ref, wup_ref, wconv_ref, bconv_ref, wdown_ref, o_ref,
                    *, tm, seq, halo, final):
    mod = mod_ref[0]
    gate_mix, shift, scale, gate_ffn = (_mod_chunk(mod, k) for k in (2, 3, 4, 5))
    if halo:
        xcat = jnp.concatenate([xp_ref[0], x_ref[...], xn_ref[0]], axis=0)
        mcat = jnp.concatenate([mp_ref[0], m_ref[...], mn_ref[0]], axis=0)
        main = slice(HALO, HALO + tm)
    else:
        xcat, mcat = x_ref[...], m_ref[...]
        main = slice(0, tm)
    r = xcat.shape[0]
    x1 = xcat + gate_mix * jnp.dot(mcat, wmo_ref[...], preferred_element_type=F32)
    hcat = _rms_mod(x1, gffn_ref[...], shift, scale).astype(BF16)

    pos = (pl.program_id(0) * tm + lax.broadcasted_iota(jnp.int32, (tm, 1), 0)) % seq
    is_first = pos == 0
    is_last = pos == seq - 1
    f = FFN_TILE

    def step(c, acc):
        a = jnp.dot(hcat, wup_ref[c], preferred_element_type=F32)
        a_prev = pltpu.roll(a, 1, 0)[main]
        a_next = pltpu.roll(a, r - 1, 0)[main]
        wc = wconv_ref[c]
        y = (jnp.where(is_first, 0.0, a_prev) * wc[0:1] + a[main] * wc[1:2]
             + jnp.where(is_last, 0.0, a_next) * wc[2:3] + bconv_ref[c])
        act = (jax.nn.silu(y[:, :f]) * y[:, f:]).astype(BF16)
        return acc + jnp.dot(act, wdown_ref[c], preferred_element_type=F32)

    acc = lax.fori_loop(0, wup_ref.shape[0], step, jnp.zeros((tm, D_MODEL), F32))
    out = x1[main] + gate_ffn * acc
    if final:
        out = out * lax.rsqrt(jnp.mean(out * out, axis=-1, keepdims=True) + EPS) * gfin_ref[...]
    o_ref[...] = out


def _mix_ffn(x, m, mods3, mod_row_fn, g_ffn, g_fin, w_mo, w_up, w_conv, b_conv, w_down,
             *, tm, seq, final):
    n, d = x.shape
    km = m.shape[1]
    halo = tm % seq != 0
    assert not halo or seq % tm == 0
    nh = n // HALO
    per = tm // HALO
    prev_map = lambda i: (jnp.maximum(i * per - 1, 0), 0, 0)
    next_map = lambda i: (jnp.minimum((i + 1) * per, nh - 1), 0, 0)
    kern = functools.partial(_mix_ffn_kernel, tm=tm, seq=seq, halo=halo, final=final)
    return pl.pallas_call(
        kern,
        grid=(n // tm,),
        in_specs=[
            pl.BlockSpec((tm, d), lambda i: (i, 0)),
            pl.BlockSpec((1, HALO, d), prev_map),
            pl.BlockSpec((1, HALO, d), next_map),
            pl.BlockSpec((tm, km), lambda i: (i, 0)),
            pl.BlockSpec((1, HALO, km), prev_map),
            pl.BlockSpec((1, HALO, km), next_map),
            pl.BlockSpec((1, 1, 6 * d), lambda i: (mod_row_fn(i), 0, 0)),
            _resident((1, d)),
            _resident((1, d)),
            _resident(w_mo.shape),
            _resident(w_up.shape),
            _resident(w_conv.shape),
            _resident(b_conv.shape),
            _resident(w_down.shape),
        ],
        out_specs=pl.BlockSpec((tm, d), lambda i: (i, 0)),
        out_shape=jax.ShapeDtypeStruct((n, d), F32),
        compiler_params=_params(),
        name="mix_ffn",
    )(x, x.reshape(nh, HALO, d), x.reshape(nh, HALO, d),
      m, m.reshape(nh, HALO, km), m.reshape(nh, HALO, km),
      mods3, g_ffn, g_fin, w_mo, w_up, w_conv, b_conv, w_down)


def _ffn_weights(w_up, w_conv, b_conv, w_down):
    d, two_f = w_up.shape
    fdim = two_f // 2
    nt = fdim // FFN_TILE

    def regroup(w):
        lead = w.shape[:-1]
        w = w.reshape(lead + (2, nt, FFN_TILE))
        w = jnp.moveaxis(w, -2, 0)
        return w.reshape((nt,) + lead + (2 * FFN_TILE,))

    return (regroup(w_up).astype(BF16), regroup(w_conv), regroup(b_conv[None, :]),
            w_down.reshape(nt, FFN_TILE, d).astype(BF16))


def kernel(x_prompt, x_sample, cache_k, cache_v, c, c_ctx, w_ada, b_ada, norm_mix_g,
           norm_ffn_g, norm_final_g, w_qkv, w_attn_out, rpb, w_gmlp_in, g_gmlp_v,
           w_spatial, b_spatial, w_gmlp_out, w_ffn_up, w_ffn_conv, b_ffn_conv, w_ffn_down):
    batch, seq, d = x_prompt.shape
    dec_batch, dec_seq, _ = x_sample.shape
    depth = w_ada.shape[0]
    assert depth == 2 and d == D_MODEL

    cond = jnp.zeros((MOD_ROWS, d), F32).at[0].set(c_ctx).at[1:1 + dec_batch].set(c)
    mods3 = _ada(cond, w_ada, b_ada).reshape(depth * MOD_ROWS, 1, 6 * d)

    w_qkv_b = w_qkv[0].astype(BF16)
    w_ao_b = w_attn_out[0].astype(BF16)
    w_gin_b = w_gmlp_in[0].astype(BF16)
    w_gout_b = w_gmlp_out[0].astype(BF16)
    w_s_b = w_spatial[0].astype(BF16)
    b_s_full = jnp.broadcast_to(b_spatial[0][:, :, None], (GMLP_GROUPS, CHUNK, CHUNK))
    ffn_w = [_ffn_weights(w_ffn_up[l], w_ffn_conv[l], b_ffn_conv[l], w_ffn_down[l])
             for l in range(depth)]
    g_mix = norm_mix_g.reshape(depth, 1, d)
    g_ffn = norm_ffn_g.reshape(depth, 1, d)
    g_fin = norm_final_g.reshape(1, d)
    tm = 512

    def ctx_row(layer):
        return lambda i: layer * MOD_ROWS
    x = x_prompt.reshape(batch * seq, d)
    o, state_k, state_v = _ctx_attn(x, mods3, 0, g_mix[0], w_qkv_b, batch=batch, seq=seq)
    x = _mix_ffn(x, o, mods3, ctx_row(0), g_ffn[0], g_fin, w_ao_b, *ffn_w[0],
                 tm=tm, seq=seq, final=False)
    m = _gmlp(x, mods3, ctx_row(1), g_mix[1], w_gin_b, g_gmlp_v[0:1], w_s_b, b_s_full, tm=tm)
    y_prompt = _mix_ffn(x, m, mods3, ctx_row(1), g_ffn[1], g_fin, w_gout_b, *ffn_w[1],
                        tm=tm, seq=seq, final=True).reshape(batch, seq, d)

    def lat_row(layer):
        return lambda i: layer * MOD_ROWS + 1 + (i * tm) // dec_seq
    x = x_sample.reshape(dec_batch * dec_seq, d)
    qkv = _qkv(x, mods3, lat_row(0), g_mix[0], w_qkv_b, tm=tm)
    bias = _na_bias(rpb[0], dec_seq // GRID_W)
    o = _nattn(qkv.reshape(dec_batch, dec_seq, 3 * d), cache_k, cache_v, bias, layer=0)
    x = _mix_ffn(x, o.reshape(dec_batch * dec_seq, d), mods3, lat_row(0), g_ffn[0], g_fin,
                 w_ao_b, *ffn_w[0], tm=tm, seq=dec_seq, final=False)
    m = _gmlp(x, mods3, lat_row(1), g_mix[1], w_gin_b, g_gmlp_v[0:1], w_s_b, b_s_full, tm=tm)
    y_sample = _mix_ffn(x, m, mods3, lat_row(1), g_ffn[1], g_fin, w_gout_b, *ffn_w[1],
                        tm=tm, seq=dec_seq, final=True).reshape(dec_batch, dec_seq, d)

    return (y_prompt, y_sample, state_k, state_v)
```

```python
import functools

import numpy as np
import jax
import jax.numpy as jnp
from jax import lax
from jax.experimental import pallas as pl
from jax.experimental.pallas import tpu as pltpu

D_MODEL = 1024
N_HEADS = 16
HEAD_DIM = D_MODEL // N_HEADS
GRID_W = 64
NA_ROWS = 8
NA_COLS = 16
CHUNK = 128
GMLP_WIDTH = 2 * D_MODEL
GMLP_GROUPS = 16
FFN_DIM = 2816
CONV_W = 3
EPS = 1e-6
ATTN_SCALE = HEAD_DIM ** -0.5

LANES = 128
HEADS_PER_BLOCK = LANES // HEAD_DIM
MOD_ROWS = 8
HALO = 16
FFN_TILE = 256
NA_QROWS = 4
NEG_BIAS = -1e30
VMEM_LIMIT = 56 * 1024 * 1024

BF16 = jnp.bfloat16
F32 = jnp.float32


def _params(n_axes=1):
    return pltpu.CompilerParams(
        dimension_semantics=("arbitrary",) * n_axes, vmem_limit_bytes=VMEM_LIMIT)


def _resident(shape):
    nd = len(shape)
    return pl.BlockSpec(shape, lambda *_: (0,) * nd, pipeline_mode=pl.Buffered(1))


def _rms_mod(x, g, shift, scale):
    y = x * lax.rsqrt(jnp.mean(x * x, axis=-1, keepdims=True) + EPS)
    return (y * g) * (1 + scale) + shift


def _mod_chunk(mod, k):
    return mod[:, k * D_MODEL:(k + 1) * D_MODEL]


def _ada_kernel(cond_ref, w_ref, b_ref, o_ref):
    s = jax.nn.silu(cond_ref[...]).astype(BF16)
    o_ref[0] = jnp.dot(s, w_ref[0].astype(BF16), preferred_element_type=F32) + b_ref[0]


def _ada(cond, w_ada, b_ada):
    depth, d, n = w_ada.shape
    tn = 1536
    return pl.pallas_call(
        _ada_kernel,
        grid=(depth, n // tn),
        in_specs=[
            pl.BlockSpec((MOD_ROWS, d), lambda l, j: (0, 0)),
            pl.BlockSpec((1, d, tn), lambda l, j: (l, 0, j)),
            pl.BlockSpec((1, 1, tn), lambda l, j: (l, 0, j)),
        ],
        out_specs=pl.BlockSpec((1, MOD_ROWS, tn), lambda l, j: (l, 0, j)),
        out_shape=jax.ShapeDtypeStruct((depth, MOD_ROWS, n), F32),
        compiler_params=_params(2),
        name="ada",
    )(cond, w_ada, b_ada.reshape(depth, 1, n))


def _softmax_pv(s, v):
    p = jnp.exp(s - jnp.max(s, axis=-1, keepdims=True))
    l = jnp.sum(p, axis=-1, keepdims=True)
    return jnp.dot(p.astype(BF16), v, preferred_element_type=F32) / l


def _ctx_attn_kernel(x_ref, mod_ref, g_ref, w_ref, o_ref, sk_ref, sv_ref, qkv_scr, *, nb, seq):
    mod = mod_ref[0]
    h = _rms_mod(x_ref[...], g_ref[...], _mod_chunk(mod, 0), _mod_chunk(mod, 1)).astype(BF16)
    qkv_scr[...] = jnp.dot(h, w_ref[...], preferred_element_type=F32)
    lo_half = lax.broadcasted_iota(jnp.int32, (1, LANES), 1) < HEAD_DIM
    for b in range(nb):
        rows = slice(b * seq, (b + 1) * seq)
        for hh in range(N_HEADS):
            sk_ref[b, 0, hh] = qkv_scr[rows, D_MODEL + hh * HEAD_DIM:D_MODEL + (hh + 1) * HEAD_DIM]
            sv_ref[b, 0, hh] = qkv_scr[rows, 2 * D_MODEL + hh * HEAD_DIM:2 * D_MODEL + (hh + 1) * HEAD_DIM]
        for hp in range(N_HEADS // HEADS_PER_BLOCK):
            cols = slice(hp * LANES, (hp + 1) * LANES)
            q2 = qkv_scr[rows, cols].astype(BF16)
            k2 = qkv_scr[rows, D_MODEL + hp * LANES:D_MODEL + (hp + 1) * LANES].astype(BF16)
            v2 = qkv_scr[rows, 2 * D_MODEL + hp * LANES:2 * D_MODEL + (hp + 1) * LANES].astype(BF16)
            o2 = None
            for e in range(HEADS_PER_BLOCK):
                msk = lo_half if e == 0 else jnp.logical_not(lo_half)
                qe = jnp.where(msk, q2, jnp.zeros_like(q2))
                ve = jnp.where(msk, v2, jnp.zeros_like(v2))
                s = lax.dot_general(qe, k2, (((1,), (1,)), ((), ())),
                                    preferred_element_type=F32) * ATTN_SCALE
                oe = _softmax_pv(s, ve)
                o2 = oe if o2 is None else o2 + oe
            o_ref[rows, cols] = o2.astype(BF16)


def _ctx_attn(x, mods3, mod_row, g, w_qkv, *, batch, seq):
    n, d = x.shape
    nb = 2
    tm = nb * seq
    kern = functools.partial(_ctx_attn_kernel, nb=nb, seq=seq)
    state = jax.ShapeDtypeStruct((batch, 1, N_HEADS, seq, HEAD_DIM), F32)
    state_spec = pl.BlockSpec((nb, 1, N_HEADS, seq, HEAD_DIM), lambda i: (i, 0, 0, 0, 0))
    return pl.pallas_call(
        kern,
        grid=(n // tm,),
        in_specs=[
            pl.BlockSpec((tm, d), lambda i: (i, 0)),
            pl.BlockSpec((1, 1, 6 * d), lambda i: (mod_row, 0, 0)),
            _resident((1, d)),
            _resident((d, 3 * d)),
        ],
        out_specs=[pl.BlockSpec((tm, d), lambda i: (i, 0)), state_spec, state_spec],
        out_shape=[jax.ShapeDtypeStruct((n, d), BF16), state, state],
        scratch_shapes=[pltpu.VMEM((tm, 3 * d), F32)],
        compiler_params=_params(),
        name="ctx_attn",
    )(x, mods3, g, w_qkv)


def _qkv_kernel(x_ref, mod_ref, g_ref, w_ref, o_ref, *, tn):
    mod = mod_ref[0]
    h = _rms_mod(x_ref[...], g_ref[...], _mod_chunk(mod, 0), _mod_chunk(mod, 1)).astype(BF16)
    for c in range(w_ref.shape[1] // tn):
        cols = slice(c * tn, (c + 1) * tn)
        o_ref[:, cols] = jnp.dot(h, w_ref[:, cols], preferred_element_type=F32).astype(BF16)


def _qkv(x, mods3, mod_row_fn, g, w_qkv, *, tm):
    n, d = x.shape
    nout = w_qkv.shape[1]
    return pl.pallas_call(
        functools.partial(_qkv_kernel, tn=512),
        grid=(n // tm,),
        in_specs=[
            pl.BlockSpec((tm, d), lambda i: (i, 0)),
            pl.BlockSpec((1, 1, 6 * d), lambda i: (mod_row_fn(i), 0, 0)),
            _resident((1, d)),
            _resident((d, nout)),
        ],
        out_specs=pl.BlockSpec((tm, nout), lambda i: (i, 0)),
        out_shape=jax.ShapeDtypeStruct((n, nout), BF16),
        compiler_params=_params(),
        name="qkv",
    )(x, mods3, g, w_qkv)


def _na_groups(rows):
    kh = min(NA_ROWS, rows)
    row_start = [min(max(r - kh // 2, 0), rows - kh) for r in range(rows)]
    groups, off = [], 0
    for r0 in range(0, rows, NA_QROWS):
        lo = min(row_start[r0:r0 + NA_QROWS])
        hi = max(row_start[r0:r0 + NA_QROWS]) + kh
        if (hi - lo) * GRID_W % LANES:
            if hi < rows:
                hi += 1
            else:
                lo -= 1
        nk = (hi - lo) * GRID_W
        groups.append((r0 * GRID_W, lo * GRID_W, nk, off))
        off += nk
    return tuple(groups), row_start, kh


def _na_bias(rpb, rows):
    groups, row_start, kh = _na_groups(rows)
    nh, nro, nco = rpb.shape
    w = GRID_W
    lead = w - NA_COLS
    padded = jnp.pad(rpb, ((0, 0), (0, 0), (lead, 2 * w - nco - lead)))
    skew = jnp.tile(padded, (1, 1, w))[..., :w * (2 * w - 1)].reshape(nh, nro, w, 2 * w - 1)
    tile = skew[..., w - 1:2 * w - 1]
    cols = np.arange(w)
    col_start = np.clip(cols - NA_COLS // 2, 0, w - NA_COLS)
    col_ok = (cols[None, :] >= col_start[:, None]) & (cols[None, :] < col_start[:, None] + NA_COLS)
    tile = jnp.where(col_ok, tile, NEG_BIAS)
    tile = jnp.concatenate([tile, jnp.full((nh, 1, w, w), NEG_BIAS, F32)], axis=1)
    blocks = []
    for (q0, k0, nk, _) in groups:
        per_row = []
        for qr in range(q0 // w, q0 // w + NA_QROWS):
            rs = row_start[qr]
            idx = [kr - qr + NA_ROWS - 1 if rs <= kr < rs + kh else nro
                   for kr in range(k0 // w, (k0 + nk) // w)]
            per_row.append(jnp.concatenate([tile[:, a] for a in idx], axis=-1))
        blocks.append(jnp.concatenate(per_row, axis=1))
    return jnp.concatenate(blocks, axis=-1)


def _nattn_kernel(q_ref, k_ref, v_ref, ck_ref, cv_ref, bias_ref, o_ref, *, groups):
    q2, k2, v2 = q_ref[0], k_ref[0], v_ref[0]
    kc2 = jnp.concatenate([ck_ref[0, 0, 0], ck_ref[0, 0, 1]], axis=-1).astype(BF16)
    vc2 = jnp.concatenate([cv_ref[0, 0, 0], cv_ref[0, 0, 1]], axis=-1).astype(BF16)
    lo_half = lax.broadcasted_iota(jnp.int32, (1, LANES), 1) < HEAD_DIM
    nq = NA_QROWS * GRID_W
    for (q0, k0, nk, boff) in groups:
        o2 = None
        for e in range(HEADS_PER_BLOCK):
            msk = lo_half if e == 0 else jnp.logical_not(lo_half)
            qe = jnp.where(msk, q2[q0:q0 + nq], jnp.zeros((nq, LANES), BF16))
            ve = jnp.where(msk, v2[k0:k0 + nk], jnp.zeros((nk, LANES), BF16))
            vce = jnp.where(msk, vc2, jnp.zeros_like(vc2))
            dn = (((1,), (1,)), ((), ()))
            s_w = (lax.dot_general(qe, k2[k0:k0 + nk], dn, preferred_element_type=F32) * ATTN_SCALE
                   + bias_ref[e, :, boff:boff + nk])
            s_c = lax.dot_general(qe, kc2, dn, preferred_element_type=F32) * ATTN_SCALE
            mx = jnp.maximum(jnp.max(s_w, axis=-1, keepdims=True),
                             jnp.max(s_c, axis=-1, keepdims=True))
            p_w = jnp.exp(s_w - mx)
            p_c = jnp.exp(s_c - mx)
            l = jnp.sum(p_w, axis=-1, keepdims=True) + jnp.sum(p_c, axis=-1, keepdims=True)
            oe = (jnp.dot(p_w.astype(BF16), ve, preferred_element_type=F32)
                  + jnp.dot(p_c.astype(BF16), vce, preferred_element_type=F32)) / l
            o2 = oe if o2 is None else o2 + oe
        o_ref[0, q0:q0 + nq, :] = o2.astype(BF16)


def _nattn(qkv, cache_k, cache_v, bias, *, layer):
    b, t, _ = qkv.shape
    past = cache_k.shape[3]
    groups, _, _ = _na_groups(t // GRID_W)
    nhp = N_HEADS // HEADS_PER_BLOCK
    nq, nbias = bias.shape[1], bias.shape[2]
    cache_spec = pl.BlockSpec((1, 1, HEADS_PER_BLOCK, past, HEAD_DIM),
                              lambda hp, bi: (bi, layer, hp, 0, 0))
    return pl.pallas_call(
        functools.partial(_nattn_kernel, groups=groups),
        grid=(nhp, b),
        in_specs=[
            pl.BlockSpec((1, t, LANES), lambda hp, bi: (bi, 0, hp)),
            pl.BlockSpec((1, t, LANES), lambda hp, bi: (bi, 0, nhp + hp)),
            pl.BlockSpec((1, t, LANES), lambda hp, bi: (bi, 0, 2 * nhp + hp)),
            cache_spec,
            cache_spec,
            pl.BlockSpec((HEADS_PER_BLOCK, nq, nbias), lambda hp, bi: (hp, 0, 0)),
        ],
        out_specs=pl.BlockSpec((1, t, LANES), lambda hp, bi: (bi, 0, hp)),
        out_shape=jax.ShapeDtypeStruct((b, t, D_MODEL), BF16),
        compiler_params=_params(2),
        name="nattn",
    )(qkv, qkv, qkv, cache_k, cache_v, bias)


def _gmlp_kernel(x_ref, mod_ref, g_ref, win_ref, gv_ref, ws_ref, bs_ref, o_ref, vv_scr, *, tm):
    e = GMLP_WIDTH
    gd = e // GMLP_GROUPS
    mod = mod_ref[0]
    h = _rms_mod(x_ref[...], g_ref[...], _mod_chunk(mod, 0), _mod_chunk(mod, 1)).astype(BF16)
    zv = jax.nn.gelu(jnp.dot(h, win_ref[:, e:2 * e], preferred_element_type=F32))
    xc = zv - jnp.mean(zv, axis=-1, keepdims=True)
    vv = xc * lax.rsqrt(jnp.mean(xc * xc, axis=-1, keepdims=True) + EPS) * gv_ref[...]
    vv_scr[...] = vv.astype(BF16)
    pair = 2 * gd
    for gp in range(e // pair):
        u2 = jax.nn.gelu(jnp.dot(h, win_ref[:, gp * pair:(gp + 1) * pair],
                                 preferred_element_type=F32))
        for gi in range(2):
            g = 2 * gp + gi
            cols = slice(g * gd, (g + 1) * gd)
            for c in range(tm // CHUNK):
                rows = slice(c * CHUNK, (c + 1) * CHUNK)
                mixed = jnp.dot(ws_ref[g], vv_scr[rows, cols],
                                preferred_element_type=F32) + bs_ref[g]
                o_ref[rows, cols] = (u2[rows, gi * gd:(gi + 1) * gd] * mixed).astype(BF16)


def _gmlp(x, mods3, mod_row_fn, g, w_in, g_v, w_s, b_s_full, *, tm):
    n, d = x.shape
    e = GMLP_WIDTH
    return pl.pallas_call(
        functools.partial(_gmlp_kernel, tm=tm),
        grid=(n // tm,),
        in_specs=[
            pl.BlockSpec((tm, d), lambda i: (i, 0)),
            pl.BlockSpec((1, 1, 6 * d), lambda i: (mod_row_fn(i), 0, 0)),
            _resident((1, d)),
            _resident((d, 2 * e)),
            _resident((1, e)),
            _resident(w_s.shape),
            _resident(b_s_full.shape),
        ],
        out_specs=pl.BlockSpec((tm, e), lambda i: (i, 0)),
        out_shape=jax.ShapeDtypeStruct((n, e), BF16),
        scratch_shapes=[pltpu.VMEM((tm, e), BF16)],
        compiler_params=_params(),
        name="gmlp",
    )(x, mods3, g, w_in, g_v, w_s, b_s_full)


def _mix_ffn_kernel(x_ref, xp_ref, xn_ref, m_ref, mp_ref, mn_ref, mod_ref, gffn_ref, gfin_ref,
                    wmo_ref, wup_ref, wconv_ref, bconv_ref, wdown_ref, o_ref,
                    *, tm, seq, halo, final):
    mod = mod_ref[0]
    gate_mix, shift, scale, gate_ffn = (_mod_chunk(mod, k) for k in (2, 3, 4, 5))
    if halo:
        xcat = jnp.concatenate([xp_ref[0], x_ref[...], xn_ref[0]], axis=0)
        mcat = jnp.concatenate([mp_ref[0], m_ref[...], mn_ref[0]], axis=0)
        main = slice(HALO, HALO + tm)
    else:
        xcat, mcat = x_ref[...], m_ref[...]
        main = slice(0, tm)
    r = xcat.shape[0]
    x1 = xcat + gate_mix * jnp.dot(mcat, wmo_ref[...], preferred_element_type=F32)
    hcat = _rms_mod(x1, gffn_ref[...], shift, scale).astype(BF16)

    pos = (pl.program_id(0) * tm + lax.broadcasted_iota(jnp.int32, (tm, 1), 0)) % seq
    is_first = pos == 0
    is_last = pos == seq - 1
    f = FFN_TILE

    def up_conv(col):
        cols = pl.ds(pl.multiple_of(col, f), f)
        a = jnp.dot(hcat, wup_ref[:, cols], preferred_element_type=F32)
        a_prev = pltpu.roll(a, 1, 0)[main]
        a_next = pltpu.roll(a, r - 1, 0)[main]
        wc = wconv_ref[:, cols]
        return (jnp.where(is_first, 0.0, a_prev) * wc[0:1] + a[main] * wc[1:2]
                + jnp.where(is_last, 0.0, a_next) * wc[2:3] + bconv_ref[:, cols])

    def step(c, acc):
        act = (jax.nn.silu(up_conv(c * f)) * up_conv(FFN_DIM + c * f)).astype(BF16)
        return acc + jnp.dot(act, wdown_ref[c], preferred_element_type=F32)

    acc = lax.fori_loop(0, wdown_ref.shape[0], step, jnp.zeros((tm, D_MODEL), F32))
    out = x1[main] + gate_ffn * acc
    if final:
        out = out * lax.rsqrt(jnp.mean(out * out, axis=-1, keepdims=True) + EPS) * gfin_ref[...]
    o_ref[...] = out


def _mix_ffn(x, m, mods3, mod_row_fn, g_ffn, g_fin, w_mo, w_up, w_conv, b_conv, w_down,
             *, tm, seq, final):
    n, d = x.shape
    km = m.shape[1]
    halo = tm % seq != 0
    assert not halo or seq % tm == 0
    nh = n // HALO
    per = tm // HALO
    prev_map = lambda i: (jnp.maximum(i * per - 1, 0), 0, 0)
    next_map = lambda i: (jnp.minimum((i + 1) * per, nh - 1), 0, 0)
    kern = functools.partial(_mix_ffn_kernel, tm=tm, seq=seq, halo=halo, final=final)
    return pl.pallas_call(
        kern,
        grid=(n // tm,),
        in_specs=[
            pl.BlockSpec((tm, d), lambda i: (i, 0)),
            pl.BlockSpec((1, HALO, d), prev_map),
            pl.BlockSpec((1, HALO, d), next_map),
            pl.BlockSpec((tm, km), lambda i: (i, 0)),
            pl.BlockSpec((1, HALO, km), prev_map),
            pl.BlockSpec((1, HALO, km), next_map),
            pl.BlockSpec((1, 1, 6 * d), lambda i: (mod_row_fn(i), 0, 0)),
            _resident((1, d)),
            _resident((1, d)),
            _resident(w_mo.shape),
            _resident(w_up.shape),
            _resident(w_conv.shape),
            _resident(b_conv.shape),
            _resident(w_down.shape),
        ],
        out_specs=pl.BlockSpec((tm, d), lambda i: (i, 0)),
        out_shape=jax.ShapeDtypeStruct((n, d), F32),
        compiler_params=_params(),
        name="mix_ffn",
    )(x, x.reshape(nh, HALO, d), x.reshape(nh, HALO, d),
      m, m.reshape(nh, HALO, km), m.reshape(nh, HALO, km),
      mods3, g_ffn, g_fin, w_mo, w_up, w_conv, b_conv, w_down)


def _ffn_weights(w_up, w_conv, b_conv, w_down):
    fdim, d = w_down.shape
    return (w_up.astype(BF16), w_conv, b_conv[None, :],
            w_down.reshape(fdim // FFN_TILE, FFN_TILE, d).astype(BF16))


def kernel(x_prompt, x_sample, cache_k, cache_v, c, c_ctx, w_ada, b_ada, norm_mix_g,
           norm_ffn_g, norm_final_g, w_qkv, w_attn_out, rpb, w_gmlp_in, g_gmlp_v,
           w_spatial, b_spatial, w_gmlp_out, w_ffn_up, w_ffn_conv, b_ffn_conv, w_ffn_down):
    batch, seq, d = x_prompt.shape
    dec_batch, dec_seq, _ = x_sample.shape
    depth = w_ada.shape[0]
    assert depth == 2 and d == D_MODEL

    cond = jnp.zeros((MOD_ROWS, d), F32).at[0].set(c_ctx).at[1:1 + dec_batch].set(c)
    mods3 = _ada(cond, w_ada, b_ada).reshape(depth * MOD_ROWS, 1, 6 * d)

    w_qkv_b = w_qkv[0].astype(BF16)
    w_ao_b = w_attn_out[0].astype(BF16)
    w_gin_b = w_gmlp_in[0].astype(BF16)
    w_gout_b = w_gmlp_out[0].astype(BF16)
    w_s_b = w_spatial[0].astype(BF16)
    b_s_full = jnp.broadcast_to(b_spatial[0][:, :, None], (GMLP_GROUPS, CHUNK, CHUNK))
    ffn_w = [_ffn_weights(w_ffn_up[l], w_ffn_conv[l], b_ffn_conv[l], w_ffn_down[l])
             for l in range(depth)]
    g_mix = norm_mix_g.reshape(depth, 1, d)
    g_ffn = norm_ffn_g.reshape(depth, 1, d)
    g_fin = norm_final_g.reshape(1, d)
    tm = 512

    def ctx_row(layer):
        return lambda i: layer * MOD_ROWS
    x = x_prompt.reshape(batch * seq, d)
    o, state_k, state_v = _ctx_attn(x, mods3, 0, g_mix[0], w_qkv_b, batch=batch, seq=seq)
    x = _mix_ffn(x, o, mods3, ctx_row(0), g_ffn[0], g_fin, w_ao_b, *ffn_w[0],
                 tm=tm, seq=seq, final=False)
    m = _gmlp(x, mods3, ctx_row(1), g_mix[1], w_gin_b, g_gmlp_v[0:1], w_s_b, b_s_full, tm=tm)
    y_prompt = _mix_ffn(x, m, mods3, ctx_row(1), g_ffn[1], g_fin, w_gout_b, *ffn_w[1],
                        tm=tm, seq=seq, final=True).reshape(batch, seq, d)

    def lat_row(layer):
        return lambda i: layer * MOD_ROWS + 1 + (i * tm) // dec_seq
    x = x_sample.reshape(dec_batch * dec_seq, d)
    qkv = _qkv(x, mods3, lat_row(0), g_mix[0], w_qkv_b, tm=tm)
    bias = _na_bias(rpb[0], dec_seq // GRID_W)
    o = _nattn(qkv.reshape(dec_batch, dec_seq, 3 * d), cache_k, cache_v, bias, layer=0)
    x = _mix_ffn(x, o.reshape(dec_batch * dec_seq, d), mods3, lat_row(0), g_ffn[0], g_fin,
                 w_ao_b, *ffn_w[0], tm=tm, seq=dec_seq, final=False)
    m = _gmlp(x, mods3, lat_row(1), g_mix[1], w_gin_b, g_gmlp_v[0:1], w_s_b, b_s_full, tm=tm)
    y_sample = _mix_ffn(x, m, mods3, lat_row(1), g_ffn[1], g_fin, w_gout_b, *ffn_w[1],
                        tm=tm, seq=dec_seq, final=True).reshape(dec_batch, dec_seq, d)

    return (y_prompt, y_sample, state_k, state_v)
```

```python
import functools

import numpy as np
import jax
import jax.numpy as jnp
from jax import lax
from jax.experimental import pallas as pl
from jax.experimental.pallas import tpu as pltpu

D_MODEL = 1024
N_HEADS = 16
HEAD_DIM = D_MODEL // N_HEADS
GRID_W = 64
NA_ROWS = 8
NA_COLS = 16
CHUNK = 128
GMLP_WIDTH = 2 * D_MODEL
GMLP_GROUPS = 16
FFN_DIM = 2816
CONV_W = 3
EPS = 1e-6
ATTN_SCALE = HEAD_DIM ** -0.5

LANES = 128
HEADS_PER_BLOCK = LANES // HEAD_DIM
MOD_ROWS = 8
HALO = 16
PAD = 8
FFN_TILE = 256
NA_QROWS = 4
NEG_BIAS = -1e30
VMEM_LIMIT = 56 * 1024 * 1024

BF16 = jnp.bfloat16
F32 = jnp.float32


def _params(n_axes=1):
    return pltpu.CompilerParams(
        dimension_semantics=("arbitrary",) * n_axes, vmem_limit_bytes=VMEM_LIMIT)


def _resident(shape):
    nd = len(shape)
    return pl.BlockSpec(shape, lambda *_: (0,) * nd, pipeline_mode=pl.Buffered(1))


def _rms_mod(x, g, shift, scale):
    y = x * lax.rsqrt(jnp.mean(x * x, axis=-1, keepdims=True) + EPS)
    return (y * g) * (1 + scale) + shift


def _mod_chunk(mod, k):
    return mod[:, k * D_MODEL:(k + 1) * D_MODEL]


def _ada_kernel(cond_ref, w_ref, b_ref, o_ref):
    s = jax.nn.silu(cond_ref[...]).astype(BF16)
    o_ref[0] = jnp.dot(s, w_ref[0].astype(BF16), preferred_element_type=F32) + b_ref[0]


def _ada(cond, w_ada, b_ada):
    depth, d, n = w_ada.shape
    tn = 1536
    return pl.pallas_call(
        _ada_kernel,
        grid=(depth, n // tn),
        in_specs=[
            pl.BlockSpec((MOD_ROWS, d), lambda l, j: (0, 0)),
            pl.BlockSpec((1, d, tn), lambda l, j: (l, 0, j)),
            pl.BlockSpec((1, 1, tn), lambda l, j: (l, 0, j)),
        ],
        out_specs=pl.BlockSpec((1, MOD_ROWS, tn), lambda l, j: (l, 0, j)),
        out_shape=jax.ShapeDtypeStruct((depth, MOD_ROWS, n), F32),
        compiler_params=_params(2),
        name="ada",
    )(cond, w_ada, b_ada.reshape(depth, 1, n))


def _softmax_pv(s, v):
    p = jnp.exp(s - jnp.max(s, axis=-1, keepdims=True))
    l = jnp.sum(p, axis=-1, keepdims=True)
    return jnp.dot(p.astype(BF16), v, preferred_element_type=F32) / l


def _ctx_attn_kernel(x_ref, mod_ref, g_ref, w_ref, o_ref, sk_ref, sv_ref, qkv_scr, *, nb, seq):
    mod = mod_ref[0]
    h = _rms_mod(x_ref[...], g_ref[...], _mod_chunk(mod, 0), _mod_chunk(mod, 1)).astype(BF16)
    qkv_scr[...] = jnp.dot(h, w_ref[...], preferred_element_type=F32)
    lo_half = lax.broadcasted_iota(jnp.int32, (1, LANES), 1) < HEAD_DIM
    for b in range(nb):
        rows = slice(b * seq, (b + 1) * seq)
        for hh in range(N_HEADS):
            sk_ref[b, 0, hh] = qkv_scr[rows, D_MODEL + hh * HEAD_DIM:D_MODEL + (hh + 1) * HEAD_DIM]
            sv_ref[b, 0, hh] = qkv_scr[rows, 2 * D_MODEL + hh * HEAD_DIM:2 * D_MODEL + (hh + 1) * HEAD_DIM]
        for hp in range(N_HEADS // HEADS_PER_BLOCK):
            cols = slice(hp * LANES, (hp + 1) * LANES)
            q2 = qkv_scr[rows, cols].astype(BF16)
            k2 = qkv_scr[rows, D_MODEL + hp * LANES:D_MODEL + (hp + 1) * LANES].astype(BF16)
            v2 = qkv_scr[rows, 2 * D_MODEL + hp * LANES:2 * D_MODEL + (hp + 1) * LANES].astype(BF16)
            o2 = None
            for e in range(HEADS_PER_BLOCK):
                msk = lo_half if e == 0 else jnp.logical_not(lo_half)
                qe = jnp.where(msk, q2, jnp.zeros_like(q2))
                ve = jnp.where(msk, v2, jnp.zeros_like(v2))
                s = lax.dot_general(qe, k2, (((1,), (1,)), ((), ())),
                                    preferred_element_type=F32) * ATTN_SCALE
                oe = _softmax_pv(s, ve)
                o2 = oe if o2 is None else o2 + oe
            o_ref[rows, cols] = o2.astype(BF16)


def _ctx_attn(x, mods3, mod_row, g, w_qkv, *, batch, seq):
    n, d = x.shape
    nb = 2
    tm = nb * seq
    kern = functools.partial(_ctx_attn_kernel, nb=nb, seq=seq)
    state = jax.ShapeDtypeStruct((batch, 1, N_HEADS, seq, HEAD_DIM), F32)
    state_spec = pl.BlockSpec((nb, 1, N_HEADS, seq, HEAD_DIM), lambda i: (i, 0, 0, 0, 0))
    return pl.pallas_call(
        kern,
        grid=(n // tm,),
        in_specs=[
            pl.BlockSpec((tm, d), lambda i: (i, 0)),
            pl.BlockSpec((1, 1, 6 * d), lambda i: (mod_row, 0, 0)),
            _resident((1, d)),
            _resident((d, 3 * d)),
        ],
        out_specs=[pl.BlockSpec((tm, d), lambda i: (i, 0)), state_spec, state_spec],
        out_shape=[jax.ShapeDtypeStruct((n, d), BF16), state, state],
        scratch_shapes=[pltpu.VMEM((tm, 3 * d), F32)],
        compiler_params=_params(),
        name="ctx_attn",
    )(x, mods3, g, w_qkv)


def _qkv_kernel(x_ref, mod_ref, g_ref, w_ref, o_ref, *, tn):
    mod = mod_ref[0]
    h = _rms_mod(x_ref[...], g_ref[...], _mod_chunk(mod, 0), _mod_chunk(mod, 1)).astype(BF16)
    for c in range(w_ref.shape[1] // tn):
        cols = slice(c * tn, (c + 1) * tn)
        o_ref[:, cols] = jnp.dot(h, w_ref[:, cols], preferred_element_type=F32).astype(BF16)


def _qkv(x, mods3, mod_row_fn, g, w_qkv, *, tm):
    n, d = x.shape
    nout = w_qkv.shape[1]
    return pl.pallas_call(
        functools.partial(_qkv_kernel, tn=512),
        grid=(n // tm,),
        in_specs=[
            pl.BlockSpec((tm, d), lambda i: (i, 0)),
            pl.BlockSpec((1, 1, 6 * d), lambda i: (mod_row_fn(i), 0, 0)),
            _resident((1, d)),
            _resident((d, nout)),
        ],
        out_specs=pl.BlockSpec((tm, nout), lambda i: (i, 0)),
        out_shape=jax.ShapeDtypeStruct((n, nout), BF16),
        compiler_params=_params(),
        name="qkv",
    )(x, mods3, g, w_qkv)


def _na_groups(rows):
    kh = min(NA_ROWS, rows)
    row_start = [min(max(r - kh // 2, 0), rows - kh) for r in range(rows)]
    groups, off = [], 0
    for r0 in range(0, rows, NA_QROWS):
        lo = min(row_start[r0:r0 + NA_QROWS])
        hi = max(row_start[r0:r0 + NA_QROWS]) + kh
        if (hi - lo) * GRID_W % LANES:
            if hi < rows:
                hi += 1
            else:
                lo -= 1
        nk = (hi - lo) * GRID_W
        groups.append((r0 * GRID_W, lo * GRID_W, nk, off))
        off += nk
    return tuple(groups), row_start, kh


def _na_bias(rpb, rows):
    groups, row_start, kh = _na_groups(rows)
    nh, nro, nco = rpb.shape
    w = GRID_W
    lead = w - NA_COLS
    padded = jnp.pad(rpb, ((0, 0), (0, 0), (lead, 2 * w - nco - lead)))
    skew = jnp.tile(padded, (1, 1, w))[..., :w * (2 * w - 1)].reshape(nh, nro, w, 2 * w - 1)
    tile = skew[..., w - 1:2 * w - 1]
    cols = np.arange(w)
    col_start = np.clip(cols - NA_COLS // 2, 0, w - NA_COLS)
    col_ok = (cols[None, :] >= col_start[:, None]) & (cols[None, :] < col_start[:, None] + NA_COLS)
    tile = jnp.where(col_ok, tile, NEG_BIAS)
    tile = jnp.concatenate([tile, jnp.full((nh, 1, w, w), NEG_BIAS, F32)], axis=1)
    blocks = []
    for (q0, k0, nk, _) in groups:
        per_row = []
        for qr in range(q0 // w, q0 // w + NA_QROWS):
            rs = row_start[qr]
            idx = [kr - qr + NA_ROWS - 1 if rs <= kr < rs + kh else nro
                   for kr in range(k0 // w, (k0 + nk) // w)]
            per_row.append(jnp.concatenate([tile[:, a] for a in idx], axis=-1))
        blocks.append(jnp.concatenate(per_row, axis=1))
    return jnp.concatenate(blocks, axis=-1)


def _nattn_kernel(q_ref, k_ref, v_ref, ck_ref, cv_ref, bias_ref, o_ref, *, groups):
    q2, k2, v2 = q_ref[0], k_ref[0], v_ref[0]
    kc2 = jnp.concatenate([ck_ref[0, 0, 0], ck_ref[0, 0, 1]], axis=-1).astype(BF16)
    vc2 = jnp.concatenate([cv_ref[0, 0, 0], cv_ref[0, 0, 1]], axis=-1).astype(BF16)
    lo_half = lax.broadcasted_iota(jnp.int32, (1, LANES), 1) < HEAD_DIM
    nq = NA_QROWS * GRID_W
    for (q0, k0, nk, boff) in groups:
        o2 = None
        for e in range(HEADS_PER_BLOCK):
            msk = lo_half if e == 0 else jnp.logical_not(lo_half)
            qe = jnp.where(msk, q2[q0:q0 + nq], jnp.zeros((nq, LANES), BF16))
            ve = jnp.where(msk, v2[k0:k0 + nk], jnp.zeros((nk, LANES), BF16))
            vce = jnp.where(msk, vc2, jnp.zeros_like(vc2))
            dn = (((1,), (1,)), ((), ()))
            s_w = (lax.dot_general(qe, k2[k0:k0 + nk], dn, preferred_element_type=F32) * ATTN_SCALE
                   + bias_ref[e, :, boff:boff + nk])
            s_c = lax.dot_general(qe, kc2, dn, preferred_element_type=F32) * ATTN_SCALE
            mx = jnp.maximum(jnp.max(s_w, axis=-1, keepdims=True),
                             jnp.max(s_c, axis=-1, keepdims=True))
            p_w = jnp.exp(s_w - mx)
            p_c = jnp.exp(s_c - mx)
            l = jnp.sum(p_w, axis=-1, keepdims=True) + jnp.sum(p_c, axis=-1, keepdims=True)
            oe = (jnp.dot(p_w.astype(BF16), ve, preferred_element_type=F32)
                  + jnp.dot(p_c.astype(BF16), vce, preferred_element_type=F32)) / l
            o2 = oe if o2 is None else o2 + oe
        o_ref[0, q0:q0 + nq, :] = o2.astype(BF16)


def _nattn(qkv, cache_k, cache_v, bias, *, layer):
    b, t, _ = qkv.shape
    past = cache_k.shape[3]
    groups, _, _ = _na_groups(t // GRID_W)
    nhp = N_HEADS // HEADS_PER_BLOCK
    nq, nbias = bias.shape[1], bias.shape[2]
    cache_spec = pl.BlockSpec((1, 1, HEADS_PER_BLOCK, past, HEAD_DIM),
                              lambda hp, bi: (bi, layer, hp, 0, 0))
    return pl.pallas_call(
        functools.partial(_nattn_kernel, groups=groups),
        grid=(nhp, b),
        in_specs=[
            pl.BlockSpec((1, t, LANES), lambda hp, bi: (bi, 0, hp)),
            pl.BlockSpec((1, t, LANES), lambda hp, bi: (bi, 0, nhp + hp)),
            pl.BlockSpec((1, t, LANES), lambda hp, bi: (bi, 0, 2 * nhp + hp)),
            cache_spec,
            cache_spec,
            pl.BlockSpec((HEADS_PER_BLOCK, nq, nbias), lambda hp, bi: (hp, 0, 0)),
        ],
        out_specs=pl.BlockSpec((1, t, LANES), lambda hp, bi: (bi, 0, hp)),
        out_shape=jax.ShapeDtypeStruct((b, t, D_MODEL), BF16),
        compiler_params=_params(2),
        name="nattn",
    )(qkv, qkv, qkv, cache_k, cache_v, bias)


def _gmlp_kernel(x_ref, mod_ref, g_ref, win_ref, gv_ref, ws_ref, bs_ref, o_ref, vv_scr, *, tm):
    e = GMLP_WIDTH
    gd = e // GMLP_GROUPS
    mod = mod_ref[0]
    h = _rms_mod(x_ref[...], g_ref[...], _mod_chunk(mod, 0), _mod_chunk(mod, 1)).astype(BF16)
    zv = jax.nn.gelu(jnp.dot(h, win_ref[:, e:2 * e], preferred_element_type=F32))
    xc = zv - jnp.mean(zv, axis=-1, keepdims=True)
    vv = xc * lax.rsqrt(jnp.mean(xc * xc, axis=-1, keepdims=True) + EPS) * gv_ref[...]
    vv_scr[...] = vv.astype(BF16)
    pair = 2 * gd
    for gp in range(e // pair):
        u2 = jax.nn.gelu(jnp.dot(h, win_ref[:, gp * pair:(gp + 1) * pair],
                                 preferred_element_type=F32))
        for gi in range(2):
            g = 2 * gp + gi
            cols = slice(g * gd, (g + 1) * gd)
            for c in range(tm // CHUNK):
                rows = slice(c * CHUNK, (c + 1) * CHUNK)
                mixed = jnp.dot(ws_ref[g], vv_scr[rows, cols],
                                preferred_element_type=F32) + bs_ref[g]
                o_ref[rows, cols] = (u2[rows, gi * gd:(gi + 1) * gd] * mixed).astype(BF16)


def _gmlp(x, mods3, mod_row_fn, g, w_in, g_v, w_s, b_s_full, *, tm):
    n, d = x.shape
    e = GMLP_WIDTH
    return pl.pallas_call(
        functools.partial(_gmlp_kernel, tm=tm),
        grid=(n // tm,),
        in_specs=[
            pl.BlockSpec((tm, d), lambda i: (i, 0)),
            pl.BlockSpec((1, 1, 6 * d), lambda i: (mod_row_fn(i), 0, 0)),
            _resident((1, d)),
            _resident((d, 2 * e)),
            _resident((1, e)),
            _resident(w_s.shape),
            _resident(b_s_full.shape),
        ],
        out_specs=pl.BlockSpec((tm, e), lambda i: (i, 0)),
        out_shape=jax.ShapeDtypeStruct((n, e), BF16),
        scratch_shapes=[pltpu.VMEM((tm, e), BF16)],
        compiler_params=_params(),
        name="gmlp",
    )(x, mods3, g, w_in, g_v, w_s, b_s_full)


def _conv_pieces(tm, seq, halo):
    if halo:
        return ((0, tm + 2 * HALO, HALO, tm),)
    return tuple((s * seq, seq, 0, seq) for s in range(tm // seq))


def _mix_ffn_kernel(x_ref, xp_ref, xn_ref, m_ref, mp_ref, mn_ref, mod_ref, gffn_ref, gfin_ref,
                    wmo_ref, wup_ref, wconv_ref, bconv_ref, wdown_ref, o_ref, act_scr,
                    *, tm, seq, halo, final):
    mod = mod_ref[0]
    gate_mix, shift, scale, gate_ffn = (_mod_chunk(mod, k) for k in (2, 3, 4, 5))
    if halo:
        xcat = jnp.concatenate([xp_ref[0], x_ref[...], xn_ref[0]], axis=0)
        mcat = jnp.concatenate([mp_ref[0], m_ref[...], mn_ref[0]], axis=0)
        main = slice(HALO, HALO + tm)
    else:
        xcat, mcat = x_ref[...], m_ref[...]
        main = slice(0, tm)
    r = xcat.shape[0]
    x1 = xcat + gate_mix * jnp.dot(mcat, wmo_ref[...], preferred_element_type=F32)
    hcat = _rms_mod(x1, gffn_ref[...], shift, scale).astype(BF16)

    f = FFN_TILE
    if halo:
        blk = pl.program_id(0) % (seq // tm)
        hcat = jnp.concatenate([
            jnp.where(blk == 0, jnp.zeros((HALO, D_MODEL), BF16), hcat[:HALO]),
            hcat[main],
            jnp.where(blk == seq // tm - 1, jnp.zeros((HALO, D_MODEL), BF16), hcat[HALO + tm:]),
        ], axis=0)
    pieces = _conv_pieces(tm, seq, halo)
    sub = lax.broadcasted_iota(jnp.int32, (PAD, 1), 0)

    def up_conv(col):
        cols = slice(col, col + f)
        a = jnp.dot(hcat, wup_ref[:, cols], preferred_element_type=F32)
        wc = wconv_ref[:, cols]
        out = []
        for (a0, n_rows, own0, own) in pieces:
            piece = a[a0:a0 + n_rows]
            own_rows = slice(own0, own0 + own)
            prev = pltpu.roll(piece, 1, 0)[own_rows]
            nxt = pltpu.roll(piece, n_rows - 1, 0)[own_rows]
            if not halo:
                prev = jnp.concatenate([jnp.where(sub == 0, 0.0, prev[:PAD]), prev[PAD:]], axis=0)
                nxt = jnp.concatenate([nxt[:-PAD], jnp.where(sub == PAD - 1, 0.0, nxt[-PAD:])], axis=0)
            out.append(prev * wc[0:1] + piece[own_rows] * wc[1:2] + nxt * wc[2:3]
                       + bconv_ref[:, cols])
        return out[0] if len(out) == 1 else jnp.concatenate(out, axis=0)

    for c in range(FFN_DIM // f):
        act = jax.nn.silu(up_conv(c * f)) * up_conv(FFN_DIM + c * f)
        act_scr[:, c * f:(c + 1) * f] = act.astype(BF16)
    acc = jnp.dot(act_scr[...], wdown_ref[...], preferred_element_type=F32)
    out = x1[main] + gate_ffn * acc
    if final:
        out = out * lax.rsqrt(jnp.mean(out * out, axis=-1, keepdims=True) + EPS) * gfin_ref[...]
    o_ref[...] = out


def _mix_ffn(x, m, mods3, mod_row_fn, g_ffn, g_fin, w_mo, w_up, w_conv, b_conv, w_down,
             *, tm, seq, final):
    n, d = x.shape
    km = m.shape[1]
    halo = tm % seq != 0
    assert not halo or seq % tm == 0
    nh = n // HALO
    per = tm // HALO
    prev_map = lambda i: (jnp.maximum(i * per - 1, 0), 0, 0)
    next_map = lambda i: (jnp.minimum((i + 1) * per, nh - 1), 0, 0)
    kern = functools.partial(_mix_ffn_kernel, tm=tm, seq=seq, halo=halo, final=final)
    return pl.pallas_call(
        kern,
        grid=(n // tm,),
        in_specs=[
            pl.BlockSpec((tm, d), lambda i: (i, 0)),
            pl.BlockSpec((1, HALO, d), prev_map),
            pl.BlockSpec((1, HALO, d), next_map),
            pl.BlockSpec((tm, km), lambda i: (i, 0)),
            pl.BlockSpec((1, HALO, km), prev_map),
            pl.BlockSpec((1, HALO, km), next_map),
            pl.BlockSpec((1, 1, 6 * d), lambda i: (mod_row_fn(i), 0, 0)),
            _resident((1, d)),
            _resident((1, d)),
            _resident(w_mo.shape),
            _resident(w_up.shape),
            _resident(w_conv.shape),
            _resident(b_conv.shape),
            _resident(w_down.shape),
        ],
        out_specs=pl.BlockSpec((tm, d), lambda i: (i, 0)),
        out_shape=jax.ShapeDtypeStruct((n, d), F32),
        scratch_shapes=[pltpu.VMEM((tm, FFN_DIM), BF16)],
        compiler_params=_params(),
        name="mix_ffn",
    )(x, x.reshape(nh, HALO, d), x.reshape(nh, HALO, d),
      m, m.reshape(nh, HALO, km), m.reshape(nh, HALO, km),
      mods3, g_ffn, g_fin, w_mo, w_up, w_conv, b_conv, w_down)


def _ffn_weights(w_up, w_conv, b_conv, w_down):
    return w_up.astype(BF16), w_conv, b_conv[None, :], w_down.astype(BF16)


def kernel(x_prompt, x_sample, cache_k, cache_v, c, c_ctx, w_ada, b_ada, norm_mix_g,
           norm_ffn_g, norm_final_g, w_qkv, w_attn_out, rpb, w_gmlp_in, g_gmlp_v,
           w_spatial, b_spatial, w_gmlp_out, w_ffn_up, w_ffn_conv, b_ffn_conv, w_ffn_down):
    batch, seq, d = x_prompt.shape
    dec_batch, dec_seq, _ = x_sample.shape
    depth = w_ada.shape[0]
    assert depth == 2 and d == D_MODEL

    cond = jnp.zeros((MOD_ROWS, d), F32).at[0].set(c_ctx).at[1:1 + dec_batch].set(c)
    mods3 = _ada(cond, w_ada, b_ada).reshape(depth * MOD_ROWS, 1, 6 * d)

    w_qkv_b = w_qkv[0].astype(BF16)
    w_ao_b = w_attn_out[0].astype(BF16)
    w_gin_b = w_gmlp_in[0].astype(BF16)
    w_gout_b = w_gmlp_out[0].astype(BF16)
    w_s_b = w_spatial[0].astype(BF16)
    b_s_full = jnp.broadcast_to(b_spatial[0][:, :, None], (GMLP_GROUPS, CHUNK, CHUNK))
    ffn_w = [_ffn_weights(w_ffn_up[l], w_ffn_conv[l], b_ffn_conv[l], w_ffn_down[l])
             for l in range(depth)]
    g_mix = norm_mix_g.reshape(depth, 1, d)
    g_ffn = norm_ffn_g.reshape(depth, 1, d)
    g_fin = norm_final_g.reshape(1, d)
    tm = 512

    def ctx_row(layer):
        return lambda i: layer * MOD_ROWS
    x = x_prompt.reshape(batch * seq, d)
    o, state_k, state_v = _ctx_attn(x, mods3, 0, g_mix[0], w_qkv_b, batch=batch, seq=seq)
    x = _mix_ffn(x, o, mods3, ctx_row(0), g_ffn[0], g_fin, w_ao_b, *ffn_w[0],
                 tm=tm, seq=seq, final=False)
    m = _gmlp(x, mods3, ctx_row(1), g_mix[1], w_gin_b, g_gmlp_v[0:1], w_s_b, b_s_full, tm=tm)
    y_prompt = _mix_ffn(x, m, mods3, ctx_row(1), g_ffn[1], g_fin, w_gout_b, *ffn_w[1],
                        tm=tm, seq=seq, final=True).reshape(batch, seq, d)

    def lat_row(layer):
        return lambda i: layer * MOD_ROWS + 1 + (i * tm) // dec_seq
    x = x_sample.reshape(dec_batch * dec_seq, d)
    qkv = _qkv(x, mods3, lat_row(0), g_mix[0], w_qkv_b, tm=tm)
    bias = _na_bias(rpb[0], dec_seq // GRID_W)
    o = _nattn(qkv.reshape(dec_batch, dec_seq, 3 * d), cache_k, cache_v, bias, layer=0)
    x = _mix_ffn(x, o.reshape(dec_batch * dec_seq, d), mods3, lat_row(0), g_ffn[0], g_fin,
                 w_ao_b, *ffn_w[0], tm=tm, seq=dec_seq, final=False)
    m = _gmlp(x, mods3, lat_row(1), g_mix[1], w_gin_b, g_gmlp_v[0:1], w_s_b, b_s_full, tm=tm)
    y_sample = _mix_ffn(x, m, mods3, lat_row(1), g_ffn[1], g_fin, w_gout_b, *ffn_w[1],
                        tm=tm, seq=dec_seq, final=True).reshape(dec_batch, dec_seq, d)

    return (y_prompt, y_sample, state_k, state_v)
```

```python
import functools

import numpy as np
import jax
import jax.numpy as jnp
from jax import lax
from jax.experimental import pallas as pl
from jax.experimental.pallas import tpu as pltpu

D_MODEL = 1024
N_HEADS = 16
HEAD_DIM = D_MODEL // N_HEADS
GRID_W = 64
NA_ROWS = 8
NA_COLS = 16
CHUNK = 128
GMLP_WIDTH = 2 * D_MODEL
GMLP_GROUPS = 16
FFN_DIM = 2816
CONV_W = 3
EPS = 1e-6
ATTN_SCALE = HEAD_DIM ** -0.5

LANES = 128
HEADS_PER_BLOCK = LANES // HEAD_DIM
MOD_ROWS = 8
HALO = 16
PAD = 8
FFN_TILE = 256
NA_QROWS = 4
NEG_BIAS = -1e30
VMEM_LIMIT = 56 * 1024 * 1024

BF16 = jnp.bfloat16
F32 = jnp.float32


def _params(n_axes=1):
    return pltpu.CompilerParams(
        dimension_semantics=("arbitrary",) * n_axes, vmem_limit_bytes=VMEM_LIMIT)


def _resident(shape):
    nd = len(shape)
    return pl.BlockSpec(shape, lambda *_: (0,) * nd, pipeline_mode=pl.Buffered(1))


def _resident_layer(shape, layer):
    nd = len(shape)
    return pl.BlockSpec((1,) + tuple(shape[1:]), lambda *_: (layer,) + (0,) * (nd - 1),
                        pipeline_mode=pl.Buffered(1))


def _rms_mod(x, g, shift, scale):
    y = x * lax.rsqrt(jnp.mean(x * x, axis=-1, keepdims=True) + EPS)
    return (y * g) * (1 + scale) + shift


def _mod_chunk(mod, k):
    return mod[:, k * D_MODEL:(k + 1) * D_MODEL]


def _ada_kernel(cond_ref, w_ref, b_ref, o_ref):
    s = jax.nn.silu(cond_ref[...]).astype(BF16)
    o_ref[0] = jnp.dot(s, w_ref[0].astype(BF16), preferred_element_type=F32) + b_ref[0]


def _ada(cond, w_ada, b_ada):
    depth, d, n = w_ada.shape
    tn = 1536
    return pl.pallas_call(
        _ada_kernel,
        grid=(depth, n // tn),
        in_specs=[
            pl.BlockSpec((MOD_ROWS, d), lambda l, j: (0, 0)),
            pl.BlockSpec((1, d, tn), lambda l, j: (l, 0, j)),
            pl.BlockSpec((1, 1, tn), lambda l, j: (l, 0, j)),
        ],
        out_specs=pl.BlockSpec((1, MOD_ROWS, tn), lambda l, j: (l, 0, j)),
        out_shape=jax.ShapeDtypeStruct((depth, MOD_ROWS, n), F32),
        compiler_params=_params(2),
        name="ada",
    )(cond, w_ada, b_ada.reshape(depth, 1, n))


def _den_lane(e):
    return (1 - e) * HEAD_DIM


def _head_lanes(e, axis=1):
    ch = lax.broadcasted_iota(jnp.int32, (1, LANES) if axis == 1 else (LANES, 1), axis)
    own = (ch < HEAD_DIM) if e == 0 else (ch >= HEAD_DIM)
    return own, ch == _den_lane(e)


def _head_q(q2, e):
    own, _ = _head_lanes(e)
    return jnp.where(own, q2, jnp.zeros_like(q2))


def _head_v(v2, e, axis=1):
    own, is_den = _head_lanes(e, axis)
    return jnp.where(own, v2, jnp.broadcast_to(is_den.astype(v2.dtype), v2.shape))


def _normalise(accs):
    outs = [acc / acc[:, _den_lane(e):_den_lane(e) + 1] for e, acc in enumerate(accs)]
    own0, _ = _head_lanes(0)
    return jnp.where(own0, outs[0], outs[1])


def _ctx_attn_kernel(x_ref, mod_ref, g_ref, w_ref, o_ref, sk_ref, sv_ref, qkv_scr, *, nb, seq):
    mod = mod_ref[0]
    h = _rms_mod(x_ref[...], g_ref[...], _mod_chunk(mod, 0), _mod_chunk(mod, 1)).astype(BF16)
    qkv_scr[...] = jnp.dot(h, w_ref[...], preferred_element_type=F32)
    for b in range(nb):
        rows = slice(b * seq, (b + 1) * seq)
        for hh in range(N_HEADS):
            sk_ref[b, 0, hh] = qkv_scr[rows, D_MODEL + hh * HEAD_DIM:D_MODEL + (hh + 1) * HEAD_DIM]
            sv_ref[b, 0, hh] = qkv_scr[rows, 2 * D_MODEL + hh * HEAD_DIM:2 * D_MODEL + (hh + 1) * HEAD_DIM]
        for hp in range(N_HEADS // HEADS_PER_BLOCK):
            cols = slice(hp * LANES, (hp + 1) * LANES)
            q2 = (qkv_scr[rows, cols] * ATTN_SCALE).astype(BF16)
            k2 = qkv_scr[rows, D_MODEL + hp * LANES:D_MODEL + (hp + 1) * LANES].astype(BF16)
            v2 = qkv_scr[rows, 2 * D_MODEL + hp * LANES:2 * D_MODEL + (hp + 1) * LANES].astype(BF16)
            o2 = None
            for e in range(HEADS_PER_BLOCK):
                own, _ = _head_lanes(e)
                qe = jnp.where(own, q2, jnp.zeros_like(q2))
                ve = jnp.where(own, v2, jnp.zeros_like(v2))
                s = lax.dot_general(qe, k2, (((1,), (1,)), ((), ())), preferred_element_type=F32)
                p = jnp.exp(s - jnp.max(s, axis=-1, keepdims=True))
                oe = (jnp.dot(p.astype(BF16), ve, preferred_element_type=F32)
                      / jnp.sum(p, axis=-1, keepdims=True))
                o2 = oe if o2 is None else o2 + oe
            o_ref[rows, cols] = o2.astype(BF16)


def _ctx_attn(x, mods3, mod_row, g, w_qkv, *, batch, seq):
    n, d = x.shape
    nb = 2
    tm = nb * seq
    kern = functools.partial(_ctx_attn_kernel, nb=nb, seq=seq)
    state = jax.ShapeDtypeStruct((batch, 1, N_HEADS, seq, HEAD_DIM), F32)
    state_spec = pl.BlockSpec((nb, 1, N_HEADS, seq, HEAD_DIM), lambda i: (i, 0, 0, 0, 0))
    return pl.pallas_call(
        kern,
        grid=(n // tm,),
        in_specs=[
            pl.BlockSpec((tm, d), lambda i: (i, 0)),
            pl.BlockSpec((1, 1, 6 * d), lambda i: (mod_row, 0, 0)),
            _resident((1, d)),
            _resident((d, 3 * d)),
        ],
        out_specs=[pl.BlockSpec((tm, d), lambda i: (i, 0)), state_spec, state_spec],
        out_shape=[jax.ShapeDtypeStruct((n, d), BF16), state, state],
        scratch_shapes=[pltpu.VMEM((tm, 3 * d), F32)],
        compiler_params=_params(),
        name="ctx_attn",
    )(x, mods3, g, w_qkv)


def _qkv_kernel(x_ref, mod_ref, g_ref, w_ref, o_ref, *, tn):
    mod = mod_ref[0]
    h = _rms_mod(x_ref[...], g_ref[...], _mod_chunk(mod, 0), _mod_chunk(mod, 1)).astype(BF16)
    for c in range(w_ref.shape[1] // tn):
        cols = slice(c * tn, (c + 1) * tn)
        o_ref[:, cols] = jnp.dot(h, w_ref[:, cols], preferred_element_type=F32).astype(BF16)


def _qkv(x, mods3, mod_row_fn, g, w_qkv, *, tm):
    n, d = x.shape
    nout = w_qkv.shape[1]
    return pl.pallas_call(
        functools.partial(_qkv_kernel, tn=512),
        grid=(n // tm,),
        in_specs=[
            pl.BlockSpec((tm, d), lambda i: (i, 0)),
            pl.BlockSpec((1, 1, 6 * d), lambda i: (mod_row_fn(i), 0, 0)),
            _resident((1, d)),
            _resident((d, nout)),
        ],
        out_specs=pl.BlockSpec((tm, nout), lambda i: (i, 0)),
        out_shape=jax.ShapeDtypeStruct((n, nout), BF16),
        compiler_params=_params(),
        name="qkv",
    )(x, mods3, g, w_qkv)


def _na_groups(rows):
    kh = min(NA_ROWS, rows)
    row_start = [min(max(r - kh // 2, 0), rows - kh) for r in range(rows)]
    groups, off = [], 0
    for r0 in range(0, rows, NA_QROWS):
        lo = min(row_start[r0:r0 + NA_QROWS])
        hi = max(row_start[r0:r0 + NA_QROWS]) + kh
        if (hi - lo) * GRID_W % LANES:
            if hi < rows:
                hi += 1
            else:
                lo -= 1
        nk = (hi - lo) * GRID_W
        groups.append((r0 * GRID_W, lo * GRID_W, nk, off))
        off += nk
    return tuple(groups), row_start, kh


def _na_bias(rpb, rows):
    groups, row_start, kh = _na_groups(rows)
    nh, nro, nco = rpb.shape
    w = GRID_W
    assert 2 * w == LANES
    plan = []
    for (q0, k0, nk, off) in groups:
        for qi in range(NA_QROWS):
            qr = q0 // w + qi
            rs = row_start[qr]
            tile_of = lambda kr: kr - qr + NA_ROWS - 1 if rs <= kr < rs + kh else nro
            for p in range(nk // LANES):
                kr = k0 // w + 2 * p
                plan.append((qi * w, off + p * LANES, tile_of(kr), tile_of(kr + 1)))
    nbias = sum(g[2] for g in groups)
    lead = w - NA_COLS
    padded = jnp.pad(rpb, ((0, 0), (0, 1), (lead, LANES - nco - lead)))
    return pl.pallas_call(
        functools.partial(_na_bias_kernel, plan=tuple(plan), n_tiles=nro),
        grid=(nh,),
        in_specs=[pl.BlockSpec((1, nro + 1, LANES), lambda h: (h, 0, 0))],
        out_specs=pl.BlockSpec((1, NA_QROWS * w, nbias), lambda h: (h, 0, 0)),
        out_shape=jax.ShapeDtypeStruct((nh, NA_QROWS * w, nbias), F32),
        scratch_shapes=[pltpu.VMEM((nro + 1, w, LANES), F32), pltpu.VMEM((nro + 1, w, LANES), F32)],
        compiler_params=_params(),
        name="na_bias",
    )(padded)


def _na_bias_kernel(g_ref, o_ref, lo_scr, hi_scr, *, plan, n_tiles):
    w = GRID_W
    lane = lax.broadcasted_iota(jnp.int32, (w, LANES), 1)
    qc = lax.broadcasted_iota(jnp.int32, (w, LANES), 0)
    kc = lane % w
    col_start = jnp.clip(qc - NA_COLS // 2, 0, w - NA_COLS)
    col_ok = (kc >= col_start) & (kc < col_start + NA_COLS)
    in_lo = lane < w
    neg = jnp.full((w, LANES), NEG_BIAS, F32)
    for a in range(n_tiles):
        row = jnp.broadcast_to(g_ref[0, a:a + 1, :], (w, LANES))
        lo_scr[a] = jnp.where(col_ok, pltpu.roll(row, w + 1, 1, stride=1, stride_axis=0), neg)
        hi_scr[a] = jnp.where(col_ok, pltpu.roll(row, 1, 1, stride=1, stride_axis=0), neg)
    lo_scr[n_tiles] = neg
    hi_scr[n_tiles] = neg
    for (r0, c0, a_lo, a_hi) in plan:
        o_ref[0, r0:r0 + w, c0:c0 + LANES] = jnp.where(in_lo, lo_scr[a_lo], hi_scr[a_hi])


def _nattn_kernel(q_ref, k_ref, v_ref, ck_ref, cv_ref, bias_ref, o_ref, *, groups):
    q2, k2, v2 = q_ref[0], k_ref[0], v_ref[0]
    past = ck_ref.shape[-1]
    kct = ck_ref[0, 0].reshape(LANES, past).astype(BF16)
    vct = cv_ref[0, 0].reshape(LANES, past).astype(BF16)
    q2 = q2 * ATTN_SCALE
    nq = NA_QROWS * GRID_W
    dn = (((1,), (1,)), ((), ()))
    for (q0, k0, nk, boff) in groups:
        accs = []
        for e in range(HEADS_PER_BLOCK):
            qe = _head_q(q2[q0:q0 + nq], e)
            s_w = (lax.dot_general(qe, k2[k0:k0 + nk], dn, preferred_element_type=F32)
                   + bias_ref[e, :, boff:boff + nk])
            s_c = jnp.dot(qe, kct, preferred_element_type=F32)
            mx = jnp.maximum(jnp.max(s_w, axis=-1, keepdims=True),
                             jnp.max(s_c, axis=-1, keepdims=True))
            accs.append(
                jnp.dot(jnp.exp(s_w - mx).astype(BF16), _head_v(v2[k0:k0 + nk], e),
                        preferred_element_type=F32)
                + lax.dot_general(jnp.exp(s_c - mx).astype(BF16), _head_v(vct, e, axis=0), dn,
                                  preferred_element_type=F32))
        o_ref[0, q0:q0 + nq, :] = _normalise(accs).astype(BF16)


def _nattn(qkv, cache_kt, cache_vt, bias, *, layer):
    b, t, _ = qkv.shape
    past = cache_kt.shape[4]
    groups, _, _ = _na_groups(t // GRID_W)
    nhp = N_HEADS // HEADS_PER_BLOCK
    nq, nbias = bias.shape[1], bias.shape[2]
    cache_spec = pl.BlockSpec((1, 1, HEADS_PER_BLOCK, HEAD_DIM, past),
                              lambda hp, bi: (bi, layer, hp, 0, 0))
    return pl.pallas_call(
        functools.partial(_nattn_kernel, groups=groups),
        grid=(nhp, b),
        in_specs=[
            pl.BlockSpec((1, t, LANES), lambda hp, bi: (bi, 0, hp)),
            pl.BlockSpec((1, t, LANES), lambda hp, bi: (bi, 0, nhp + hp)),
            pl.BlockSpec((1, t, LANES), lambda hp, bi: (bi, 0, 2 * nhp + hp)),
            cache_spec,
            cache_spec,
            pl.BlockSpec((HEADS_PER_BLOCK, nq, nbias), lambda hp, bi: (hp, 0, 0)),
        ],
        out_specs=pl.BlockSpec((1, t, LANES), lambda hp, bi: (bi, 0, hp)),
        out_shape=jax.ShapeDtypeStruct((b, t, D_MODEL), BF16),
        compiler_params=_params(2),
        name="nattn",
    )(qkv, qkv, qkv, cache_kt, cache_vt, bias)


def _gmlp_kernel(x_ref, mod_ref, g_ref, win_ref, gv_ref, ws_ref, bs_ref, o_ref, vv_scr, *, tm):
    e = GMLP_WIDTH
    gd = e // GMLP_GROUPS
    mod = mod_ref[0]
    h = _rms_mod(x_ref[...], g_ref[...], _mod_chunk(mod, 0), _mod_chunk(mod, 1)).astype(BF16)
    zv = jax.nn.gelu(jnp.dot(h, win_ref[:, e:2 * e], preferred_element_type=F32))
    xc = zv - jnp.mean(zv, axis=-1, keepdims=True)
    vv = xc * lax.rsqrt(jnp.mean(xc * xc, axis=-1, keepdims=True) + EPS) * gv_ref[...]
    vv_scr[...] = vv.astype(BF16)
    pair = 2 * gd
    for gp in range(e // pair):
        u2 = jax.nn.gelu(jnp.dot(h, win_ref[:, gp * pair:(gp + 1) * pair],
                                 preferred_element_type=F32))
        for gi in range(2):
            g = 2 * gp + gi
            cols = slice(g * gd, (g + 1) * gd)
            for c in range(tm // CHUNK):
                rows = slice(c * CHUNK, (c + 1) * CHUNK)
                mixed = jnp.dot(ws_ref[g], vv_scr[rows, cols],
                                preferred_element_type=F32) + bs_ref[g]
                o_ref[rows, cols] = (u2[rows, gi * gd:(gi + 1) * gd] * mixed).astype(BF16)


def _gmlp(x, mods3, mod_row_fn, g, w_in, g_v, w_s, b_s_full, *, tm):
    n, d = x.shape
    e = GMLP_WIDTH
    return pl.pallas_call(
        functools.partial(_gmlp_kernel, tm=tm),
        grid=(n // tm,),
        in_specs=[
            pl.BlockSpec((tm, d), lambda i: (i, 0)),
            pl.BlockSpec((1, 1, 6 * d), lambda i: (mod_row_fn(i), 0, 0)),
            _resident((1, d)),
            _resident((d, 2 * e)),
            _resident((1, e)),
            _resident(w_s.shape),
            _resident(b_s_full.shape),
        ],
        out_specs=pl.BlockSpec((tm, e), lambda i: (i, 0)),
        out_shape=jax.ShapeDtypeStruct((n, e), BF16),
        scratch_shapes=[pltpu.VMEM((tm, e), BF16)],
        compiler_params=_params(),
        name="gmlp",
    )(x, mods3, g, w_in, g_v, w_s, b_s_full)


def _conv_pieces(tm, seq, halo):
    if halo:
        return ((0, tm + 2 * HALO, HALO, tm),)
    return tuple((s * seq, seq, 0, seq) for s in range(tm // seq))


def _mix_ffn_kernel(x_ref, xp_ref, xn_ref, m_ref, mp_ref, mn_ref, mod_ref, gffn_ref, gfin_ref,
                    wmo_ref, wup_ref, wconv_ref, bconv_ref, wdown_ref, o_ref, act_scr,
                    *, tm, seq, halo, final):
    mod = mod_ref[0]
    gate_mix, shift, scale, gate_ffn = (_mod_chunk(mod, k) for k in (2, 3, 4, 5))
    if halo:
        xcat = jnp.concatenate([xp_ref[0], x_ref[...], xn_ref[0]], axis=0)
        mcat = jnp.concatenate([mp_ref[0], m_ref[...], mn_ref[0]], axis=0)
        main = slice(HALO, HALO + tm)
    else:
        xcat, mcat = x_ref[...], m_ref[...]
        main = slice(0, tm)
    r = xcat.shape[0]
    x1 = xcat + gate_mix * jnp.dot(mcat, wmo_ref[...], preferred_element_type=F32)
    hcat = _rms_mod(x1, gffn_ref[...], shift, scale).astype(BF16)

    f = FFN_TILE
    if halo:
        blk = pl.program_id(0) % (seq // tm)
        hcat = jnp.concatenate([
            jnp.where(blk == 0, jnp.zeros((HALO, D_MODEL), BF16), hcat[:HALO]),
            hcat[main],
            jnp.where(blk == seq // tm - 1, jnp.zeros((HALO, D_MODEL), BF16), hcat[HALO + tm:]),
        ], axis=0)
    pieces = _conv_pieces(tm, seq, halo)
    sub = lax.broadcasted_iota(jnp.int32, (PAD, 1), 0)

    def up_conv(col):
        cols = slice(col, col + f)
        a = jnp.dot(hcat, wup_ref[0, :, cols], preferred_element_type=F32)
        wc = wconv_ref[0, :, cols]
        out = []
        for (a0, n_rows, own0, own) in pieces:
            piece = a[a0:a0 + n_rows]
            own_rows = slice(own0, own0 + own)
            prev = pltpu.roll(piece, 1, 0)[own_rows]
            nxt = pltpu.roll(piece, n_rows - 1, 0)[own_rows]
            if not halo:
                prev = jnp.concatenate([jnp.where(sub == 0, 0.0, prev[:PAD]), prev[PAD:]], axis=0)
                nxt = jnp.concatenate([nxt[:-PAD], jnp.where(sub == PAD - 1, 0.0, nxt[-PAD:])], axis=0)
            out.append(prev * wc[0:1] + piece[own_rows] * wc[1:2] + nxt * wc[2:3]
                       + bconv_ref[0, :, cols])
        return out[0] if len(out) == 1 else jnp.concatenate(out, axis=0)

    for c in range(FFN_DIM // f):
        act = jax.nn.silu(up_conv(c * f)) * up_conv(FFN_DIM + c * f)
        act_scr[:, c * f:(c + 1) * f] = act.astype(BF16)
    acc = jnp.dot(act_scr[...], wdown_ref[0], preferred_element_type=F32)
    out = x1[main] + gate_ffn * acc
    if final:
        out = out * lax.rsqrt(jnp.mean(out * out, axis=-1, keepdims=True) + EPS) * gfin_ref[...]
    o_ref[...] = out


def _mix_ffn(x, m, mods3, mod_row_fn, g_ffn, g_fin, w_mo, w_up, w_conv, b_conv, w_down,
             *, layer, tm, seq, final):
    n, d = x.shape
    km = m.shape[1]
    halo = tm % seq != 0
    assert not halo or seq % tm == 0
    nh = n // HALO
    per = tm // HALO
    prev_map = lambda i: (jnp.maximum(i * per - 1, 0), 0, 0)
    next_map = lambda i: (jnp.minimum((i + 1) * per, nh - 1), 0, 0)
    kern = functools.partial(_mix_ffn_kernel, tm=tm, seq=seq, halo=halo, final=final)
    return pl.pallas_call(
        kern,
        grid=(n // tm,),
        in_specs=[
            pl.BlockSpec((tm, d), lambda i: (i, 0)),
            pl.BlockSpec((1, HALO, d), prev_map),
            pl.BlockSpec((1, HALO, d), next_map),
            pl.BlockSpec((tm, km), lambda i: (i, 0)),
            pl.BlockSpec((1, HALO, km), prev_map),
            pl.BlockSpec((1, HALO, km), next_map),
            pl.BlockSpec((1, 1, 6 * d), lambda i: (mod_row_fn(i), 0, 0)),
            _resident((1, d)),
            _resident((1, d)),
            _resident(w_mo.shape),
            _resident_layer(w_up.shape, layer),
            _resident_layer(w_conv.shape, layer),
            _resident_layer(b_conv.shape, layer),
            _resident_layer(w_down.shape, layer),
        ],
        out_specs=pl.BlockSpec((tm, d), lambda i: (i, 0)),
        out_shape=jax.ShapeDtypeStruct((n, d), F32),
        scratch_shapes=[pltpu.VMEM((tm, FFN_DIM), BF16)],
        compiler_params=_params(),
        name="mix_ffn",
    )(x, x.reshape(nh, HALO, d), x.reshape(nh, HALO, d),
      m, m.reshape(nh, HALO, km), m.reshape(nh, HALO, km),
      mods3, g_ffn, g_fin, w_mo, w_up, w_conv, b_conv, w_down)


def kernel(x_prompt, x_sample, cache_k, cache_v, c, c_ctx, w_ada, b_ada, norm_mix_g,
           norm_ffn_g, norm_final_g, w_qkv, w_attn_out, rpb, w_gmlp_in, g_gmlp_v,
           w_spatial, b_spatial, w_gmlp_out, w_ffn_up, w_ffn_conv, b_ffn_conv, w_ffn_down):
    batch, seq, d = x_prompt.shape
    dec_batch, dec_seq, _ = x_sample.shape
    depth = w_ada.shape[0]
    assert depth == 2 and d == D_MODEL

    cond = jnp.zeros((MOD_ROWS, d), F32).at[0].set(c_ctx).at[1:1 + dec_batch].set(c)
    mods3 = _ada(cond, w_ada, b_ada).reshape(depth * MOD_ROWS, 1, 6 * d)

    w_qkv_b = w_qkv[0].astype(BF16)
    w_ao_b = w_attn_out[0].astype(BF16)
    w_gin_b = w_gmlp_in[0].astype(BF16)
    w_gout_b = w_gmlp_out[0].astype(BF16)
    w_s_b = w_spatial[0].astype(BF16)
    b_s_full = jnp.broadcast_to(b_spatial[0][:, :, None], (GMLP_GROUPS, CHUNK, CHUNK))
    ffn_w = (w_ffn_up.astype(BF16), w_ffn_conv, b_ffn_conv[:, None, :], w_ffn_down.astype(BF16))
    g_mix = norm_mix_g.reshape(depth, 1, d)
    g_ffn = norm_ffn_g.reshape(depth, 1, d)
    g_fin = norm_final_g.reshape(1, d)
    tm = 512

    def ctx_row(layer):
        return lambda i: layer * MOD_ROWS
    x = x_prompt.reshape(batch * seq, d)
    o, state_k, state_v = _ctx_attn(x, mods3, 0, g_mix[0], w_qkv_b, batch=batch, seq=seq)
    x = _mix_ffn(x, o, mods3, ctx_row(0), g_ffn[0], g_fin, w_ao_b, *ffn_w,
                 layer=0, tm=tm, seq=seq, final=False)
    m = _gmlp(x, mods3, ctx_row(1), g_mix[1], w_gin_b, g_gmlp_v[0:1], w_s_b, b_s_full, tm=tm)
    y_prompt = _mix_ffn(x, m, mods3, ctx_row(1), g_ffn[1], g_fin, w_gout_b, *ffn_w,
                        layer=1, tm=tm, seq=seq, final=True).reshape(batch, seq, d)

    def lat_row(layer):
        return lambda i: layer * MOD_ROWS + 1 + (i * tm) // dec_seq
    x = x_sample.reshape(dec_batch * dec_seq, d)
    qkv = _qkv(x, mods3, lat_row(0), g_mix[0], w_qkv_b, tm=tm)
    bias = _na_bias(rpb[0], dec_seq // GRID_W)
    o = _nattn(qkv.reshape(dec_batch, dec_seq, 3 * d), jnp.swapaxes(cache_k, 3, 4),
               jnp.swapaxes(cache_v, 3, 4), bias, layer=0)
    x = _mix_ffn(x, o.reshape(dec_batch * dec_seq, d), mods3, lat_row(0), g_ffn[0], g_fin,
                 w_ao_b, *ffn_w, layer=0, tm=tm, seq=dec_seq, final=False)
    m = _gmlp(x, mods3, lat_row(1), g_mix[1], w_gin_b, g_gmlp_v[0:1], w_s_b, b_s_full, tm=tm)
    y_sample = _mix_ffn(x, m, mods3, lat_row(1), g_ffn[1], g_fin, w_gout_b, *ffn_w,
                        layer=1, tm=tm, seq=dec_seq, final=True).reshape(dec_batch, dec_seq, d)

    return (y_prompt, y_sample, state_k, state_v)
```

```python
import functools

import numpy as np
import jax
import jax.numpy as jnp
from jax import lax
from jax.experimental import pallas as pl
from jax.experimental.pallas import tpu as pltpu

D_MODEL = 1024
N_HEADS = 16
HEAD_DIM = D_MODEL // N_HEADS
GRID_W = 64
NA_ROWS = 8
NA_COLS = 16
CHUNK = 128
GMLP_WIDTH = 2 * D_MODEL
GMLP_GROUPS = 16
FFN_DIM = 2816
CONV_W = 3
EPS = 1e-6
ATTN_SCALE = HEAD_DIM ** -0.5

LANES = 128
HEADS_PER_BLOCK = LANES // HEAD_DIM
MOD_ROWS = 8
HALO = 16
PAD = 8
FFN_TILE = 256
NA_QROWS = 4
NEG_BIAS = -1e30
VMEM_LIMIT = 56 * 1024 * 1024

BF16 = jnp.bfloat16
F32 = jnp.float32


def _params(n_axes=1):
    return pltpu.CompilerParams(
        dimension_semantics=("arbitrary",) * n_axes, vmem_limit_bytes=VMEM_LIMIT)


def _resident(shape):
    nd = len(shape)
    return pl.BlockSpec(shape, lambda *_: (0,) * nd, pipeline_mode=pl.Buffered(1))


def _resident_layer(shape, layer):
    nd = len(shape)
    return pl.BlockSpec((1,) + tuple(shape[1:]), lambda *_: (layer,) + (0,) * (nd - 1),
                        pipeline_mode=pl.Buffered(1))


def _rms_mod(x, g, shift, scale):
    y = x * lax.rsqrt(jnp.mean(x * x, axis=-1, keepdims=True) + EPS)
    return (y * g) * (1 + scale) + shift


def _mod_chunk(mod, k):
    return mod[:, k * D_MODEL:(k + 1) * D_MODEL]


CAST_BLOCK_BYTES = 4 * 1024 * 1024


def _cast_kernel(w_ref, o_ref):
    o_ref[...] = w_ref[...].astype(o_ref.dtype)


def _to_bf16(w):
    shape = w.shape
    cols = shape[-1]
    rows = w.size // cols
    per = 16
    assert rows % per == 0
    blk = per
    while rows % (2 * blk) == 0 and 2 * blk * cols * 4 <= CAST_BLOCK_BYTES:
        blk *= 2
    out = pl.pallas_call(
        _cast_kernel,
        grid=(rows // blk,),
        in_specs=[pl.BlockSpec((blk, cols), lambda i: (i, 0))],
        out_specs=pl.BlockSpec((blk, cols), lambda i: (i, 0)),
        out_shape=jax.ShapeDtypeStruct((rows, cols), BF16),
        compiler_params=_params(),
        name="to_bf16",
    )(w.reshape(rows, cols))
    return out.reshape(shape)


def _ada_kernel(cond_ref, w_ref, b_ref, o_ref):
    s = jax.nn.silu(cond_ref[...]).astype(BF16)
    o_ref[0] = jnp.dot(s, w_ref[0].astype(BF16), preferred_element_type=F32) + b_ref[0]


def _ada(cond, w_ada, b_ada):
    depth, d, n = w_ada.shape
    tn = 1536
    return pl.pallas_call(
        _ada_kernel,
        grid=(depth, n // tn),
        in_specs=[
            pl.BlockSpec((MOD_ROWS, d), lambda l, j: (0, 0)),
            pl.BlockSpec((1, d, tn), lambda l, j: (l, 0, j)),
            pl.BlockSpec((1, 1, tn), lambda l, j: (l, 0, j)),
        ],
        out_specs=pl.BlockSpec((1, MOD_ROWS, tn), lambda l, j: (l, 0, j)),
        out_shape=jax.ShapeDtypeStruct((depth, MOD_ROWS, n), F32),
        compiler_params=_params(2),
        name="ada",
    )(cond, w_ada, b_ada.reshape(depth, 1, n))


def _den_lane(e):
    return (1 - e) * HEAD_DIM


def _head_lanes(e, axis=1):
    ch = lax.broadcasted_iota(jnp.int32, (1, LANES) if axis == 1 else (LANES, 1), axis)
    own = (ch < HEAD_DIM) if e == 0 else (ch >= HEAD_DIM)
    return own, ch == _den_lane(e)


def _head_q(q2, e):
    own, _ = _head_lanes(e)
    return jnp.where(own, q2, jnp.zeros_like(q2))


def _head_v(v2, e, axis=1):
    own, is_den = _head_lanes(e, axis)
    return jnp.where(own, v2, jnp.broadcast_to(is_den.astype(v2.dtype), v2.shape))


def _normalise(accs):
    outs = [acc / acc[:, _den_lane(e):_den_lane(e) + 1] for e, acc in enumerate(accs)]
    own0, _ = _head_lanes(0)
    return jnp.where(own0, outs[0], outs[1])


def _ctx_attn_kernel(x_ref, mod_ref, g_ref, w_ref, o_ref, sk_ref, sv_ref, qkv_scr, *, nb, seq):
    mod = mod_ref[0]
    h = _rms_mod(x_ref[...], g_ref[...], _mod_chunk(mod, 0), _mod_chunk(mod, 1)).astype(BF16)
    qkv_scr[...] = jnp.dot(h, w_ref[...], preferred_element_type=F32)
    for b in range(nb):
        rows = slice(b * seq, (b + 1) * seq)
        for hh in range(N_HEADS):
            sk_ref[b, 0, hh] = qkv_scr[rows, D_MODEL + hh * HEAD_DIM:D_MODEL + (hh + 1) * HEAD_DIM]
            sv_ref[b, 0, hh] = qkv_scr[rows, 2 * D_MODEL + hh * HEAD_DIM:2 * D_MODEL + (hh + 1) * HEAD_DIM]
        for hp in range(N_HEADS // HEADS_PER_BLOCK):
            cols = slice(hp * LANES, (hp + 1) * LANES)
            q2 = (qkv_scr[rows, cols] * ATTN_SCALE).astype(BF16)
            k2 = qkv_scr[rows, D_MODEL + hp * LANES:D_MODEL + (hp + 1) * LANES].astype(BF16)
            v2 = qkv_scr[rows, 2 * D_MODEL + hp * LANES:2 * D_MODEL + (hp + 1) * LANES].astype(BF16)
            o2 = None
            for e in range(HEADS_PER_BLOCK):
                own, _ = _head_lanes(e)
                qe = jnp.where(own, q2, jnp.zeros_like(q2))
                ve = jnp.where(own, v2, jnp.zeros_like(v2))
                s = lax.dot_general(qe, k2, (((1,), (1,)), ((), ())), preferred_element_type=F32)
                p = jnp.exp(s - jnp.max(s, axis=-1, keepdims=True))
                oe = (jnp.dot(p.astype(BF16), ve, preferred_element_type=F32)
                      / jnp.sum(p, axis=-1, keepdims=True))
                o2 = oe if o2 is None else o2 + oe
            o_ref[rows, cols] = o2.astype(BF16)


def _ctx_attn(x, mods3, mod_row, g, w_qkv, *, batch, seq):
    n, d = x.shape
    nb = 2
    tm = nb * seq
    kern = functools.partial(_ctx_attn_kernel, nb=nb, seq=seq)
    state = jax.ShapeDtypeStruct((batch, 1, N_HEADS, seq, HEAD_DIM), F32)
    state_spec = pl.BlockSpec((nb, 1, N_HEADS, seq, HEAD_DIM), lambda i: (i, 0, 0, 0, 0))
    return pl.pallas_call(
        kern,
        grid=(n // tm,),
        in_specs=[
            pl.BlockSpec((tm, d), lambda i: (i, 0)),
            pl.BlockSpec((1, 1, 6 * d), lambda i: (mod_row, 0, 0)),
            _resident((1, d)),
            _resident((d, 3 * d)),
        ],
        out_specs=[pl.BlockSpec((tm, d), lambda i: (i, 0)), state_spec, state_spec],
        out_shape=[jax.ShapeDtypeStruct((n, d), BF16), state, state],
        scratch_shapes=[pltpu.VMEM((tm, 3 * d), F32)],
        compiler_params=_params(),
        name="ctx_attn",
    )(x, mods3, g, w_qkv)


def _qkv_kernel(x_ref, mod_ref, g_ref, w_ref, o_ref, *, tn):
    mod = mod_ref[0]
    h = _rms_mod(x_ref[...], g_ref[...], _mod_chunk(mod, 0), _mod_chunk(mod, 1)).astype(BF16)
    for c in range(w_ref.shape[1] // tn):
        cols = slice(c * tn, (c + 1) * tn)
        o_ref[:, cols] = jnp.dot(h, w_ref[:, cols], preferred_element_type=F32).astype(BF16)


def _qkv(x, mods3, mod_row_fn, g, w_qkv, *, tm):
    n, d = x.shape
    nout = w_qkv.shape[1]
    return pl.pallas_call(
        functools.partial(_qkv_kernel, tn=512),
        grid=(n // tm,),
        in_specs=[
            pl.BlockSpec((tm, d), lambda i: (i, 0)),
            pl.BlockSpec((1, 1, 6 * d), lambda i: (mod_row_fn(i), 0, 0)),
            _resident((1, d)),
            _resident((d, nout)),
        ],
        out_specs=pl.BlockSpec((tm, nout), lambda i: (i, 0)),
        out_shape=jax.ShapeDtypeStruct((n, nout), BF16),
        compiler_params=_params(),
        name="qkv",
    )(x, mods3, g, w_qkv)


def _na_groups(rows):
    kh = min(NA_ROWS, rows)
    row_start = [min(max(r - kh // 2, 0), rows - kh) for r in range(rows)]
    groups, off = [], 0
    for r0 in range(0, rows, NA_QROWS):
        lo = min(row_start[r0:r0 + NA_QROWS])
        hi = max(row_start[r0:r0 + NA_QROWS]) + kh
        if (hi - lo) * GRID_W % LANES:
            if hi < rows:
                hi += 1
            else:
                lo -= 1
        nk = (hi - lo) * GRID_W
        groups.append((r0 * GRID_W, lo * GRID_W, nk, off))
        off += nk
    return tuple(groups), row_start, kh


def _na_bias(rpb, rows):
    groups, row_start, kh = _na_groups(rows)
    nh, nro, nco = rpb.shape
    w = GRID_W
    assert 2 * w == LANES
    plan = []
    for (q0, k0, nk, off) in groups:
        for qi in range(NA_QROWS):
            qr = q0 // w + qi
            rs = row_start[qr]
            tile_of = lambda kr: kr - qr + NA_ROWS - 1 if rs <= kr < rs + kh else nro
            for p in range(nk // LANES):
                kr = k0 // w + 2 * p
                plan.append((qi * w, off + p * LANES, tile_of(kr), tile_of(kr + 1)))
    nbias = sum(g[2] for g in groups)
    lead = w - NA_COLS
    padded = jnp.pad(rpb, ((0, 0), (0, 1), (lead, LANES - nco - lead)))
    return pl.pallas_call(
        functools.partial(_na_bias_kernel, plan=tuple(plan), n_tiles=nro),
        grid=(nh,),
        in_specs=[pl.BlockSpec((1, nro + 1, LANES), lambda h: (h, 0, 0))],
        out_specs=pl.BlockSpec((1, NA_QROWS * w, nbias), lambda h: (h, 0, 0)),
        out_shape=jax.ShapeDtypeStruct((nh, NA_QROWS * w, nbias), F32),
        scratch_shapes=[pltpu.VMEM((nro + 1, w, LANES), F32), pltpu.VMEM((nro + 1, w, LANES), F32)],
        compiler_params=_params(),
        name="na_bias",
    )(padded)


def _na_bias_kernel(g_ref, o_ref, lo_scr, hi_scr, *, plan, n_tiles):
    w = GRID_W
    lane = lax.broadcasted_iota(jnp.int32, (w, LANES), 1)
    qc = lax.broadcasted_iota(jnp.int32, (w, LANES), 0)
    kc = lane % w
    col_start = jnp.clip(qc - NA_COLS // 2, 0, w - NA_COLS)
    col_ok = (kc >= col_start) & (kc < col_start + NA_COLS)
    in_lo = lane < w
    neg = jnp.full((w, LANES), NEG_BIAS, F32)
    for a in range(n_tiles):
        row = jnp.broadcast_to(g_ref[0, a:a + 1, :], (w, LANES))
        lo_scr[a] = jnp.where(col_ok, pltpu.roll(row, w + 1, 1, stride=1, stride_axis=0), neg)
        hi_scr[a] = jnp.where(col_ok, pltpu.roll(row, 1, 1, stride=1, stride_axis=0), neg)
    lo_scr[n_tiles] = neg
    hi_scr[n_tiles] = neg
    for (r0, c0, a_lo, a_hi) in plan:
        o_ref[0, r0:r0 + w, c0:c0 + LANES] = jnp.where(in_lo, lo_scr[a_lo], hi_scr[a_hi])


def _nattn_kernel(q_ref, k_ref, v_ref, ck_ref, cv_ref, bias_ref, o_ref, *, groups):
    q2, k2, v2 = q_ref[0], k_ref[0], v_ref[0]
    past = ck_ref.shape[-1]
    kct = ck_ref[0, 0].reshape(LANES, past).astype(BF16)
    vct = cv_ref[0, 0].reshape(LANES, past).astype(BF16)
    q2 = q2 * ATTN_SCALE
    nq = NA_QROWS * GRID_W
    dn = (((1,), (1,)), ((), ()))
    for (q0, k0, nk, boff) in groups:
        accs = []
        for e in range(HEADS_PER_BLOCK):
            qe = _head_q(q2[q0:q0 + nq], e)
            s_w = (lax.dot_general(qe, k2[k0:k0 + nk], dn, preferred_element_type=F32)
                   + bias_ref[e, :, boff:boff + nk])
            s_c = jnp.dot(qe, kct, preferred_element_type=F32)
            mx = jnp.maximum(jnp.max(s_w, axis=-1, keepdims=True),
                             jnp.max(s_c, axis=-1, keepdims=True))
            accs.append(
                jnp.dot(jnp.exp(s_w - mx).astype(BF16), _head_v(v2[k0:k0 + nk], e),
                        preferred_element_type=F32)
                + lax.dot_general(jnp.exp(s_c - mx).astype(BF16), _head_v(vct, e, axis=0), dn,
                                  preferred_element_type=F32))
        o_ref[0, q0:q0 + nq, :] = _normalise(accs).astype(BF16)


def _nattn(qkv, cache_kt, cache_vt, bias, *, layer):
    b, t, _ = qkv.shape
    past = cache_kt.shape[4]
    groups, _, _ = _na_groups(t // GRID_W)
    nhp = N_HEADS // HEADS_PER_BLOCK
    nq, nbias = bias.shape[1], bias.shape[2]
    cache_spec = pl.BlockSpec((1, 1, HEADS_PER_BLOCK, HEAD_DIM, past),
                              lambda hp, bi: (bi, layer, hp, 0, 0))
    return pl.pallas_call(
        functools.partial(_nattn_kernel, groups=groups),
        grid=(nhp, b),
        in_specs=[
            pl.BlockSpec((1, t, LANES), lambda hp, bi: (bi, 0, hp)),
            pl.BlockSpec((1, t, LANES), lambda hp, bi: (bi, 0, nhp + hp)),
            pl.BlockSpec((1, t, LANES), lambda hp, bi: (bi, 0, 2 * nhp + hp)),
            cache_spec,
            cache_spec,
            pl.BlockSpec((HEADS_PER_BLOCK, nq, nbias), lambda hp, bi: (hp, 0, 0)),
        ],
        out_specs=pl.BlockSpec((1, t, LANES), lambda hp, bi: (bi, 0, hp)),
        out_shape=jax.ShapeDtypeStruct((b, t, D_MODEL), BF16),
        compiler_params=_params(2),
        name="nattn",
    )(qkv, qkv, qkv, cache_kt, cache_vt, bias)


def _gmlp_kernel(x_ref, mod_ref, g_ref, win_ref, gv_ref, ws_ref, bs_ref, o_ref, vv_scr, *, tm):
    e = GMLP_WIDTH
    gd = e // GMLP_GROUPS
    mod = mod_ref[0]
    h = _rms_mod(x_ref[...], g_ref[...], _mod_chunk(mod, 0), _mod_chunk(mod, 1)).astype(BF16)
    zv = jax.nn.gelu(jnp.dot(h, win_ref[:, e:2 * e], preferred_element_type=F32))
    xc = zv - jnp.mean(zv, axis=-1, keepdims=True)
    vv = xc * lax.rsqrt(jnp.mean(xc * xc, axis=-1, keepdims=True) + EPS) * gv_ref[...]
    vv_scr[...] = vv.astype(BF16)
    pair = 2 * gd
    for gp in range(e // pair):
        u2 = jax.nn.gelu(jnp.dot(h, win_ref[:, gp * pair:(gp + 1) * pair],
                                 preferred_element_type=F32))
        for gi in range(2):
            g = 2 * gp + gi
            cols = slice(g * gd, (g + 1) * gd)
            for c in range(tm // CHUNK):
                rows = slice(c * CHUNK, (c + 1) * CHUNK)
                mixed = jnp.dot(ws_ref[g], vv_scr[rows, cols],
                                preferred_element_type=F32) + bs_ref[g]
                o_ref[rows, cols] = (u2[rows, gi * gd:(gi + 1) * gd] * mixed).astype(BF16)


def _gmlp(x, mods3, mod_row_fn, g, w_in, g_v, w_s, b_s_full, *, tm):
    n, d = x.shape
    e = GMLP_WIDTH
    return pl.pallas_call(
        functools.partial(_gmlp_kernel, tm=tm),
        grid=(n // tm,),
        in_specs=[
            pl.BlockSpec((tm, d), lambda i: (i, 0)),
            pl.BlockSpec((1, 1, 6 * d), lambda i: (mod_row_fn(i), 0, 0)),
            _resident((1, d)),
            _resident((d, 2 * e)),
            _resident((1, e)),
            _resident(w_s.shape),
            _resident(b_s_full.shape),
        ],
        out_specs=pl.BlockSpec((tm, e), lambda i: (i, 0)),
        out_shape=jax.ShapeDtypeStruct((n, e), BF16),
        scratch_shapes=[pltpu.VMEM((tm, e), BF16)],
        compiler_params=_params(),
        name="gmlp",
    )(x, mods3, g, w_in, g_v, w_s, b_s_full)


def _conv_pieces(tm, seq, halo):
    if halo:
        return ((0, tm + 2 * HALO, HALO, tm),)
    return tuple((s * seq, seq, 0, seq) for s in range(tm // seq))


def _mix_ffn_kernel(x_ref, xp_ref, xn_ref, m_ref, mp_ref, mn_ref, mod_ref, gffn_ref, gfin_ref,
                    wmo_ref, wup_ref, wconv_ref, bconv_ref, wdown_ref, o_ref, act_scr,
                    *, tm, seq, halo, final):
    mod = mod_ref[0]
    gate_mix, shift, scale, gate_ffn = (_mod_chunk(mod, k) for k in (2, 3, 4, 5))
    if halo:
        xcat = jnp.concatenate([xp_ref[0], x_ref[...], xn_ref[0]], axis=0)
        mcat = jnp.concatenate([mp_ref[0], m_ref[...], mn_ref[0]], axis=0)
        main = slice(HALO, HALO + tm)
    else:
        xcat, mcat = x_ref[...], m_ref[...]
        main = slice(0, tm)
    r = xcat.shape[0]
    x1 = xcat + gate_mix * jnp.dot(mcat, wmo_ref[...], preferred_element_type=F32)
    hcat = _rms_mod(x1, gffn_ref[...], shift, scale).astype(BF16)

    f = FFN_TILE
    if halo:
        blk = pl.program_id(0) % (seq // tm)
        hcat = jnp.concatenate([
            jnp.where(blk == 0, jnp.zeros((HALO, D_MODEL), BF16), hcat[:HALO]),
            hcat[main],
            jnp.where(blk == seq // tm - 1, jnp.zeros((HALO, D_MODEL), BF16), hcat[HALO + tm:]),
        ], axis=0)
    pieces = _conv_pieces(tm, seq, halo)
    sub = lax.broadcasted_iota(jnp.int32, (PAD, 1), 0)

    def up_conv(col):
        cols = slice(col, col + f)
        a = jnp.dot(hcat, wup_ref[0, :, cols], preferred_element_type=F32)
        wc = wconv_ref[0, :, cols]
        out = []
        for (a0, n_rows, own0, own) in pieces:
            piece = a[a0:a0 + n_rows]
            own_rows = slice(own0, own0 + own)
            prev = pltpu.roll(piece, 1, 0)[own_rows]
            nxt = pltpu.roll(piece, n_rows - 1, 0)[own_rows]
            if not halo:
                prev = jnp.concatenate([jnp.where(sub == 0, 0.0, prev[:PAD]), prev[PAD:]], axis=0)
                nxt = jnp.concatenate([nxt[:-PAD], jnp.where(sub == PAD - 1, 0.0, nxt[-PAD:])], axis=0)
            out.append(prev * wc[0:1] + piece[own_rows] * wc[1:2] + nxt * wc[2:3]
                       + bconv_ref[0, :, cols])
        return out[0] if len(out) == 1 else jnp.concatenate(out, axis=0)

    for c in range(FFN_DIM // f):
        act = jax.nn.silu(up_conv(c * f)) * up_conv(FFN_DIM + c * f)
        act_scr[:, c * f:(c + 1) * f] = act.astype(BF16)
    acc = jnp.dot(act_scr[...], wdown_ref[0], preferred_element_type=F32)
    out = x1[main] + gate_ffn * acc
    if final:
        out = out * lax.rsqrt(jnp.mean(out * out, axis=-1, keepdims=True) + EPS) * gfin_ref[...]
    o_ref[...] = out


def _mix_ffn(x, m, mods3, mod_row_fn, g_ffn, g_fin, w_mo, w_up, w_conv, b_conv, w_down,
             *, layer, tm, seq, final):
    n, d = x.shape
    km = m.shape[1]
    halo = tm % seq != 0
    assert not halo or seq % tm == 0
    nh = n // HALO
    per = tm // HALO
    prev_map = lambda i: (jnp.maximum(i * per - 1, 0), 0, 0)
    next_map = lambda i: (jnp.minimum((i + 1) * per, nh - 1), 0, 0)
    kern = functools.partial(_mix_ffn_kernel, tm=tm, seq=seq, halo=halo, final=final)
    return pl.pallas_call(
        kern,
        grid=(n // tm,),
        in_specs=[
            pl.BlockSpec((tm, d), lambda i: (i, 0)),
            pl.BlockSpec((1, HALO, d), prev_map),
            pl.BlockSpec((1, HALO, d), next_map),
            pl.BlockSpec((tm, km), lambda i: (i, 0)),
            pl.BlockSpec((1, HALO, km), prev_map),
            pl.BlockSpec((1, HALO, km), next_map),
            pl.BlockSpec((1, 1, 6 * d), lambda i: (mod_row_fn(i), 0, 0)),
            _resident((1, d)),
            _resident((1, d)),
            _resident(w_mo.shape),
            _resident_layer(w_up.shape, layer),
            _resident_layer(w_conv.shape, layer),
            _resident_layer(b_conv.shape, layer),
            _resident_layer(w_down.shape, layer),
        ],
        out_specs=pl.BlockSpec((tm, d), lambda i: (i, 0)),
        out_shape=jax.ShapeDtypeStruct((n, d), F32),
        scratch_shapes=[pltpu.VMEM((tm, FFN_DIM), BF16)],
        compiler_params=_params(),
        name="mix_ffn",
    )(x, x.reshape(nh, HALO, d), x.reshape(nh, HALO, d),
      m, m.reshape(nh, HALO, km), m.reshape(nh, HALO, km),
      mods3, g_ffn, g_fin, w_mo, w_up, w_conv, b_conv, w_down)


def kernel(x_prompt, x_sample, cache_k, cache_v, c, c_ctx, w_ada, b_ada, norm_mix_g,
           norm_ffn_g, norm_final_g, w_qkv, w_attn_out, rpb, w_gmlp_in, g_gmlp_v,
           w_spatial, b_spatial, w_gmlp_out, w_ffn_up, w_ffn_conv, b_ffn_conv, w_ffn_down):
    batch, seq, d = x_prompt.shape
    dec_batch, dec_seq, _ = x_sample.shape
    depth = w_ada.shape[0]
    assert depth == 2 and d == D_MODEL

    cond = jnp.zeros((MOD_ROWS, d), F32).at[0].set(c_ctx).at[1:1 + dec_batch].set(c)
    mods3 = _ada(cond, w_ada, b_ada).reshape(depth * MOD_ROWS, 1, 6 * d)

    w_qkv_b = _to_bf16(w_qkv[0])
    w_ao_b = _to_bf16(w_attn_out[0])
    w_gin_b = _to_bf16(w_gmlp_in[0])
    w_gout_b = _to_bf16(w_gmlp_out[0])
    w_s_b = _to_bf16(w_spatial[0])
    b_s_full = jnp.broadcast_to(b_spatial[0][:, :, None], (GMLP_GROUPS, CHUNK, CHUNK))
    ffn_w = (_to_bf16(w_ffn_up), w_ffn_conv, b_ffn_conv[:, None, :], _to_bf16(w_ffn_down))
    g_mix = norm_mix_g.reshape(depth, 1, d)
    g_ffn = norm_ffn_g.reshape(depth, 1, d)
    g_fin = norm_final_g.reshape(1, d)
    tm = 512

    def ctx_row(layer):
        return lambda i: layer * MOD_ROWS
    x = x_prompt.reshape(batch * seq, d)
    o, state_k, state_v = _ctx_attn(x, mods3, 0, g_mix[0], w_qkv_b, batch=batch, seq=seq)
    x = _mix_ffn(x, o, mods3, ctx_row(0), g_ffn[0], g_fin, w_ao_b, *ffn_w,
                 layer=0, tm=tm, seq=seq, final=False)
    m = _gmlp(x, mods3, ctx_row(1), g_mix[1], w_gin_b, g_gmlp_v[0:1], w_s_b, b_s_full, tm=tm)
    y_prompt = _mix_ffn(x, m, mods3, ctx_row(1), g_ffn[1], g_fin, w_gout_b, *ffn_w,
                        layer=1, tm=tm, seq=seq, final=True).reshape(batch, seq, d)

    def lat_row(layer, rows=tm):
        return lambda i: layer * MOD_ROWS + 1 + (i * rows) // dec_seq
    x = x_sample.reshape(dec_batch * dec_seq, d)
    qkv = _qkv(x, mods3, lat_row(0, dec_seq), g_mix[0], w_qkv_b, tm=dec_seq)
    bias = _na_bias(rpb[0], dec_seq // GRID_W)
    o = _nattn(qkv.reshape(dec_batch, dec_seq, 3 * d), jnp.swapaxes(cache_k, 3, 4),
               jnp.swapaxes(cache_v, 3, 4), bias, layer=0)
    x = _mix_ffn(x, o.reshape(dec_batch * dec_seq, d), mods3, lat_row(0), g_ffn[0], g_fin,
                 w_ao_b, *ffn_w, layer=0, tm=tm, seq=dec_seq, final=False)
    m = _gmlp(x, mods3, lat_row(1), g_mix[1], w_gin_b, g_gmlp_v[0:1], w_s_b, b_s_full, tm=tm)
    y_sample = _mix_ffn(x, m, mods3, lat_row(1), g_ffn[1], g_fin, w_gout_b, *ffn_w,
                        layer=1, tm=tm, seq=dec_seq, final=True).reshape(dec_batch, dec_seq, d)

    return (y_prompt, y_sample, state_k, state_v)
```

```python
import functools
import itertools

import numpy as np
import jax
import jax.numpy as jnp
from jax import lax
from jax.experimental import pallas as pl
from jax.experimental.pallas import tpu as pltpu

D_MODEL = 1024
N_HEADS = 16
HEAD_DIM = D_MODEL // N_HEADS
GRID_W = 64
NA_ROWS = 8
NA_COLS = 16
CHUNK = 128
GMLP_WIDTH = 2 * D_MODEL
GMLP_GROUPS = 16
FFN_DIM = 2816
CONV_W = 3
EPS = 1e-6
ATTN_SCALE = HEAD_DIM ** -0.5

LANES = 128
HEADS_PER_BLOCK = LANES // HEAD_DIM
MOD_ROWS = 8
HALO = 16
PAD = 8
FFN_TILE = 256
NA_QROWS = 4
LOOKAHEAD = 1
NEG_BIAS = -1e30
VMEM_LIMIT = 56 * 1024 * 1024

BF16 = jnp.bfloat16
F32 = jnp.float32


def _params(n_axes=1):
    return pltpu.CompilerParams(
        dimension_semantics=("arbitrary",) * n_axes, vmem_limit_bytes=VMEM_LIMIT)


def _resident(shape):
    nd = len(shape)
    return pl.BlockSpec(shape, lambda *_: (0,) * nd, pipeline_mode=pl.Buffered(1))


def _resident_layer(shape, layer):
    nd = len(shape)
    return pl.BlockSpec((1,) + tuple(shape[1:]), lambda *_: (layer,) + (0,) * (nd - 1),
                        pipeline_mode=pl.Buffered(1))


def _pipelined(items, first_stage):
    ready = [first_stage(it) for it in items[:LOOKAHEAD]]
    for i, item in enumerate(items):
        if i + LOOKAHEAD < len(items):
            ready.append(first_stage(items[i + LOOKAHEAD]))
        yield item, ready.pop(0)


def _rms_mod(x, g, shift, scale):
    y = x * lax.rsqrt(jnp.mean(x * x, axis=-1, keepdims=True) + EPS)
    return (y * g) * (1 + scale) + shift


def _mod_chunk(mod, k):
    return mod[:, k * D_MODEL:(k + 1) * D_MODEL]


CAST_BLOCK_BYTES = 8 * 1024 * 1024


def _cast_kernel(w_ref, o_ref):
    o_ref[...] = w_ref[...].astype(o_ref.dtype)


def _to_bf16(w):
    shape = w.shape
    cols = shape[-1]
    rows = w.size // cols
    per = 16
    assert rows % per == 0
    blk = max(b for b in range(per, rows + 1, per)
              if rows % b == 0 and (b * cols * 4 <= CAST_BLOCK_BYTES or b == per))
    out = pl.pallas_call(
        _cast_kernel,
        grid=(rows // blk,),
        in_specs=[pl.BlockSpec((blk, cols), lambda i: (i, 0))],
        out_specs=pl.BlockSpec((blk, cols), lambda i: (i, 0)),
        out_shape=jax.ShapeDtypeStruct((rows, cols), BF16),
        compiler_params=_params(),
        name="to_bf16",
    )(w.reshape(rows, cols))
    return out.reshape(shape)


def _ada_kernel(cond_ref, w_ref, b_ref, o_ref):
    s = jax.nn.silu(cond_ref[...]).astype(BF16)
    o_ref[0] = jnp.dot(s, w_ref[0].astype(BF16), preferred_element_type=F32) + b_ref[0]


def _ada(cond, w_ada, b_ada):
    depth, d, n = w_ada.shape
    tn = 1536
    return pl.pallas_call(
        _ada_kernel,
        grid=(depth, n // tn),
        in_specs=[
            pl.BlockSpec((MOD_ROWS, d), lambda l, j: (0, 0)),
            pl.BlockSpec((1, d, tn), lambda l, j: (l, 0, j)),
            pl.BlockSpec((1, 1, tn), lambda l, j: (l, 0, j)),
        ],
        out_specs=pl.BlockSpec((1, MOD_ROWS, tn), lambda l, j: (l, 0, j)),
        out_shape=jax.ShapeDtypeStruct((depth, MOD_ROWS, n), F32),
        compiler_params=_params(2),
        name="ada",
    )(cond, w_ada, b_ada.reshape(depth, 1, n))


def _den_lane(e):
    return (1 - e) * HEAD_DIM


def _head_lanes(e, axis=1):
    ch = lax.broadcasted_iota(jnp.int32, (1, LANES) if axis == 1 else (LANES, 1), axis)
    own = (ch < HEAD_DIM) if e == 0 else (ch >= HEAD_DIM)
    return own, ch == _den_lane(e)


def _head_q(q2, e):
    own, _ = _head_lanes(e)
    return jnp.where(own, q2, jnp.zeros_like(q2))


def _head_v(v2, e, axis=1):
    own, is_den = _head_lanes(e, axis)
    return jnp.where(own, v2, jnp.broadcast_to(is_den.astype(v2.dtype), v2.shape))


def _normalise(accs):
    outs = [acc / acc[:, _den_lane(e):_den_lane(e) + 1] for e, acc in enumerate(accs)]
    own0, _ = _head_lanes(0)
    return jnp.where(own0, outs[0], outs[1])


def _ctx_attn_kernel(x_ref, mod_ref, g_ref, w_ref, o_ref, sk_ref, sv_ref, qkv_scr, *, nb, seq):
    mod = mod_ref[0]
    h = _rms_mod(x_ref[...], g_ref[...], _mod_chunk(mod, 0), _mod_chunk(mod, 1)).astype(BF16)
    qkv_scr[...] = jnp.dot(h, w_ref[...], preferred_element_type=F32)
    for b in range(nb):
        rows = slice(b * seq, (b + 1) * seq)
        for hh in range(N_HEADS):
            sk_ref[b, 0, hh] = qkv_scr[rows, D_MODEL + hh * HEAD_DIM:D_MODEL + (hh + 1) * HEAD_DIM]
            sv_ref[b, 0, hh] = qkv_scr[rows, 2 * D_MODEL + hh * HEAD_DIM:2 * D_MODEL + (hh + 1) * HEAD_DIM]

    def block(unit, part):
        b, hp = unit
        return (slice(b * seq, (b + 1) * seq),
                slice(part * D_MODEL + hp * LANES, part * D_MODEL + (hp + 1) * LANES))

    def scores(unit):
        q2 = (qkv_scr[block(unit, 0)] * ATTN_SCALE).astype(BF16)
        k2 = qkv_scr[block(unit, 1)].astype(BF16)
        return [lax.dot_general(_head_q(q2, e), k2, (((1,), (1,)), ((), ())),
                                preferred_element_type=F32) for e in range(HEADS_PER_BLOCK)]

    units = [(b, hp) for b in range(nb) for hp in range(N_HEADS // HEADS_PER_BLOCK)]
    for unit, per_head in _pipelined(units, scores):
        v2 = qkv_scr[block(unit, 2)].astype(BF16)
        o2 = None
        for e, s in enumerate(per_head):
            own, _ = _head_lanes(e)
            ve = jnp.where(own, v2, jnp.zeros_like(v2))
            p = jnp.exp(s - jnp.max(s, axis=-1, keepdims=True))
            oe = (jnp.dot(p.astype(BF16), ve, preferred_element_type=F32)
                  / jnp.sum(p, axis=-1, keepdims=True))
            o2 = oe if o2 is None else o2 + oe
        o_ref[block(unit, 0)] = o2.astype(BF16)


def _ctx_attn(x, mods3, mod_row, g, w_qkv, *, batch, seq):
    n, d = x.shape
    nb = 2
    tm = nb * seq
    kern = functools.partial(_ctx_attn_kernel, nb=nb, seq=seq)
    state = jax.ShapeDtypeStruct((batch, 1, N_HEADS, seq, HEAD_DIM), F32)
    state_spec = pl.BlockSpec((nb, 1, N_HEADS, seq, HEAD_DIM), lambda i: (i, 0, 0, 0, 0))
    return pl.pallas_call(
        kern,
        grid=(n // tm,),
        in_specs=[
            pl.BlockSpec((tm, d), lambda i: (i, 0)),
            pl.BlockSpec((1, 1, 6 * d), lambda i: (mod_row, 0, 0)),
            _resident((1, d)),
            _resident((d, 3 * d)),
        ],
        out_specs=[pl.BlockSpec((tm, d), lambda i: (i, 0)), state_spec, state_spec],
        out_shape=[jax.ShapeDtypeStruct((n, d), BF16), state, state],
        scratch_shapes=[pltpu.VMEM((tm, 3 * d), F32)],
        compiler_params=_params(),
        name="ctx_attn",
    )(x, mods3, g, w_qkv)


def _qkv_kernel(x_ref, mod_ref, g_ref, w_ref, o_ref, *, tn):
    mod = mod_ref[0]
    h = _rms_mod(x_ref[...], g_ref[...], _mod_chunk(mod, 0), _mod_chunk(mod, 1)).astype(BF16)
    for c in range(w_ref.shape[1] // tn):
        cols = slice(c * tn, (c + 1) * tn)
        o_ref[:, cols] = jnp.dot(h, w_ref[:, cols], preferred_element_type=F32).astype(BF16)


def _qkv(x, mods3, mod_row_fn, g, w_qkv, *, tm):
    n, d = x.shape
    nout = w_qkv.shape[1]
    return pl.pallas_call(
        functools.partial(_qkv_kernel, tn=512),
        grid=(n // tm,),
        in_specs=[
            pl.BlockSpec((tm, d), lambda i: (i, 0)),
            pl.BlockSpec((1, 1, 6 * d), lambda i: (mod_row_fn(i), 0, 0)),
            _resident((1, d)),
            _resident((d, nout)),
        ],
        out_specs=pl.BlockSpec((tm, nout), lambda i: (i, 0)),
        out_shape=jax.ShapeDtypeStruct((n, nout), BF16),
        compiler_params=_params(),
        name="qkv",
    )(x, mods3, g, w_qkv)


def _na_groups(rows):
    kh = min(NA_ROWS, rows)
    row_start = [min(max(r - kh // 2, 0), rows - kh) for r in range(rows)]
    groups, off = [], 0
    for r0 in range(0, rows, NA_QROWS):
        lo = min(row_start[r0:r0 + NA_QROWS])
        hi = max(row_start[r0:r0 + NA_QROWS]) + kh
        if (hi - lo) * GRID_W % LANES:
            if hi < rows:
                hi += 1
            else:
                lo -= 1
        nk = (hi - lo) * GRID_W
        groups.append((r0 * GRID_W, lo * GRID_W, nk, off))
        off += nk
    return tuple(groups), row_start, kh


def _na_bias(rpb, rows):
    groups, row_start, kh = _na_groups(rows)
    nh, nro, nco = rpb.shape
    w = GRID_W
    assert 2 * w == LANES
    plan = []
    for (q0, k0, nk, off) in groups:
        for qi in range(NA_QROWS):
            qr = q0 // w + qi
            rs = row_start[qr]
            tile_of = lambda kr: kr - qr + NA_ROWS - 1 if rs <= kr < rs + kh else nro
            for p in range(nk // LANES):
                kr = k0 // w + 2 * p
                plan.append((qi * w, off + p * LANES, tile_of(kr), tile_of(kr + 1)))
    nbias = sum(g[2] for g in groups)
    lead = w - NA_COLS
    padded = jnp.pad(rpb, ((0, 0), (0, 1), (lead, LANES - nco - lead)))
    return pl.pallas_call(
        functools.partial(_na_bias_kernel, plan=tuple(plan), n_tiles=nro),
        grid=(nh,),
        in_specs=[pl.BlockSpec((1, nro + 1, LANES), lambda h: (h, 0, 0))],
        out_specs=pl.BlockSpec((1, NA_QROWS * w, nbias), lambda h: (h, 0, 0)),
        out_shape=jax.ShapeDtypeStruct((nh, NA_QROWS * w, nbias), F32),
        scratch_shapes=[pltpu.VMEM((nro + 1, w, LANES), F32), pltpu.VMEM((nro + 1, w, LANES), F32)],
        compiler_params=_params(),
        name="na_bias",
    )(padded)


def _na_bias_kernel(g_ref, o_ref, lo_scr, hi_scr, *, plan, n_tiles):
    w = GRID_W
    lane = lax.broadcasted_iota(jnp.int32, (w, LANES), 1)
    qc = lax.broadcasted_iota(jnp.int32, (w, LANES), 0)
    kc = lane % w
    col_start = jnp.clip(qc - NA_COLS // 2, 0, w - NA_COLS)
    col_ok = (kc >= col_start) & (kc < col_start + NA_COLS)
    in_lo = lane < w
    neg = jnp.full((w, LANES), NEG_BIAS, F32)
    for a in range(n_tiles):
        row = jnp.broadcast_to(g_ref[0, a:a + 1, :], (w, LANES))
        lo_scr[a] = jnp.where(col_ok, pltpu.roll(row, w + 1, 1, stride=1, stride_axis=0), neg)
        hi_scr[a] = jnp.where(col_ok, pltpu.roll(row, 1, 1, stride=1, stride_axis=0), neg)
    lo_scr[n_tiles] = neg
    hi_scr[n_tiles] = neg
    for (r0, c0, a_lo, a_hi) in plan:
        o_ref[0, r0:r0 + w, c0:c0 + LANES] = jnp.where(in_lo, lo_scr[a_lo], hi_scr[a_hi])


def _nattn_kernel(q_ref, k_ref, v_ref, ck_ref, cv_ref, bias_ref, o_ref, *, groups):
    q2, k2, v2 = q_ref[0], k_ref[0], v_ref[0]
    past = ck_ref.shape[-1]
    kct = ck_ref[0, 0].reshape(LANES, past).astype(BF16)
    vct = cv_ref[0, 0].reshape(LANES, past).astype(BF16)
    q2 = q2 * ATTN_SCALE
    nq = NA_QROWS * GRID_W
    dn = (((1,), (1,)), ((), ()))
    units = [(grp, e) for grp in groups for e in range(HEADS_PER_BLOCK)]

    def scores(unit):
        (q0, k0, nk, boff), e = unit
        qe = _head_q(q2[q0:q0 + nq], e)
        s_w = (lax.dot_general(qe, k2[k0:k0 + nk], dn, preferred_element_type=F32)
               + bias_ref[e, :, boff:boff + nk])
        return s_w, jnp.dot(qe, kct, preferred_element_type=F32)

    def attend(unit, s_w, s_c):
        (_, k0, nk, _), e = unit
        mx = jnp.maximum(jnp.max(s_w, axis=-1, keepdims=True),
                         jnp.max(s_c, axis=-1, keepdims=True))
        return (jnp.dot(jnp.exp(s_w - mx).astype(BF16), _head_v(v2[k0:k0 + nk], e),
                        preferred_element_type=F32)
                + lax.dot_general(jnp.exp(s_c - mx).astype(BF16), _head_v(vct, e, axis=0), dn,
                                  preferred_element_type=F32))

    accs = []
    for unit, s in _pipelined(units, scores):
        accs.append(attend(unit, *s))
        if len(accs) == HEADS_PER_BLOCK:
            q0 = unit[0][0]
            o_ref[0, q0:q0 + nq, :] = _normalise(accs).astype(BF16)
            accs = []


def _nattn(qkv, cache_kt, cache_vt, bias, *, layer):
    b, t, _ = qkv.shape
    past = cache_kt.shape[4]
    groups, _, _ = _na_groups(t // GRID_W)
    nhp = N_HEADS // HEADS_PER_BLOCK
    nq, nbias = bias.shape[1], bias.shape[2]
    cache_spec = pl.BlockSpec((1, 1, HEADS_PER_BLOCK, HEAD_DIM, past),
                              lambda hp, bi: (bi, layer, hp, 0, 0))
    return pl.pallas_call(
        functools.partial(_nattn_kernel, groups=groups),
        grid=(nhp, b),
        in_specs=[
            pl.BlockSpec((1, t, LANES), lambda hp, bi: (bi, 0, hp)),
            pl.BlockSpec((1, t, LANES), lambda hp, bi: (bi, 0, nhp + hp)),
            pl.BlockSpec((1, t, LANES), lambda hp, bi: (bi, 0, 2 * nhp + hp)),
            cache_spec,
            cache_spec,
            pl.BlockSpec((HEADS_PER_BLOCK, nq, nbias), lambda hp, bi: (hp, 0, 0)),
        ],
        out_specs=pl.BlockSpec((1, t, LANES), lambda hp, bi: (bi, 0, hp)),
        out_shape=jax.ShapeDtypeStruct((b, t, D_MODEL), BF16),
        compiler_params=_params(2),
        name="nattn",
    )(qkv, qkv, qkv, cache_kt, cache_vt, bias)


def _gmlp_kernel(x_ref, mod_ref, g_ref, win_ref, gv_ref, ws_ref, bs_ref, o_ref, vv_scr, *, tm):
    e = GMLP_WIDTH
    gd = e // GMLP_GROUPS
    mod = mod_ref[0]
    h = _rms_mod(x_ref[...], g_ref[...], _mod_chunk(mod, 0), _mod_chunk(mod, 1)).astype(BF16)
    pair = 2 * gd

    def u_proj(gp):
        return jnp.dot(h, win_ref[:, gp * pair:(gp + 1) * pair], preferred_element_type=F32)

    zv = jax.nn.gelu(jnp.dot(h, win_ref[:, e:2 * e], preferred_element_type=F32))
    pipe = _pipelined(list(range(e // pair)), u_proj)
    head = next(pipe)
    xc = zv - jnp.mean(zv, axis=-1, keepdims=True)
    vv = xc * lax.rsqrt(jnp.mean(xc * xc, axis=-1, keepdims=True) + EPS) * gv_ref[...]
    vv_scr[...] = vv.astype(BF16)
    for gp, u_raw in itertools.chain([head], pipe):
        u2 = jax.nn.gelu(u_raw)
        for gi in range(2):
            g = 2 * gp + gi
            cols = slice(g * gd, (g + 1) * gd)
            for c in range(tm // CHUNK):
                rows = slice(c * CHUNK, (c + 1) * CHUNK)
                mixed = jnp.dot(ws_ref[g], vv_scr[rows, cols],
                                preferred_element_type=F32) + bs_ref[g]
                o_ref[rows, cols] = (u2[rows, gi * gd:(gi + 1) * gd] * mixed).astype(BF16)


def _gmlp(x, mods3, mod_row_fn, g, w_in, g_v, w_s, b_s_full, *, tm):
    n, d = x.shape
    e = GMLP_WIDTH
    return pl.pallas_call(
        functools.partial(_gmlp_kernel, tm=tm),
        grid=(n // tm,),
        in_specs=[
            pl.BlockSpec((tm, d), lambda i: (i, 0)),
            pl.BlockSpec((1, 1, 6 * d), lambda i: (mod_row_fn(i), 0, 0)),
            _resident((1, d)),
            _resident((d, 2 * e)),
            _resident((1, e)),
            _resident(w_s.shape),
            _resident(b_s_full.shape),
        ],
        out_specs=pl.BlockSpec((tm, e), lambda i: (i, 0)),
        out_shape=jax.ShapeDtypeStruct((n, e), BF16),
        scratch_shapes=[pltpu.VMEM((tm, e), BF16)],
        compiler_params=_params(),
        name="gmlp",
    )(x, mods3, g, w_in, g_v, w_s, b_s_full)


def _conv_pieces(tm, seq, halo):
    if halo:
        return ((0, tm + 2 * HALO, HALO, tm),)
    return tuple((s * seq, seq, 0, seq) for s in range(tm // seq))


def _mix_ffn_kernel(x_ref, xp_ref, xn_ref, m_ref, mp_ref, mn_ref, mod_ref, gffn_ref, gfin_ref,
                    wmo_ref, wup_ref, wconv_ref, bconv_ref, wdown_ref, o_ref, act_scr,
                    *, tm, seq, halo, final):
    mod = mod_ref[0]
    gate_mix, shift, scale, gate_ffn = (_mod_chunk(mod, k) for k in (2, 3, 4, 5))
    if halo:
        xcat = jnp.concatenate([xp_ref[0], x_ref[...], xn_ref[0]], axis=0)
        mcat = jnp.concatenate([mp_ref[0], m_ref[...], mn_ref[0]], axis=0)
        main = slice(HALO, HALO + tm)
    else:
        xcat, mcat = x_ref[...], m_ref[...]
        main = slice(0, tm)
    r = xcat.shape[0]
    x1 = xcat + gate_mix * jnp.dot(mcat, wmo_ref[...], preferred_element_type=F32)
    hcat = _rms_mod(x1, gffn_ref[...], shift, scale).astype(BF16)

    f = FFN_TILE
    if halo:
        blk = pl.program_id(0) % (seq // tm)
        hcat = jnp.concatenate([
            jnp.where(blk == 0, jnp.zeros((HALO, D_MODEL), BF16), hcat[:HALO]),
            hcat[main],
            jnp.where(blk == seq // tm - 1, jnp.zeros((HALO, D_MODEL), BF16), hcat[HALO + tm:]),
        ], axis=0)
    pieces = _conv_pieces(tm, seq, halo)
    sub = lax.broadcasted_iota(jnp.int32, (PAD, 1), 0)

    def up(c):
        return [jnp.dot(hcat, wup_ref[0, :, col:col + f], preferred_element_type=F32)
                for col in (c * f, FFN_DIM + c * f)]

    def conv(a, col):
        cols = slice(col, col + f)
        wc = wconv_ref[0, :, cols]
        out = []
        for (a0, n_rows, own0, own) in pieces:
            piece = a[a0:a0 + n_rows]
            own_rows = slice(own0, own0 + own)
            prev = pltpu.roll(piece, 1, 0)[own_rows]
            nxt = pltpu.roll(piece, n_rows - 1, 0)[own_rows]
            if not halo:
                prev = jnp.concatenate([jnp.where(sub == 0, 0.0, prev[:PAD]), prev[PAD:]], axis=0)
                nxt = jnp.concatenate([nxt[:-PAD], jnp.where(sub == PAD - 1, 0.0, nxt[-PAD:])], axis=0)
            out.append(prev * wc[0:1] + piece[own_rows] * wc[1:2] + nxt * wc[2:3]
                       + bconv_ref[0, :, cols])
        return out[0] if len(out) == 1 else jnp.concatenate(out, axis=0)

    for c, (a_gate, a_val) in _pipelined(list(range(FFN_DIM // f)), up):
        act = jax.nn.silu(conv(a_gate, c * f)) * conv(a_val, FFN_DIM + c * f)
        act_scr[:, c * f:(c + 1) * f] = act.astype(BF16)
    acc = jnp.dot(act_scr[...], wdown_ref[0], preferred_element_type=F32)
    out = x1[main] + gate_ffn * acc
    if final:
        out = out * lax.rsqrt(jnp.mean(out * out, axis=-1, keepdims=True) + EPS) * gfin_ref[...]
    o_ref[...] = out


def _mix_ffn(x, m, mods3, mod_row_fn, g_ffn, g_fin, w_mo, w_up, w_conv, b_conv, w_down,
             *, layer, tm, seq, final):
    n, d = x.shape
    km = m.shape[1]
    halo = tm % seq != 0
    assert not halo or seq % tm == 0
    nh = n // HALO
    per = tm // HALO
    prev_map = lambda i: (jnp.maximum(i * per - 1, 0), 0, 0)
    next_map = lambda i: (jnp.minimum((i + 1) * per, nh - 1), 0, 0)
    kern = functools.partial(_mix_ffn_kernel, tm=tm, seq=seq, halo=halo, final=final)
    return pl.pallas_call(
        kern,
        grid=(n // tm,),
        in_specs=[
            pl.BlockSpec((tm, d), lambda i: (i, 0)),
            pl.BlockSpec((1, HALO, d), prev_map),
            pl.BlockSpec((1, HALO, d), next_map),
            pl.BlockSpec((tm, km), lambda i: (i, 0)),
            pl.BlockSpec((1, HALO, km), prev_map),
            pl.BlockSpec((1, HALO, km), next_map),
            pl.BlockSpec((1, 1, 6 * d), lambda i: (mod_row_fn(i), 0, 0)),
            _resident((1, d)),
            _resident((1, d)),
            _resident(w_mo.shape),
            _resident_layer(w_up.shape, layer),
            _resident_layer(w_conv.shape, layer),
            _resident_layer(b_conv.shape, layer),
            _resident_layer(w_down.shape, layer),
        ],
        out_specs=pl.BlockSpec((tm, d), lambda i: (i, 0)),
        out_shape=jax.ShapeDtypeStruct((n, d), F32),
        scratch_shapes=[pltpu.VMEM((tm, FFN_DIM), BF16)],
        compiler_params=_params(),
        name="mix_ffn",
    )(x, x.reshape(nh, HALO, d), x.reshape(nh, HALO, d),
      m, m.reshape(nh, HALO, km), m.reshape(nh, HALO, km),
      mods3, g_ffn, g_fin, w_mo, w_up, w_conv, b_conv, w_down)


def kernel(x_prompt, x_sample, cache_k, cache_v, c, c_ctx, w_ada, b_ada, norm_mix_g,
           norm_ffn_g, norm_final_g, w_qkv, w_attn_out, rpb, w_gmlp_in, g_gmlp_v,
           w_spatial, b_spatial, w_gmlp_out, w_ffn_up, w_ffn_conv, b_ffn_conv, w_ffn_down):
    batch, seq, d = x_prompt.shape
    dec_batch, dec_seq, _ = x_sample.shape
    depth = w_ada.shape[0]
    assert depth == 2 and d == D_MODEL

    cond = jnp.zeros((MOD_ROWS, d), F32).at[0].set(c_ctx).at[1:1 + dec_batch].set(c)
    mods3 = _ada(cond, w_ada, b_ada).reshape(depth * MOD_ROWS, 1, 6 * d)

    w_qkv_b = _to_bf16(w_qkv[0])
    w_ao_b = _to_bf16(w_attn_out[0])
    w_gin_b = _to_bf16(w_gmlp_in[0])
    w_gout_b = _to_bf16(w_gmlp_out[0])
    w_s_b = _to_bf16(w_spatial[0])
    b_s_full = jnp.broadcast_to(b_spatial[0][:, :, None], (GMLP_GROUPS, CHUNK, CHUNK))
    ffn_w = (_to_bf16(w_ffn_up), w_ffn_conv, b_ffn_conv[:, None, :], _to_bf16(w_ffn_down))
    g_mix = norm_mix_g.reshape(depth, 1, d)
    g_ffn = norm_ffn_g.reshape(depth, 1, d)
    g_fin = norm_final_g.reshape(1, d)
    tm = 512

    def ctx_row(layer):
        return lambda i: layer * MOD_ROWS
    x = x_prompt.reshape(batch * seq, d)
    o, state_k, state_v = _ctx_attn(x, mods3, 0, g_mix[0], w_qkv_b, batch=batch, seq=seq)
    x = _mix_ffn(x, o, mods3, ctx_row(0), g_ffn[0], g_fin, w_ao_b, *ffn_w,
                 layer=0, tm=tm, seq=seq, final=False)
    m = _gmlp(x, mods3, ctx_row(1), g_mix[1], w_gin_b, g_gmlp_v[0:1], w_s_b, b_s_full, tm=tm)
    y_prompt = _mix_ffn(x, m, mods3, ctx_row(1), g_ffn[1], g_fin, w_gout_b, *ffn_w,
                        layer=1, tm=tm, seq=seq, final=True).reshape(batch, seq, d)

    def lat_row(layer, rows=tm):
        return lambda i: layer * MOD_ROWS + 1 + (i * rows) // dec_seq
    x = x_sample.reshape(dec_batch * dec_seq, d)
    qkv = _qkv(x, mods3, lat_row(0, dec_seq), g_mix[0], w_qkv_b, tm=dec_seq)
    bias = _na_bias(rpb[0], dec_seq // GRID_W)
    o = _nattn(qkv.reshape(dec_batch, dec_seq, 3 * d), jnp.swapaxes(cache_k, 3, 4),
               jnp.swapaxes(cache_v, 3, 4), bias, layer=0)
    x = _mix_ffn(x, o.reshape(dec_batch * dec_seq, d), mods3, lat_row(0), g_ffn[0], g_fin,
                 w_ao_b, *ffn_w, layer=0, tm=tm, seq=dec_seq, final=False)
    m = _gmlp(x, mods3, lat_row(1), g_mix[1], w_gin_b, g_gmlp_v[0:1], w_s_b, b_s_full, tm=tm)
    y_sample = _mix_ffn(x, m, mods3, lat_row(1), g_ffn[1], g_fin, w_gout_b, *ffn_w,
                        layer=1, tm=tm, seq=dec_seq, final=True).reshape(dec_batch, dec_seq, d)

    return (y_prompt, y_sample, state_k, state_v)
```

```python
import functools
import itertools

import numpy as np
import jax
import jax.numpy as jnp
from jax import lax
from jax.experimental import pallas as pl
from jax.experimental.pallas import tpu as pltpu

D_MODEL = 1024
N_HEADS = 16
HEAD_DIM = D_MODEL // N_HEADS
GRID_W = 64
NA_ROWS = 8
NA_COLS = 16
CHUNK = 128
GMLP_WIDTH = 2 * D_MODEL
GMLP_GROUPS = 16
FFN_DIM = 2816
CONV_W = 3
EPS = 1e-6
ATTN_SCALE = HEAD_DIM ** -0.5

LANES = 128
HEADS_PER_BLOCK = LANES // HEAD_DIM
MOD_ROWS = 8
HALO = 16
PAD = 8
FFN_TILE = 256
FFN_CAST_STEPS = 8
NA_QROWS = 4
LOOKAHEAD = 1
NEG_BIAS = -1e30
VMEM_LIMIT = 56 * 1024 * 1024

BF16 = jnp.bfloat16
F32 = jnp.float32


def _params(n_axes=1):
    return pltpu.CompilerParams(
        dimension_semantics=("arbitrary",) * n_axes, vmem_limit_bytes=VMEM_LIMIT)


def _resident(shape):
    nd = len(shape)
    return pl.BlockSpec(shape, lambda *_: (0,) * nd, pipeline_mode=pl.Buffered(1))


def _resident_layer(shape, layer):
    nd = len(shape)
    return pl.BlockSpec((1,) + tuple(shape[1:]), lambda *_: (layer,) + (0,) * (nd - 1),
                        pipeline_mode=pl.Buffered(1))


def _pipelined(items, first_stage):
    ready = [first_stage(it) for it in items[:LOOKAHEAD]]
    for i, item in enumerate(items):
        if i + LOOKAHEAD < len(items):
            ready.append(first_stage(items[i + LOOKAHEAD]))
        yield item, ready.pop(0)


def _rms_mod(x, g, shift, scale):
    y = x * lax.rsqrt(jnp.mean(x * x, axis=-1, keepdims=True) + EPS)
    return (y * g) * (1 + scale) + shift


def _mod_chunk(mod, k):
    return mod[:, k * D_MODEL:(k + 1) * D_MODEL]


CAST_BLOCK_BYTES = 8 * 1024 * 1024


def _cast_kernel(w_ref, o_ref):
    o_ref[...] = w_ref[...].astype(o_ref.dtype)


def _to_bf16(w):
    shape = w.shape
    cols = shape[-1]
    rows = w.size // cols
    per = 16
    assert rows % per == 0
    blk = max(b for b in range(per, rows + 1, per)
              if rows % b == 0 and (b * cols * 4 <= CAST_BLOCK_BYTES or b == per))
    out = pl.pallas_call(
        _cast_kernel,
        grid=(rows // blk,),
        in_specs=[pl.BlockSpec((blk, cols), lambda i: (i, 0))],
        out_specs=pl.BlockSpec((blk, cols), lambda i: (i, 0)),
        out_shape=jax.ShapeDtypeStruct((rows, cols), BF16),
        compiler_params=_params(),
        name="to_bf16",
    )(w.reshape(rows, cols))
    return out.reshape(shape)


def _ada_kernel(cond_ref, w_ref, b_ref, o_ref):
    s = jax.nn.silu(cond_ref[...]).astype(BF16)
    o_ref[0] = jnp.dot(s, w_ref[0].astype(BF16), preferred_element_type=F32) + b_ref[0]


def _ada(cond, w_ada, b_ada):
    depth, d, n = w_ada.shape
    tn = 1536
    return pl.pallas_call(
        _ada_kernel,
        grid=(depth, n // tn),
        in_specs=[
            pl.BlockSpec((MOD_ROWS, d), lambda l, j: (0, 0)),
            pl.BlockSpec((1, d, tn), lambda l, j: (l, 0, j)),
            pl.BlockSpec((1, 1, tn), lambda l, j: (l, 0, j)),
        ],
        out_specs=pl.BlockSpec((1, MOD_ROWS, tn), lambda l, j: (l, 0, j)),
        out_shape=jax.ShapeDtypeStruct((depth, MOD_ROWS, n), F32),
        compiler_params=_params(2),
        name="ada",
    )(cond, w_ada, b_ada.reshape(depth, 1, n))


def _den_lane(e):
    return (1 - e) * HEAD_DIM


def _head_lanes(e, axis=1):
    ch = lax.broadcasted_iota(jnp.int32, (1, LANES) if axis == 1 else (LANES, 1), axis)
    own = (ch < HEAD_DIM) if e == 0 else (ch >= HEAD_DIM)
    return own, ch == _den_lane(e)


def _head_q(q2, e):
    own, _ = _head_lanes(e)
    return jnp.where(own, q2, jnp.zeros_like(q2))


def _head_v(v2, e, axis=1):
    own, is_den = _head_lanes(e, axis)
    return jnp.where(own, v2, jnp.broadcast_to(is_den.astype(v2.dtype), v2.shape))


def _normalise(accs):
    outs = [acc / acc[:, _den_lane(e):_den_lane(e) + 1] for e, acc in enumerate(accs)]
    own0, _ = _head_lanes(0)
    return jnp.where(own0, outs[0], outs[1])


def _ctx_attn_kernel(x_ref, mod_ref, g_ref, w_ref, o_ref, sk_ref, sv_ref, qkv_scr, *, nb, seq):
    mod = mod_ref[0]
    h = _rms_mod(x_ref[...], g_ref[...], _mod_chunk(mod, 0), _mod_chunk(mod, 1)).astype(BF16)
    qkv_scr[...] = jnp.dot(h, w_ref[...], preferred_element_type=F32)
    for b in range(nb):
        rows = slice(b * seq, (b + 1) * seq)
        for hh in range(N_HEADS):
            sk_ref[b, 0, hh] = qkv_scr[rows, D_MODEL + hh * HEAD_DIM:D_MODEL + (hh + 1) * HEAD_DIM]
            sv_ref[b, 0, hh] = qkv_scr[rows, 2 * D_MODEL + hh * HEAD_DIM:2 * D_MODEL + (hh + 1) * HEAD_DIM]

    def block(unit, part):
        b, hp = unit
        return (slice(b * seq, (b + 1) * seq),
                slice(part * D_MODEL + hp * LANES, part * D_MODEL + (hp + 1) * LANES))

    def scores(unit):
        q2 = (qkv_scr[block(unit, 0)] * ATTN_SCALE).astype(BF16)
        k2 = qkv_scr[block(unit, 1)].astype(BF16)
        return [lax.dot_general(_head_q(q2, e), k2, (((1,), (1,)), ((), ())),
                                preferred_element_type=F32) for e in range(HEADS_PER_BLOCK)]

    units = [(b, hp) for b in range(nb) for hp in range(N_HEADS // HEADS_PER_BLOCK)]
    for unit, per_head in _pipelined(units, scores):
        v2 = qkv_scr[block(unit, 2)].astype(BF16)
        o2 = None
        for e, s in enumerate(per_head):
            own, _ = _head_lanes(e)
            ve = jnp.where(own, v2, jnp.zeros_like(v2))
            p = jnp.exp(s - jnp.max(s, axis=-1, keepdims=True))
            oe = (jnp.dot(p.astype(BF16), ve, preferred_element_type=F32)
                  / jnp.sum(p, axis=-1, keepdims=True))
            o2 = oe if o2 is None else o2 + oe
        o_ref[block(unit, 0)] = o2.astype(BF16)


def _ctx_attn(x, mods3, mod_row, g, w_qkv, *, batch, seq):
    n, d = x.shape
    nb = 2
    tm = nb * seq
    kern = functools.partial(_ctx_attn_kernel, nb=nb, seq=seq)
    state = jax.ShapeDtypeStruct((batch, 1, N_HEADS, seq, HEAD_DIM), F32)
    state_spec = pl.BlockSpec((nb, 1, N_HEADS, seq, HEAD_DIM), lambda i: (i, 0, 0, 0, 0))
    return pl.pallas_call(
        kern,
        grid=(n // tm,),
        in_specs=[
            pl.BlockSpec((tm, d), lambda i: (i, 0)),
            pl.BlockSpec((1, 1, 6 * d), lambda i: (mod_row, 0, 0)),
            _resident((1, d)),
            _resident((d, 3 * d)),
        ],
        out_specs=[pl.BlockSpec((tm, d), lambda i: (i, 0)), state_spec, state_spec],
        out_shape=[jax.ShapeDtypeStruct((n, d), BF16), state, state],
        scratch_shapes=[pltpu.VMEM((tm, 3 * d), F32)],
        compiler_params=_params(),
        name="ctx_attn",
    )(x, mods3, g, w_qkv)


def _qkv_kernel(x_ref, mod_ref, g_ref, w_ref, o_ref, *, tn):
    mod = mod_ref[0]
    h = _rms_mod(x_ref[...], g_ref[...], _mod_chunk(mod, 0), _mod_chunk(mod, 1)).astype(BF16)
    for c in range(w_ref.shape[1] // tn):
        cols = slice(c * tn, (c + 1) * tn)
        o_ref[:, cols] = jnp.dot(h, w_ref[:, cols], preferred_element_type=F32).astype(BF16)


def _qkv(x, mods3, mod_row_fn, g, w_qkv, *, tm):
    n, d = x.shape
    nout = w_qkv.shape[1]
    return pl.pallas_call(
        functools.partial(_qkv_kernel, tn=512),
        grid=(n // tm,),
        in_specs=[
            pl.BlockSpec((tm, d), lambda i: (i, 0)),
            pl.BlockSpec((1, 1, 6 * d), lambda i: (mod_row_fn(i), 0, 0)),
            _resident((1, d)),
            _resident((d, nout)),
        ],
        out_specs=pl.BlockSpec((tm, nout), lambda i: (i, 0)),
        out_shape=jax.ShapeDtypeStruct((n, nout), BF16),
        compiler_params=_params(),
        name="qkv",
    )(x, mods3, g, w_qkv)


def _na_groups(rows):
    kh = min(NA_ROWS, rows)
    row_start = [min(max(r - kh // 2, 0), rows - kh) for r in range(rows)]
    groups, off = [], 0
    for r0 in range(0, rows, NA_QROWS):
        lo = min(row_start[r0:r0 + NA_QROWS])
        hi = max(row_start[r0:r0 + NA_QROWS]) + kh
        if (hi - lo) * GRID_W % LANES:
            if hi < rows:
                hi += 1
            else:
                lo -= 1
        nk = (hi - lo) * GRID_W
        groups.append((r0 * GRID_W, lo * GRID_W, nk, off))
        off += nk
    return tuple(groups), row_start, kh


def _na_bias(rpb, rows):
    groups, row_start, kh = _na_groups(rows)
    nh, nro, nco = rpb.shape
    w = GRID_W
    assert 2 * w == LANES
    plan = []
    for (q0, k0, nk, off) in groups:
        for qi in range(NA_QROWS):
            qr = q0 // w + qi
            rs = row_start[qr]
            tile_of = lambda kr: kr - qr + NA_ROWS - 1 if rs <= kr < rs + kh else nro
            for p in range(nk // LANES):
                kr = k0 // w + 2 * p
                plan.append((qi * w, off + p * LANES, tile_of(kr), tile_of(kr + 1)))
    nbias = sum(g[2] for g in groups)
    lead = w - NA_COLS
    padded = jnp.pad(rpb, ((0, 0), (0, 1), (lead, LANES - nco - lead)))
    return pl.pallas_call(
        functools.partial(_na_bias_kernel, plan=tuple(plan), n_tiles=nro),
        grid=(nh,),
        in_specs=[pl.BlockSpec((1, nro + 1, LANES), lambda h: (h, 0, 0))],
        out_specs=pl.BlockSpec((1, NA_QROWS * w, nbias), lambda h: (h, 0, 0)),
        out_shape=jax.ShapeDtypeStruct((nh, NA_QROWS * w, nbias), F32),
        scratch_shapes=[pltpu.VMEM((nro + 1, w, LANES), F32), pltpu.VMEM((nro + 1, w, LANES), F32)],
        compiler_params=_params(),
        name="na_bias",
    )(padded)


def _na_bias_kernel(g_ref, o_ref, lo_scr, hi_scr, *, plan, n_tiles):
    w = GRID_W
    lane = lax.broadcasted_iota(jnp.int32, (w, LANES), 1)
    qc = lax.broadcasted_iota(jnp.int32, (w, LANES), 0)
    kc = lane % w
    col_start = jnp.clip(qc - NA_COLS // 2, 0, w - NA_COLS)
    col_ok = (kc >= col_start) & (kc < col_start + NA_COLS)
    in_lo = lane < w
    neg = jnp.full((w, LANES), NEG_BIAS, F32)
    for a in range(n_tiles):
        row = jnp.broadcast_to(g_ref[0, a:a + 1, :], (w, LANES))
        lo_scr[a] = jnp.where(col_ok, pltpu.roll(row, w + 1, 1, stride=1, stride_axis=0), neg)
        hi_scr[a] = jnp.where(col_ok, pltpu.roll(row, 1, 1, stride=1, stride_axis=0), neg)
    lo_scr[n_tiles] = neg
    hi_scr[n_tiles] = neg
    for (r0, c0, a_lo, a_hi) in plan:
        o_ref[0, r0:r0 + w, c0:c0 + LANES] = jnp.where(in_lo, lo_scr[a_lo], hi_scr[a_hi])


def _nattn_kernel(q_ref, k_ref, v_ref, ck_ref, cv_ref, bias_ref, o_ref, *, groups):
    q2, k2, v2 = q_ref[0], k_ref[0], v_ref[0]
    past = ck_ref.shape[-1]
    kct = ck_ref[0, 0].reshape(LANES, past).astype(BF16)
    vct = cv_ref[0, 0].reshape(LANES, past).astype(BF16)
    q2 = q2 * ATTN_SCALE
    nq = NA_QROWS * GRID_W
    dn = (((1,), (1,)), ((), ()))
    units = [(grp, e) for grp in groups for e in range(HEADS_PER_BLOCK)]

    def scores(unit):
        (q0, k0, nk, boff), e = unit
        qe = _head_q(q2[q0:q0 + nq], e)
        s_w = (lax.dot_general(qe, k2[k0:k0 + nk], dn, preferred_element_type=F32)
               + bias_ref[e, :, boff:boff + nk])
        return s_w, jnp.dot(qe, kct, preferred_element_type=F32)

    def attend(unit, s_w, s_c):
        (_, k0, nk, _), e = unit
        mx = jnp.maximum(jnp.max(s_w, axis=-1, keepdims=True),
                         jnp.max(s_c, axis=-1, keepdims=True))
        return (jnp.dot(jnp.exp(s_w - mx).astype(BF16), _head_v(v2[k0:k0 + nk], e),
                        preferred_element_type=F32)
                + lax.dot_general(jnp.exp(s_c - mx).astype(BF16), _head_v(vct, e, axis=0), dn,
                                  preferred_element_type=F32))

    accs = []
    for unit, s in _pipelined(units, scores):
        accs.append(attend(unit, *s))
        if len(accs) == HEADS_PER_BLOCK:
            q0 = unit[0][0]
            o_ref[0, q0:q0 + nq, :] = _normalise(accs).astype(BF16)
            accs = []


def _nattn(qkv, cache_kt, cache_vt, bias, *, layer):
    b, t, _ = qkv.shape
    past = cache_kt.shape[4]
    groups, _, _ = _na_groups(t // GRID_W)
    nhp = N_HEADS // HEADS_PER_BLOCK
    nq, nbias = bias.shape[1], bias.shape[2]
    cache_spec = pl.BlockSpec((1, 1, HEADS_PER_BLOCK, HEAD_DIM, past),
                              lambda hp, bi: (bi, layer, hp, 0, 0))
    return pl.pallas_call(
        functools.partial(_nattn_kernel, groups=groups),
        grid=(nhp, b),
        in_specs=[
            pl.BlockSpec((1, t, LANES), lambda hp, bi: (bi, 0, hp)),
            pl.BlockSpec((1, t, LANES), lambda hp, bi: (bi, 0, nhp + hp)),
            pl.BlockSpec((1, t, LANES), lambda hp, bi: (bi, 0, 2 * nhp + hp)),
            cache_spec,
            cache_spec,
            pl.BlockSpec((HEADS_PER_BLOCK, nq, nbias), lambda hp, bi: (hp, 0, 0)),
        ],
        out_specs=pl.BlockSpec((1, t, LANES), lambda hp, bi: (bi, 0, hp)),
        out_shape=jax.ShapeDtypeStruct((b, t, D_MODEL), BF16),
        compiler_params=_params(2),
        name="nattn",
    )(qkv, qkv, qkv, cache_kt, cache_vt, bias)


def _gmlp_kernel(x_ref, mod_ref, g_ref, win_ref, gv_ref, ws_ref, bs_ref, o_ref, vv_scr, *, tm):
    e = GMLP_WIDTH
    gd = e // GMLP_GROUPS
    mod = mod_ref[0]
    h = _rms_mod(x_ref[...], g_ref[...], _mod_chunk(mod, 0), _mod_chunk(mod, 1)).astype(BF16)
    pair = 2 * gd

    def u_proj(gp):
        return jnp.dot(h, win_ref[:, gp * pair:(gp + 1) * pair], preferred_element_type=F32)

    zv = jax.nn.gelu(jnp.dot(h, win_ref[:, e:2 * e], preferred_element_type=F32))
    pipe = _pipelined(list(range(e // pair)), u_proj)
    head = next(pipe)
    xc = zv - jnp.mean(zv, axis=-1, keepdims=True)
    vv = xc * lax.rsqrt(jnp.mean(xc * xc, axis=-1, keepdims=True) + EPS) * gv_ref[...]
    vv_scr[...] = vv.astype(BF16)
    for gp, u_raw in itertools.chain([head], pipe):
        u2 = jax.nn.gelu(u_raw)
        for gi in range(2):
            g = 2 * gp + gi
            cols = slice(g * gd, (g + 1) * gd)
            for c in range(tm // CHUNK):
                rows = slice(c * CHUNK, (c + 1) * CHUNK)
                mixed = jnp.dot(ws_ref[g], vv_scr[rows, cols],
                                preferred_element_type=F32) + bs_ref[g]
                o_ref[rows, cols] = (u2[rows, gi * gd:(gi + 1) * gd] * mixed).astype(BF16)


def _gmlp(x, mods3, mod_row_fn, g, w_in, g_v, w_s, b_s_full, *, tm):
    n, d = x.shape
    e = GMLP_WIDTH
    return pl.pallas_call(
        functools.partial(_gmlp_kernel, tm=tm),
        grid=(n // tm,),
        in_specs=[
            pl.BlockSpec((tm, d), lambda i: (i, 0)),
            pl.BlockSpec((1, 1, 6 * d), lambda i: (mod_row_fn(i), 0, 0)),
            _resident((1, d)),
            _resident((d, 2 * e)),
            _resident((1, e)),
            _resident(w_s.shape),
            _resident(b_s_full.shape),
        ],
        out_specs=pl.BlockSpec((tm, e), lambda i: (i, 0)),
        out_shape=jax.ShapeDtypeStruct((n, e), BF16),
        scratch_shapes=[pltpu.VMEM((tm, e), BF16)],
        compiler_params=_params(),
        name="gmlp",
    )(x, mods3, g, w_in, g_v, w_s, b_s_full)


def _conv_pieces(tm, seq, halo):
    if halo:
        return ((0, tm + 2 * HALO, HALO, tm),)
    return tuple((s * seq, seq, 0, seq) for s in range(tm // seq))


def _mix_ffn_kernel(x_ref, xp_ref, xn_ref, m_ref, mp_ref, mn_ref, mod_ref, gffn_ref, gfin_ref,
                    wmo_f32, wup_f32, wconv_ref, bconv_ref, wdown_f32, o_ref,
                    wmo_ref, wup_ref, wdown_ref, act_scr, *, n_cast, **block_args):
    step = pl.program_id(0)

    @pl.when(step < n_cast)
    def _():
        for src, dst in ((wmo_f32, wmo_ref), (wup_f32, wup_ref), (wdown_f32, wdown_ref)):
            rows = src.shape[-2]
            dst[pl.ds(pl.multiple_of(step * rows, rows), rows), :] = (
                src[...].reshape(rows, src.shape[-1]).astype(BF16))

    @pl.when(step >= n_cast)
    def _():
        _mix_ffn_block(x_ref, xp_ref, xn_ref, m_ref, mp_ref, mn_ref, mod_ref, gffn_ref, gfin_ref,
                       wmo_ref, wup_ref, wconv_ref, bconv_ref, wdown_ref, o_ref, act_scr,
                       block=step - n_cast, **block_args)


def _mix_ffn_block(x_ref, xp_ref, xn_ref, m_ref, mp_ref, mn_ref, mod_ref, gffn_ref, gfin_ref,
                   wmo_ref, wup_ref, wconv_ref, bconv_ref, wdown_ref, o_ref, act_scr,
                   *, block, tm, seq, halo, final):
    mod = mod_ref[0]
    gate_mix, shift, scale, gate_ffn = (_mod_chunk(mod, k) for k in (2, 3, 4, 5))
    if halo:
        xcat = jnp.concatenate([xp_ref[0], x_ref[...], xn_ref[0]], axis=0)
        mcat = jnp.concatenate([mp_ref[0], m_ref[...], mn_ref[0]], axis=0)
        main = slice(HALO, HALO + tm)
    else:
        xcat, mcat = x_ref[...], m_ref[...]
        main = slice(0, tm)
    r = xcat.shape[0]
    x1 = xcat + gate_mix * jnp.dot(mcat, wmo_ref[...], preferred_element_type=F32)
    hcat = _rms_mod(x1, gffn_ref[...], shift, scale).astype(BF16)

    f = FFN_TILE
    if halo:
        blk = block % (seq // tm)
        hcat = jnp.concatenate([
            jnp.where(blk == 0, jnp.zeros((HALO, D_MODEL), BF16), hcat[:HALO]),
            hcat[main],
            jnp.where(blk == seq // tm - 1, jnp.zeros((HALO, D_MODEL), BF16), hcat[HALO + tm:]),
        ], axis=0)
    pieces = _conv_pieces(tm, seq, halo)
    sub = lax.broadcasted_iota(jnp.int32, (PAD, 1), 0)

    def up(c):
        return [jnp.dot(hcat, wup_ref[:, col:col + f], preferred_element_type=F32)
                for col in (c * f, FFN_DIM + c * f)]

    def conv(a, col):
        cols = slice(col, col + f)
        wc = wconv_ref[0, :, cols]
        out = []
        for (a0, n_rows, own0, own) in pieces:
            piece = a[a0:a0 + n_rows]
            own_rows = slice(own0, own0 + own)
            prev = pltpu.roll(piece, 1, 0)[own_rows]
            nxt = pltpu.roll(piece, n_rows - 1, 0)[own_rows]
            if not halo:
                prev = jnp.concatenate([jnp.where(sub == 0, 0.0, prev[:PAD]), prev[PAD:]], axis=0)
                nxt = jnp.concatenate([nxt[:-PAD], jnp.where(sub == PAD - 1, 0.0, nxt[-PAD:])], axis=0)
            out.append(prev * wc[0:1] + piece[own_rows] * wc[1:2] + nxt * wc[2:3]
                       + bconv_ref[0, :, cols])
        return out[0] if len(out) == 1 else jnp.concatenate(out, axis=0)

    for c, (a_gate, a_val) in _pipelined(list(range(FFN_DIM // f)), up):
        act = jax.nn.silu(conv(a_gate, c * f)) * conv(a_val, FFN_DIM + c * f)
        act_scr[:, c * f:(c + 1) * f] = act.astype(BF16)
    acc = jnp.dot(act_scr[...], wdown_ref[...], preferred_element_type=F32)
    out = x1[main] + gate_ffn * acc
    if final:
        out = out * lax.rsqrt(jnp.mean(out * out, axis=-1, keepdims=True) + EPS) * gfin_ref[...]
    o_ref[...] = out


def _mix_ffn(x, m, mods3, mod_row_fn, g_ffn, g_fin, w_mo, w_up, w_conv, b_conv, w_down,
             *, layer, tm, seq, final):
    n, d = x.shape
    km = m.shape[1]
    fdim = w_down.shape[1]
    halo = tm % seq != 0
    assert not halo or seq % tm == 0
    nh = n // HALO
    per = tm // HALO
    nc = FFN_CAST_STEPS
    assert all(rows % (nc * HALO) == 0 for rows in (km, d, fdim))
    blk = lambda i: jnp.maximum(i - nc, 0)
    chunk = lambda i: jnp.minimum(i, nc - 1)
    prev_map = lambda i: (jnp.maximum(blk(i) * per - 1, 0), 0, 0)
    next_map = lambda i: (jnp.minimum((blk(i) + 1) * per, nh - 1), 0, 0)
    kern = functools.partial(_mix_ffn_kernel, n_cast=nc, tm=tm, seq=seq, halo=halo, final=final)
    return pl.pallas_call(
        kern,
        grid=(nc + n // tm,),
        in_specs=[
            pl.BlockSpec((tm, d), lambda i: (blk(i), 0)),
            pl.BlockSpec((1, HALO, d), prev_map),
            pl.BlockSpec((1, HALO, d), next_map),
            pl.BlockSpec((tm, km), lambda i: (blk(i), 0)),
            pl.BlockSpec((1, HALO, km), prev_map),
            pl.BlockSpec((1, HALO, km), next_map),
            pl.BlockSpec((1, 1, 6 * d), lambda i: (mod_row_fn(blk(i)), 0, 0)),
            _resident((1, d)),
            _resident((1, d)),
            pl.BlockSpec((km // nc, d), lambda i: (chunk(i), 0)),
            pl.BlockSpec((1, d // nc, 2 * fdim), lambda i: (layer, chunk(i), 0)),
            _resident_layer(w_conv.shape, layer),
            _resident_layer(b_conv.shape, layer),
            pl.BlockSpec((1, fdim // nc, d), lambda i: (layer, chunk(i), 0)),
        ],
        out_specs=pl.BlockSpec((tm, d), lambda i: (blk(i), 0)),
        out_shape=jax.ShapeDtypeStruct((n, d), F32),
        scratch_shapes=[pltpu.VMEM((km, d), BF16), pltpu.VMEM((d, 2 * fdim), BF16),
                        pltpu.VMEM((fdim, d), BF16), pltpu.VMEM((tm, FFN_DIM), BF16)],
        compiler_params=_params(),
        name="mix_ffn",
    )(x, x.reshape(nh, HALO, d), x.reshape(nh, HALO, d),
      m, m.reshape(nh, HALO, km), m.reshape(nh, HALO, km),
      mods3, g_ffn, g_fin, w_mo, w_up, w_conv, b_conv, w_down)


def kernel(x_prompt, x_sample, cache_k, cache_v, c, c_ctx, w_ada, b_ada, norm_mix_g,
           norm_ffn_g, norm_final_g, w_qkv, w_attn_out, rpb, w_gmlp_in, g_gmlp_v,
           w_spatial, b_spatial, w_gmlp_out, w_ffn_up, w_ffn_conv, b_ffn_conv, w_ffn_down):
    batch, seq, d = x_prompt.shape
    dec_batch, dec_seq, _ = x_sample.shape
    depth = w_ada.shape[0]
    assert depth == 2 and d == D_MODEL

    cond = jnp.zeros((MOD_ROWS, d), F32).at[0].set(c_ctx).at[1:1 + dec_batch].set(c)
    mods3 = _ada(cond, w_ada, b_ada).reshape(depth * MOD_ROWS, 1, 6 * d)

    w_qkv_b = _to_bf16(w_qkv[0])
    w_ao_b = w_attn_out[0]
    w_gin_b = _to_bf16(w_gmlp_in[0])
    w_gout_b = w_gmlp_out[0]
    w_s_b = _to_bf16(w_spatial[0])
    b_s_full = jnp.broadcast_to(b_spatial[0][:, :, None], (GMLP_GROUPS, CHUNK, CHUNK))
    ffn_w = (w_ffn_up, w_ffn_conv, b_ffn_conv[:, None, :], w_ffn_down)
    g_mix = norm_mix_g.reshape(depth, 1, d)
    g_ffn = norm_ffn_g.reshape(depth, 1, d)
    g_fin = norm_final_g.reshape(1, d)
    tm = 512

    def ctx_row(layer):
        return lambda i: layer * MOD_ROWS
    x = x_prompt.reshape(batch * seq, d)
    o, state_k, state_v = _ctx_attn(x, mods3, 0, g_mix[0], w_qkv_b, batch=batch, seq=seq)
    x = _mix_ffn(x, o, mods3, ctx_row(0), g_ffn[0], g_fin, w_ao_b, *ffn_w,
                 layer=0, tm=tm, seq=seq, final=False)
    m = _gmlp(x, mods3, ctx_row(1), g_mix[1], w_gin_b, g_gmlp_v[0:1], w_s_b, b_s_full, tm=tm)
    y_prompt = _mix_ffn(x, m, mods3, ctx_row(1), g_ffn[1], g_fin, w_gout_b, *ffn_w,
                        layer=1, tm=tm, seq=seq, final=True).reshape(batch, seq, d)

    def lat_row(layer, rows=tm):
        return lambda i: layer * MOD_ROWS + 1 + (i * rows) // dec_seq
    x = x_sample.reshape(dec_batch * dec_seq, d)
    qkv = _qkv(x, mods3, lat_row(0, dec_seq), g_mix[0], w_qkv_b, tm=dec_seq)
    bias = _na_bias(rpb[0], dec_seq // GRID_W)
    o = _nattn(qkv.reshape(dec_batch, dec_seq, 3 * d), jnp.swapaxes(cache_k, 3, 4),
               jnp.swapaxes(cache_v, 3, 4), bias, layer=0)
    x = _mix_ffn(x, o.reshape(dec_batch * dec_seq, d), mods3, lat_row(0), g_ffn[0], g_fin,
                 w_ao_b, *ffn_w, layer=0, tm=tm, seq=dec_seq, final=False)
    m = _gmlp(x, mods3, lat_row(1), g_mix[1], w_gin_b, g_gmlp_v[0:1], w_s_b, b_s_full, tm=tm)
    y_sample = _mix_ffn(x, m, mods3, lat_row(1), g_ffn[1], g_fin, w_gout_b, *ffn_w,
                        layer=1, tm=tm, seq=dec_seq, final=True).reshape(dec_batch, dec_seq, d)

    return (y_prompt, y_sample, state_k, state_v)
```

```python
import functools
import itertools
import math

import numpy as np
import jax
import jax.numpy as jnp
from jax import lax
from jax.experimental import pallas as pl
from jax.experimental.pallas import tpu as pltpu

D_MODEL = 1024
N_HEADS = 16
HEAD_DIM = D_MODEL // N_HEADS
GRID_W = 64
NA_ROWS = 8
NA_COLS = 16
CHUNK = 128
GMLP_WIDTH = 2 * D_MODEL
GMLP_GROUPS = 16
FFN_DIM = 2816
CONV_W = 3
EPS = 1e-6
ATTN_SCALE = HEAD_DIM ** -0.5

LANES = 128
HEADS_PER_BLOCK = LANES // HEAD_DIM
MOD_ROWS = 8
HALO = 16
PAD = 8
FFN_TILE = 256
CAST_STEPS = 8
NA_QROWS = 4
LOOKAHEAD = 1
NEG_BIAS = -1e30
VMEM_LIMIT = 56 * 1024 * 1024

BF16 = jnp.bfloat16
F32 = jnp.float32


def _params(n_axes=1):
    return pltpu.CompilerParams(
        dimension_semantics=("arbitrary",) * n_axes, vmem_limit_bytes=VMEM_LIMIT)


def _resident(shape):
    nd = len(shape)
    return pl.BlockSpec(shape, lambda *_: (0,) * nd, pipeline_mode=pl.Buffered(1))


def _resident_layer(shape, layer):
    nd = len(shape)
    return pl.BlockSpec((1,) + tuple(shape[1:]), lambda *_: (layer,) + (0,) * (nd - 1),
                        pipeline_mode=pl.Buffered(1))


def _pipelined(items, first_stage):
    ready = [first_stage(it) for it in items[:LOOKAHEAD]]
    for i, item in enumerate(items):
        if i + LOOKAHEAD < len(items):
            ready.append(first_stage(items[i + LOOKAHEAD]))
        yield item, ready.pop(0)


def _cast_then_run(n_cast, weights, run):
    step = pl.program_id(0)

    @pl.when(step < n_cast)
    def _():
        for src, dst in weights:
            rows = src.shape[-2]
            dst[pl.ds(pl.multiple_of(step * rows, rows), rows), :] = (
                src[...].reshape(rows, src.shape[-1]).astype(BF16))

    @pl.when(step >= n_cast)
    def _():
        run(step - n_cast)


def _cast_maps(n_cast):
    return (lambda i: jnp.maximum(i - n_cast, 0)), (lambda i: jnp.minimum(i, n_cast - 1))


def _rms_mod(x, g, shift, scale):
    y = x * lax.rsqrt(jnp.mean(x * x, axis=-1, keepdims=True) + EPS)
    return (y * g) * (1 + scale) + shift


def _mod_chunk(mod, k):
    return mod[:, k * D_MODEL:(k + 1) * D_MODEL]


CAST_BLOCK_BYTES = 8 * 1024 * 1024


def _cast_kernel(w_ref, o_ref):
    o_ref[...] = w_ref[...].astype(o_ref.dtype)


def _to_bf16(w):
    shape = w.shape
    cols = shape[-1]
    rows = w.size // cols
    per = 16
    assert rows % per == 0
    blk = max(b for b in range(per, rows + 1, per)
              if rows % b == 0 and (b * cols * 4 <= CAST_BLOCK_BYTES or b == per))
    out = pl.pallas_call(
        _cast_kernel,
        grid=(rows // blk,),
        in_specs=[pl.BlockSpec((blk, cols), lambda i: (i, 0))],
        out_specs=pl.BlockSpec((blk, cols), lambda i: (i, 0)),
        out_shape=jax.ShapeDtypeStruct((rows, cols), BF16),
        compiler_params=_params(),
        name="to_bf16",
    )(w.reshape(rows, cols))
    return out.reshape(shape)


def _ada_kernel(cond_ref, w_ref, b_ref, o_ref):
    s = jax.nn.silu(cond_ref[...]).astype(BF16)
    o_ref[0] = jnp.dot(s, w_ref[0].astype(BF16), preferred_element_type=F32) + b_ref[0]


def _ada(cond, w_ada, b_ada):
    depth, d, n = w_ada.shape
    tn = 1536
    return pl.pallas_call(
        _ada_kernel,
        grid=(depth, n // tn),
        in_specs=[
            pl.BlockSpec((MOD_ROWS, d), lambda l, j: (0, 0)),
            pl.BlockSpec((1, d, tn), lambda l, j: (l, 0, j)),
            pl.BlockSpec((1, 1, tn), lambda l, j: (l, 0, j)),
        ],
        out_specs=pl.BlockSpec((1, MOD_ROWS, tn), lambda l, j: (l, 0, j)),
        out_shape=jax.ShapeDtypeStruct((depth, MOD_ROWS, n), F32),
        compiler_params=_params(2),
        name="ada",
    )(cond, w_ada, b_ada.reshape(depth, 1, n))


def _den_lane(e):
    return (1 - e) * HEAD_DIM


def _head_lanes(e, axis=1):
    ch = lax.broadcasted_iota(jnp.int32, (1, LANES) if axis == 1 else (LANES, 1), axis)
    own = (ch < HEAD_DIM) if e == 0 else (ch >= HEAD_DIM)
    return own, ch == _den_lane(e)


def _head_q(q2, e):
    own, _ = _head_lanes(e)
    return jnp.where(own, q2, jnp.zeros_like(q2))


def _head_v(v2, e, axis=1):
    own, is_den = _head_lanes(e, axis)
    return jnp.where(own, v2, jnp.broadcast_to(is_den.astype(v2.dtype), v2.shape))


def _normalise(accs):
    outs = [acc / acc[:, _den_lane(e):_den_lane(e) + 1] for e, acc in enumerate(accs)]
    own0, _ = _head_lanes(0)
    return jnp.where(own0, outs[0], outs[1])


def _ctx_attn_kernel(x_ref, mod_ref, g_ref, w_ref, o_ref, sk_ref, sv_ref, qkv_scr, *, nb, seq):
    mod = mod_ref[0]
    h = _rms_mod(x_ref[...], g_ref[...], _mod_chunk(mod, 0), _mod_chunk(mod, 1)).astype(BF16)
    qkv_scr[...] = jnp.dot(h, w_ref[...], preferred_element_type=F32)
    for b in range(nb):
        rows = slice(b * seq, (b + 1) * seq)
        for hh in range(N_HEADS):
            sk_ref[b, 0, hh] = qkv_scr[rows, D_MODEL + hh * HEAD_DIM:D_MODEL + (hh + 1) * HEAD_DIM]
            sv_ref[b, 0, hh] = qkv_scr[rows, 2 * D_MODEL + hh * HEAD_DIM:2 * D_MODEL + (hh + 1) * HEAD_DIM]

    def block(unit, part):
        b, hp = unit
        return (slice(b * seq, (b + 1) * seq),
                slice(part * D_MODEL + hp * LANES, part * D_MODEL + (hp + 1) * LANES))

    def scores(unit):
        q2 = (qkv_scr[block(unit, 0)] * ATTN_SCALE).astype(BF16)
        k2 = qkv_scr[block(unit, 1)].astype(BF16)
        return [lax.dot_general(_head_q(q2, e), k2, (((1,), (1,)), ((), ())),
                                preferred_element_type=F32) for e in range(HEADS_PER_BLOCK)]

    units = [(b, hp) for b in range(nb) for hp in range(N_HEADS // HEADS_PER_BLOCK)]
    for unit, per_head in _pipelined(units, scores):
        v2 = qkv_scr[block(unit, 2)].astype(BF16)
        o2 = None
        for e, s in enumerate(per_head):
            own, _ = _head_lanes(e)
            ve = jnp.where(own, v2, jnp.zeros_like(v2))
            p = jnp.exp(s - jnp.max(s, axis=-1, keepdims=True))
            oe = (jnp.dot(p.astype(BF16), ve, preferred_element_type=F32)
                  / jnp.sum(p, axis=-1, keepdims=True))
            o2 = oe if o2 is None else o2 + oe
        o_ref[block(unit, 0)] = o2.astype(BF16)


def _ctx_attn(x, mods3, mod_row, g, w_qkv, *, batch, seq):
    n, d = x.shape
    nb = 2
    tm = nb * seq
    kern = functools.partial(_ctx_attn_kernel, nb=nb, seq=seq)
    state = jax.ShapeDtypeStruct((batch, 1, N_HEADS, seq, HEAD_DIM), F32)
    state_spec = pl.BlockSpec((nb, 1, N_HEADS, seq, HEAD_DIM), lambda i: (i, 0, 0, 0, 0))
    return pl.pallas_call(
        kern,
        grid=(n // tm,),
        in_specs=[
            pl.BlockSpec((tm, d), lambda i: (i, 0)),
            pl.BlockSpec((1, 1, 6 * d), lambda i: (mod_row, 0, 0)),
            _resident((1, d)),
            _resident((d, 3 * d)),
        ],
        out_specs=[pl.BlockSpec((tm, d), lambda i: (i, 0)), state_spec, state_spec],
        out_shape=[jax.ShapeDtypeStruct((n, d), BF16), state, state],
        scratch_shapes=[pltpu.VMEM((tm, 3 * d), F32)],
        compiler_params=_params(),
        name="ctx_attn",
    )(x, mods3, g, w_qkv)


def _qkv_kernel(x_ref, mod_ref, g_ref, w_ref, o_ref, *, tn):
    mod = mod_ref[0]
    h = _rms_mod(x_ref[...], g_ref[...], _mod_chunk(mod, 0), _mod_chunk(mod, 1)).astype(BF16)
    for c in range(w_ref.shape[1] // tn):
        cols = slice(c * tn, (c + 1) * tn)
        o_ref[:, cols] = jnp.dot(h, w_ref[:, cols], preferred_element_type=F32).astype(BF16)


def _qkv(x, mods3, mod_row_fn, g, w_qkv, *, tm):
    n, d = x.shape
    nout = w_qkv.shape[1]
    return pl.pallas_call(
        functools.partial(_qkv_kernel, tn=512),
        grid=(n // tm,),
        in_specs=[
            pl.BlockSpec((tm, d), lambda i: (i, 0)),
            pl.BlockSpec((1, 1, 6 * d), lambda i: (mod_row_fn(i), 0, 0)),
            _resident((1, d)),
            _resident((d, nout)),
        ],
        out_specs=pl.BlockSpec((tm, nout), lambda i: (i, 0)),
        out_shape=jax.ShapeDtypeStruct((n, nout), BF16),
        compiler_params=_params(),
        name="qkv",
    )(x, mods3, g, w_qkv)


def _na_groups(rows):
    kh = min(NA_ROWS, rows)
    row_start = [min(max(r - kh // 2, 0), rows - kh) for r in range(rows)]
    groups, off = [], 0
    for r0 in range(0, rows, NA_QROWS):
        lo = min(row_start[r0:r0 + NA_QROWS])
        hi = max(row_start[r0:r0 + NA_QROWS]) + kh
        if (hi - lo) * GRID_W % LANES:
            if hi < rows:
                hi += 1
            else:
                lo -= 1
        nk = (hi - lo) * GRID_W
        groups.append((r0 * GRID_W, lo * GRID_W, nk, off))
        off += nk
    return tuple(groups), row_start, kh


def _na_bias(rpb, rows):
    groups, row_start, kh = _na_groups(rows)
    nh, nro, nco = rpb.shape
    w = GRID_W
    assert 2 * w == LANES
    plan = []
    for (q0, k0, nk, off) in groups:
        for qi in range(NA_QROWS):
            qr = q0 // w + qi
            rs = row_start[qr]
            tile_of = lambda kr: kr - qr + NA_ROWS - 1 if rs <= kr < rs + kh else nro
            for p in range(nk // LANES):
                kr = k0 // w + 2 * p
                plan.append((qi * w, off + p * LANES, tile_of(kr), tile_of(kr + 1)))
    nbias = sum(g[2] for g in groups)
    lead = w - NA_COLS
    padded = jnp.pad(rpb, ((0, 0), (0, 1), (lead, LANES - nco - lead)))
    return pl.pallas_call(
        functools.partial(_na_bias_kernel, plan=tuple(plan), n_tiles=nro),
        grid=(nh,),
        in_specs=[pl.BlockSpec((1, nro + 1, LANES), lambda h: (h, 0, 0))],
        out_specs=pl.BlockSpec((1, NA_QROWS * w, nbias), lambda h: (h, 0, 0)),
        out_shape=jax.ShapeDtypeStruct((nh, NA_QROWS * w, nbias), F32),
        scratch_shapes=[pltpu.VMEM((nro + 1, w, LANES), F32), pltpu.VMEM((nro + 1, w, LANES), F32)],
        compiler_params=_params(),
        name="na_bias",
    )(padded)


def _na_bias_kernel(g_ref, o_ref, lo_scr, hi_scr, *, plan, n_tiles):
    w = GRID_W
    lane = lax.broadcasted_iota(jnp.int32, (w, LANES), 1)
    qc = lax.broadcasted_iota(jnp.int32, (w, LANES), 0)
    kc = lane % w
    col_start = jnp.clip(qc - NA_COLS // 2, 0, w - NA_COLS)
    col_ok = (kc >= col_start) & (kc < col_start + NA_COLS)
    in_lo = lane < w
    neg = jnp.full((w, LANES), NEG_BIAS, F32)
    for a in range(n_tiles):
        row = jnp.broadcast_to(g_ref[0, a:a + 1, :], (w, LANES))
        lo_scr[a] = jnp.where(col_ok, pltpu.roll(row, w + 1, 1, stride=1, stride_axis=0), neg)
        hi_scr[a] = jnp.where(col_ok, pltpu.roll(row, 1, 1, stride=1, stride_axis=0), neg)
    lo_scr[n_tiles] = neg
    hi_scr[n_tiles] = neg
    for (r0, c0, a_lo, a_hi) in plan:
        o_ref[0, r0:r0 + w, c0:c0 + LANES] = jnp.where(in_lo, lo_scr[a_lo], hi_scr[a_hi])


def _nattn_kernel(q_ref, k_ref, v_ref, ck_ref, cv_ref, bias_ref, o_ref, *, groups):
    q2, k2, v2 = q_ref[0], k_ref[0], v_ref[0]
    past = ck_ref.shape[-1]
    kct = ck_ref[0, 0].reshape(LANES, past).astype(BF16)
    vct = cv_ref[0, 0].reshape(LANES, past).astype(BF16)
    q2 = q2 * ATTN_SCALE
    nq = NA_QROWS * GRID_W
    dn = (((1,), (1,)), ((), ()))
    units = [(grp, e) for grp in groups for e in range(HEADS_PER_BLOCK)]

    def scores(unit):
        (q0, k0, nk, boff), e = unit
        qe = _head_q(q2[q0:q0 + nq], e)
        s_w = (lax.dot_general(qe, k2[k0:k0 + nk], dn, preferred_element_type=F32)
               + bias_ref[e, :, boff:boff + nk])
        return s_w, jnp.dot(qe, kct, preferred_element_type=F32)

    def attend(unit, s_w, s_c):
        (_, k0, nk, _), e = unit
        mx = jnp.maximum(jnp.max(s_w, axis=-1, keepdims=True),
                         jnp.max(s_c, axis=-1, keepdims=True))
        return (jnp.dot(jnp.exp(s_w - mx).astype(BF16), _head_v(v2[k0:k0 + nk], e),
                        preferred_element_type=F32)
                + lax.dot_general(jnp.exp(s_c - mx).astype(BF16), _head_v(vct, e, axis=0), dn,
                                  preferred_element_type=F32))

    accs = []
    for unit, s in _pipelined(units, scores):
        accs.append(attend(unit, *s))
        if len(accs) == HEADS_PER_BLOCK:
            q0 = unit[0][0]
            o_ref[0, q0:q0 + nq, :] = _normalise(accs).astype(BF16)
            accs = []


def _nattn(qkv, cache_kt, cache_vt, bias, *, layer):
    b, t, _ = qkv.shape
    past = cache_kt.shape[4]
    groups, _, _ = _na_groups(t // GRID_W)
    nhp = N_HEADS // HEADS_PER_BLOCK
    nq, nbias = bias.shape[1], bias.shape[2]
    cache_spec = pl.BlockSpec((1, 1, HEADS_PER_BLOCK, HEAD_DIM, past),
                              lambda hp, bi: (bi, layer, hp, 0, 0))
    return pl.pallas_call(
        functools.partial(_nattn_kernel, groups=groups),
        grid=(nhp, b),
        in_specs=[
            pl.BlockSpec((1, t, LANES), lambda hp, bi: (bi, 0, hp)),
            pl.BlockSpec((1, t, LANES), lambda hp, bi: (bi, 0, nhp + hp)),
            pl.BlockSpec((1, t, LANES), lambda hp, bi: (bi, 0, 2 * nhp + hp)),
            cache_spec,
            cache_spec,
            pl.BlockSpec((HEADS_PER_BLOCK, nq, nbias), lambda hp, bi: (hp, 0, 0)),
        ],
        out_specs=pl.BlockSpec((1, t, LANES), lambda hp, bi: (bi, 0, hp)),
        out_shape=jax.ShapeDtypeStruct((b, t, D_MODEL), BF16),
        compiler_params=_params(2),
        name="nattn",
    )(qkv, qkv, qkv, cache_kt, cache_vt, bias)


GELU_C = float(np.float32(np.sqrt(2 / np.pi)))
GELU_K = -2.0 * GELU_C * math.log2(math.e)


def _gelu(x):
    return x / (1.0 + jnp.exp2(x * (x * x * (GELU_K * 0.044715) + GELU_K)))


def _gmlp_kernel(x_ref, mod_ref, g_ref, win_f32, gv_ref, ws_ref, bs_ref, o_ref, win_ref, vv_scr,
                 *, n_cast, tm):
    def run(_):
        _gmlp_block(x_ref, mod_ref, g_ref, win_ref, gv_ref, ws_ref, bs_ref, o_ref, vv_scr, tm=tm)

    _cast_then_run(n_cast, ((win_f32, win_ref),), run)


def _gmlp_block(x_ref, mod_ref, g_ref, win_ref, gv_ref, ws_ref, bs_ref, o_ref, vv_scr, *, tm):
    e = GMLP_WIDTH
    gd = e // GMLP_GROUPS
    mod = mod_ref[0]
    h = _rms_mod(x_ref[...], g_ref[...], _mod_chunk(mod, 0), _mod_chunk(mod, 1)).astype(BF16)
    pair = 2 * gd

    def u_proj(gp):
        return jnp.dot(h, win_ref[:, gp * pair:(gp + 1) * pair], preferred_element_type=F32)

    zv = _gelu(jnp.dot(h, win_ref[:, e:2 * e], preferred_element_type=F32))
    pipe = _pipelined(list(range(e // pair)), u_proj)
    head = next(pipe)
    xc = zv - jnp.mean(zv, axis=-1, keepdims=True)
    vv = xc * lax.rsqrt(jnp.mean(xc * xc, axis=-1, keepdims=True) + EPS) * gv_ref[...]
    vv_scr[...] = vv.astype(BF16)
    for gp, u_raw in itertools.chain([head], pipe):
        u2 = _gelu(u_raw)
        for gi in range(2):
            g = 2 * gp + gi
            cols = slice(g * gd, (g + 1) * gd)
            ws_g = ws_ref[g].astype(BF16)
            for c in range(tm // CHUNK):
                rows = slice(c * CHUNK, (c + 1) * CHUNK)
                mixed = jnp.dot(ws_g, vv_scr[rows, cols],
                                preferred_element_type=F32) + bs_ref[g]
                o_ref[rows, cols] = (u2[rows, gi * gd:(gi + 1) * gd] * mixed).astype(BF16)


def _gmlp(x, mods3, mod_row_fn, g, w_in, g_v, w_s, b_s_full, *, tm):
    n, d = x.shape
    e = GMLP_WIDTH
    nc = CAST_STEPS
    assert d % (nc * HALO) == 0
    blk, chunk = _cast_maps(nc)
    return pl.pallas_call(
        functools.partial(_gmlp_kernel, n_cast=nc, tm=tm),
        grid=(nc + n // tm,),
        in_specs=[
            pl.BlockSpec((tm, d), lambda i: (blk(i), 0)),
            pl.BlockSpec((1, 1, 6 * d), lambda i: (mod_row_fn(blk(i)), 0, 0)),
            _resident((1, d)),
            pl.BlockSpec((d // nc, 2 * e), lambda i: (chunk(i), 0)),
            _resident((1, e)),
            _resident(w_s.shape),
            _resident(b_s_full.shape),
        ],
        out_specs=pl.BlockSpec((tm, e), lambda i: (blk(i), 0)),
        out_shape=jax.ShapeDtypeStruct((n, e), BF16),
        scratch_shapes=[pltpu.VMEM((d, 2 * e), BF16), pltpu.VMEM((tm, e), BF16)],
        compiler_params=_params(),
        name="gmlp",
    )(x, mods3, g, w_in, g_v, w_s, b_s_full)


def _conv_pieces(tm, seq, halo):
    if halo:
        return ((0, tm + 2 * HALO, HALO, tm),)
    return tuple((s * seq, seq, 0, seq) for s in range(tm // seq))


def _mix_ffn_kernel(x_ref, xp_ref, xn_ref, m_ref, mp_ref, mn_ref, mod_ref, gffn_ref, gfin_ref,
                    wmo_f32, wup_f32, wconv_ref, bconv_ref, wdown_f32, o_ref,
                    wmo_ref, wup_ref, wdown_ref, act_scr, *, n_cast, **block_args):
    def run(block):
        _mix_ffn_block(x_ref, xp_ref, xn_ref, m_ref, mp_ref, mn_ref, mod_ref, gffn_ref, gfin_ref,
                       wmo_ref, wup_ref, wconv_ref, bconv_ref, wdown_ref, o_ref, act_scr,
                       block=block, **block_args)

    _cast_then_run(n_cast, ((wmo_f32, wmo_ref), (wup_f32, wup_ref), (wdown_f32, wdown_ref)), run)


def _mix_ffn_block(x_ref, xp_ref, xn_ref, m_ref, mp_ref, mn_ref, mod_ref, gffn_ref, gfin_ref,
                   wmo_ref, wup_ref, wconv_ref, bconv_ref, wdown_ref, o_ref, act_scr,
                   *, block, tm, seq, halo, final):
    mod = mod_ref[0]
    gate_mix, shift, scale, gate_ffn = (_mod_chunk(mod, k) for k in (2, 3, 4, 5))
    if halo:
        xcat = jnp.concatenate([xp_ref[0], x_ref[...], xn_ref[0]], axis=0)
        mcat = jnp.concatenate([mp_ref[0], m_ref[...], mn_ref[0]], axis=0)
        main = slice(HALO, HALO + tm)
    else:
        xcat, mcat = x_ref[...], m_ref[...]
        main = slice(0, tm)
    r = xcat.shape[0]
    x1 = xcat + gate_mix * jnp.dot(mcat, wmo_ref[...], preferred_element_type=F32)
    hcat = _rms_mod(x1, gffn_ref[...], shift, scale).astype(BF16)

    f = FFN_TILE
    if halo:
        blk = block % (seq // tm)
        hcat = jnp.concatenate([
            jnp.where(blk == 0, jnp.zeros((HALO, D_MODEL), BF16), hcat[:HALO]),
            hcat[main],
            jnp.where(blk == seq // tm - 1, jnp.zeros((HALO, D_MODEL), BF16), hcat[HALO + tm:]),
        ], axis=0)
    pieces = _conv_pieces(tm, seq, halo)
    sub = lax.broadcasted_iota(jnp.int32, (PAD, 1), 0)

    def up(c):
        return [jnp.dot(hcat, wup_ref[:, col:col + f], preferred_element_type=F32)
                for col in (c * f, FFN_DIM + c * f)]

    def conv(a, col):
        cols = slice(col, col + f)
        wc = wconv_ref[0, :, cols]
        out = []
        for (a0, n_rows, own0, own) in pieces:
            piece = a[a0:a0 + n_rows]
            own_rows = slice(own0, own0 + own)
            prev = pltpu.roll(piece, 1, 0)[own_rows]
            nxt = pltpu.roll(piece, n_rows - 1, 0)[own_rows]
            if not halo:
                prev = jnp.concatenate([jnp.where(sub == 0, 0.0, prev[:PAD]), prev[PAD:]], axis=0)
                nxt = jnp.concatenate([nxt[:-PAD], jnp.where(sub == PAD - 1, 0.0, nxt[-PAD:])], axis=0)
            out.append(prev * wc[0:1] + piece[own_rows] * wc[1:2] + nxt * wc[2:3]
                       + bconv_ref[0, :, cols])
        return out[0] if len(out) == 1 else jnp.concatenate(out, axis=0)

    for c, (a_gate, a_val) in _pipelined(list(range(FFN_DIM // f)), up):
        act = jax.nn.silu(conv(a_gate, c * f)) * conv(a_val, FFN_DIM + c * f)
        act_scr[:, c * f:(c + 1) * f] = act.astype(BF16)
    acc = jnp.dot(act_scr[...], wdown_ref[...], preferred_element_type=F32)
    out = x1[main] + gate_ffn * acc
    if final:
        out = out * lax.rsqrt(jnp.mean(out * out, axis=-1, keepdims=True) + EPS) * gfin_ref[...]
    o_ref[...] = out


def _mix_ffn(x, m, mods3, mod_row_fn, g_ffn, g_fin, w_mo, w_up, w_conv, b_conv, w_down,
             *, layer, tm, seq, final):
    n, d = x.shape
    km = m.shape[1]
    fdim = w_down.shape[1]
    halo = tm % seq != 0
    assert not halo or seq % tm == 0
    nh = n // HALO
    per = tm // HALO
    nc = CAST_STEPS
    assert all(rows % (nc * HALO) == 0 for rows in (km, d, fdim))
    blk, chunk = _cast_maps(nc)
    prev_map = lambda i: (jnp.maximum(blk(i) * per - 1, 0), 0, 0)
    next_map = lambda i: (jnp.minimum((blk(i) + 1) * per, nh - 1), 0, 0)
    kern = functools.partial(_mix_ffn_kernel, n_cast=nc, tm=tm, seq=seq, halo=halo, final=final)
    return pl.pallas_call(
        kern,
        grid=(nc + n // tm,),
        in_specs=[
            pl.BlockSpec((tm, d), lambda i: (blk(i), 0)),
            pl.BlockSpec((1, HALO, d), prev_map),
            pl.BlockSpec((1, HALO, d), next_map),
            pl.BlockSpec((tm, km), lambda i: (blk(i), 0)),
            pl.BlockSpec((1, HALO, km), prev_map),
            pl.BlockSpec((1, HALO, km), next_map),
            pl.BlockSpec((1, 1, 6 * d), lambda i: (mod_row_fn(blk(i)), 0, 0)),
            _resident((1, d)),
            _resident((1, d)),
            pl.BlockSpec((km // nc, d), lambda i: (chunk(i), 0)),
            pl.BlockSpec((1, d // nc, 2 * fdim), lambda i: (layer, chunk(i), 0)),
            _resident_layer(w_conv.shape, layer),
            _resident_layer(b_conv.shape, layer),
            pl.BlockSpec((1, fdim // nc, d), lambda i: (layer, chunk(i), 0)),
        ],
        out_specs=pl.BlockSpec((tm, d), lambda i: (blk(i), 0)),
        out_shape=jax.ShapeDtypeStruct((n, d), F32),
        scratch_shapes=[pltpu.VMEM((km, d), BF16), pltpu.VMEM((d, 2 * fdim), BF16),
                        pltpu.VMEM((fdim, d), BF16), pltpu.VMEM((tm, FFN_DIM), BF16)],
        compiler_params=_params(),
        name="mix_ffn",
    )(x, x.reshape(nh, HALO, d), x.reshape(nh, HALO, d),
      m, m.reshape(nh, HALO, km), m.reshape(nh, HALO, km),
      mods3, g_ffn, g_fin, w_mo, w_up, w_conv, b_conv, w_down)


def kernel(x_prompt, x_sample, cache_k, cache_v, c, c_ctx, w_ada, b_ada, norm_mix_g,
           norm_ffn_g, norm_final_g, w_qkv, w_attn_out, rpb, w_gmlp_in, g_gmlp_v,
           w_spatial, b_spatial, w_gmlp_out, w_ffn_up, w_ffn_conv, b_ffn_conv, w_ffn_down):
    batch, seq, d = x_prompt.shape
    dec_batch, dec_seq, _ = x_sample.shape
    depth = w_ada.shape[0]
    assert depth == 2 and d == D_MODEL

    cond = jnp.zeros((MOD_ROWS, d), F32).at[0].set(c_ctx).at[1:1 + dec_batch].set(c)
    mods3 = _ada(cond, w_ada, b_ada).reshape(depth * MOD_ROWS, 1, 6 * d)

    w_qkv_b = _to_bf16(w_qkv[0])
    w_ao = w_attn_out[0]
    w_gin = w_gmlp_in[0]
    w_gout = w_gmlp_out[0]
    w_sp = w_spatial[0]
    b_s_full = jnp.broadcast_to(b_spatial[0][:, :, None], (GMLP_GROUPS, CHUNK, CHUNK))
    ffn_w = (w_ffn_up, w_ffn_conv, b_ffn_conv[:, None, :], w_ffn_down)
    g_mix = norm_mix_g.reshape(depth, 1, d)
    g_ffn = norm_ffn_g.reshape(depth, 1, d)
    g_fin = norm_final_g.reshape(1, d)
    tm = 512

    def ctx_row(layer):
        return lambda i: layer * MOD_ROWS
    x = x_prompt.reshape(batch * seq, d)
    o, state_k, state_v = _ctx_attn(x, mods3, 0, g_mix[0], w_qkv_b, batch=batch, seq=seq)
    x = _mix_ffn(x, o, mods3, ctx_row(0), g_ffn[0], g_fin, w_ao, *ffn_w,
                 layer=0, tm=tm, seq=seq, final=False)
    m = _gmlp(x, mods3, ctx_row(1), g_mix[1], w_gin, g_gmlp_v[0:1], w_sp, b_s_full, tm=tm)
    y_prompt = _mix_ffn(x, m, mods3, ctx_row(1), g_ffn[1], g_fin, w_gout, *ffn_w,
                        layer=1, tm=tm, seq=seq, final=True).reshape(batch, seq, d)

    def lat_row(layer):
        return lambda i: layer * MOD_ROWS + 1 + (i * tm) // dec_seq
    x = x_sample.reshape(dec_batch * dec_seq, d)
    qkv = _qkv(x, mods3, lat_row(0), g_mix[0], w_qkv_b, tm=tm)
    bias = _na_bias(rpb[0], dec_seq // GRID_W)
    o = _nattn(qkv.reshape(dec_batch, dec_seq, 3 * d), jnp.swapaxes(cache_k, 3, 4),
               jnp.swapaxes(cache_v, 3, 4), bias, layer=0)
    x = _mix_ffn(x, o.reshape(dec_batch * dec_seq, d), mods3, lat_row(0), g_ffn[0], g_fin,
                 w_ao, *ffn_w, layer=0, tm=tm, seq=dec_seq, final=False)
    m = _gmlp(x, mods3, lat_row(1), g_mix[1], w_gin, g_gmlp_v[0:1], w_sp, b_s_full, tm=tm)
    y_sample = _mix_ffn(x, m, mods3, lat_row(1), g_ffn[1], g_fin, w_gout, *ffn_w,
                        layer=1, tm=tm, seq=dec_seq, final=True).reshape(dec_batch, dec_seq, d)

    return (y_prompt, y_sample, state_k, state_v)
```

```python
import functools
import itertools
import math

import numpy as np
import jax
import jax.numpy as jnp
from jax import lax
from jax.experimental import pallas as pl
from jax.experimental.pallas import tpu as pltpu

D_MODEL = 1024
N_HEADS = 16
HEAD_DIM = D_MODEL // N_HEADS
GRID_W = 64
NA_ROWS = 8
NA_COLS = 16
CHUNK = 128
GMLP_WIDTH = 2 * D_MODEL
GMLP_GROUPS = 16
FFN_DIM = 2816
CONV_W = 3
EPS = 1e-6
ATTN_SCALE = HEAD_DIM ** -0.5

LANES = 128
HEADS_PER_BLOCK = LANES // HEAD_DIM
MOD_ROWS = 8
HALO = 16
PAD = 8
FFN_TILE = 256
CAST_STEPS = 8
NA_QROWS = 4
LOOKAHEAD = 1
NEG_BIAS = -1e30
VMEM_LIMIT = 56 * 1024 * 1024

BF16 = jnp.bfloat16
F32 = jnp.float32


def _params(n_axes=1):
    return pltpu.CompilerParams(
        dimension_semantics=("arbitrary",) * n_axes, vmem_limit_bytes=VMEM_LIMIT)


def _resident(shape):
    nd = len(shape)
    return pl.BlockSpec(shape, lambda *_: (0,) * nd, pipeline_mode=pl.Buffered(1))


def _resident_layer(shape, layer):
    nd = len(shape)
    return pl.BlockSpec((1,) + tuple(shape[1:]), lambda *_: (layer,) + (0,) * (nd - 1),
                        pipeline_mode=pl.Buffered(1))


def _pipelined(items, first_stage):
    ready = [first_stage(it) for it in items[:LOOKAHEAD]]
    for i, item in enumerate(items):
        if i + LOOKAHEAD < len(items):
            ready.append(first_stage(items[i + LOOKAHEAD]))
        yield item, ready.pop(0)


def _cast_then_run(n_cast, weights, run):
    step = pl.program_id(0)

    @pl.when(step < n_cast)
    def _():
        for src, dst in weights:
            rows = src.shape[-2]
            dst[pl.ds(pl.multiple_of(step * rows, rows), rows), :] = (
                src[...].reshape(rows, src.shape[-1]).astype(BF16))

    @pl.when(step >= n_cast)
    def _():
        run(step - n_cast)


def _cast_maps(n_cast):
    return (lambda i: jnp.maximum(i - n_cast, 0)), (lambda i: jnp.minimum(i, n_cast - 1))


def _rms_mod(x, g, shift, scale):
    y = x * lax.rsqrt(jnp.mean(x * x, axis=-1, keepdims=True) + EPS)
    return (y * g) * (1 + scale) + shift


def _mod_chunk(mod, k):
    return mod[:, k * D_MODEL:(k + 1) * D_MODEL]


CAST_BLOCK_BYTES = 8 * 1024 * 1024


def _cast_kernel(w_ref, o_ref):
    o_ref[...] = w_ref[...].astype(o_ref.dtype)


def _to_bf16(w):
    shape = w.shape
    cols = shape[-1]
    rows = w.size // cols
    per = 16
    assert rows % per == 0
    blk = max(b for b in range(per, rows + 1, per)
              if rows % b == 0 and (b * cols * 4 <= CAST_BLOCK_BYTES or b == per))
    out = pl.pallas_call(
        _cast_kernel,
        grid=(rows // blk,),
        in_specs=[pl.BlockSpec((blk, cols), lambda i: (i, 0))],
        out_specs=pl.BlockSpec((blk, cols), lambda i: (i, 0)),
        out_shape=jax.ShapeDtypeStruct((rows, cols), BF16),
        compiler_params=_params(),
        name="to_bf16",
    )(w.reshape(rows, cols))
    return out.reshape(shape)


def _ada_kernel(cond_ref, w_ref, b_ref, o_ref):
    s = jax.nn.silu(cond_ref[...]).astype(BF16)
    o_ref[0] = jnp.dot(s, w_ref[0].astype(BF16), preferred_element_type=F32) + b_ref[0]


def _ada(cond, w_ada, b_ada):
    depth, d, n = w_ada.shape
    tn = 1536
    return pl.pallas_call(
        _ada_kernel,
        grid=(depth, n // tn),
        in_specs=[
            pl.BlockSpec((MOD_ROWS, d), lambda l, j: (0, 0)),
            pl.BlockSpec((1, d, tn), lambda l, j: (l, 0, j)),
            pl.BlockSpec((1, 1, tn), lambda l, j: (l, 0, j)),
        ],
        out_specs=pl.BlockSpec((1, MOD_ROWS, tn), lambda l, j: (l, 0, j)),
        out_shape=jax.ShapeDtypeStruct((depth, MOD_ROWS, n), F32),
        compiler_params=_params(2),
        name="ada",
    )(cond, w_ada, b_ada.reshape(depth, 1, n))


def _den_lane(e):
    return (1 - e) * HEAD_DIM


def _head_lanes(e, axis=1):
    ch = lax.broadcasted_iota(jnp.int32, (1, LANES) if axis == 1 else (LANES, 1), axis)
    own = (ch < HEAD_DIM) if e == 0 else (ch >= HEAD_DIM)
    return own, ch == _den_lane(e)


def _head_q(q2, e):
    own, _ = _head_lanes(e)
    return jnp.where(own, q2, jnp.zeros_like(q2))


def _head_v(v2, e, axis=1):
    own, is_den = _head_lanes(e, axis)
    return jnp.where(own, v2, jnp.broadcast_to(is_den.astype(v2.dtype), v2.shape))


def _normalise(accs):
    outs = [acc / acc[:, _den_lane(e):_den_lane(e) + 1] for e, acc in enumerate(accs)]
    own0, _ = _head_lanes(0)
    return jnp.where(own0, outs[0], outs[1])


def _ctx_attn_kernel(x_ref, mod_ref, g_ref, w_ref, o_ref, sk_ref, sv_ref, qkv_scr, *, nb, seq):
    mod = mod_ref[0]
    h = _rms_mod(x_ref[...], g_ref[...], _mod_chunk(mod, 0), _mod_chunk(mod, 1)).astype(BF16)
    qkv_scr[...] = jnp.dot(h, w_ref[...], preferred_element_type=F32)
    for b in range(nb):
        rows = slice(b * seq, (b + 1) * seq)
        for hp in range(N_HEADS // HEADS_PER_BLOCK):
            heads = slice(hp * HEADS_PER_BLOCK, (hp + 1) * HEADS_PER_BLOCK)
            for part, s_ref in ((1, sk_ref), (2, sv_ref)):
                cols = slice(part * D_MODEL + hp * LANES, part * D_MODEL + (hp + 1) * LANES)
                s_ref[b, 0, heads] = qkv_scr[rows, cols].T.reshape(HEADS_PER_BLOCK, HEAD_DIM, seq)

    def block(unit, part):
        b, hp = unit
        return (slice(b * seq, (b + 1) * seq),
                slice(part * D_MODEL + hp * LANES, part * D_MODEL + (hp + 1) * LANES))

    def scores(unit):
        q2 = (qkv_scr[block(unit, 0)] * ATTN_SCALE).astype(BF16)
        k2 = qkv_scr[block(unit, 1)].astype(BF16)
        return [lax.dot_general(_head_q(q2, e), k2, (((1,), (1,)), ((), ())),
                                preferred_element_type=F32) for e in range(HEADS_PER_BLOCK)]

    units = [(b, hp) for b in range(nb) for hp in range(N_HEADS // HEADS_PER_BLOCK)]
    for unit, per_head in _pipelined(units, scores):
        v2 = qkv_scr[block(unit, 2)].astype(BF16)
        o2 = None
        for e, s in enumerate(per_head):
            own, _ = _head_lanes(e)
            ve = jnp.where(own, v2, jnp.zeros_like(v2))
            p = jnp.exp(s - jnp.max(s, axis=-1, keepdims=True))
            oe = (jnp.dot(p.astype(BF16), ve, preferred_element_type=F32)
                  / jnp.sum(p, axis=-1, keepdims=True))
            o2 = oe if o2 is None else o2 + oe
        o_ref[block(unit, 0)] = o2.astype(BF16)


def _ctx_attn(x, mods3, mod_row, g, w_qkv, *, batch, seq):
    n, d = x.shape
    nb = 2
    tm = nb * seq
    kern = functools.partial(_ctx_attn_kernel, nb=nb, seq=seq)
    state = jax.ShapeDtypeStruct((batch, 1, N_HEADS, HEAD_DIM, seq), F32)
    state_spec = pl.BlockSpec((nb, 1, N_HEADS, HEAD_DIM, seq), lambda i: (i, 0, 0, 0, 0))
    return pl.pallas_call(
        kern,
        grid=(n // tm,),
        in_specs=[
            pl.BlockSpec((tm, d), lambda i: (i, 0)),
            pl.BlockSpec((1, 1, 6 * d), lambda i: (mod_row, 0, 0)),
            _resident((1, d)),
            _resident((d, 3 * d)),
        ],
        out_specs=[pl.BlockSpec((tm, d), lambda i: (i, 0)), state_spec, state_spec],
        out_shape=[jax.ShapeDtypeStruct((n, d), BF16), state, state],
        scratch_shapes=[pltpu.VMEM((tm, 3 * d), F32)],
        compiler_params=_params(),
        name="ctx_attn",
    )(x, mods3, g, w_qkv)


def _qkv_kernel(x_ref, mod_ref, g_ref, w_ref, o_ref, *, tn):
    mod = mod_ref[0]
    h = _rms_mod(x_ref[...], g_ref[...], _mod_chunk(mod, 0), _mod_chunk(mod, 1)).astype(BF16)
    for c in range(w_ref.shape[1] // tn):
        cols = slice(c * tn, (c + 1) * tn)
        o_ref[:, cols] = jnp.dot(h, w_ref[:, cols], preferred_element_type=F32).astype(BF16)


def _qkv(x, mods3, mod_row_fn, g, w_qkv, *, tm):
    n, d = x.shape
    nout = w_qkv.shape[1]
    return pl.pallas_call(
        functools.partial(_qkv_kernel, tn=512),
        grid=(n // tm,),
        in_specs=[
            pl.BlockSpec((tm, d), lambda i: (i, 0)),
            pl.BlockSpec((1, 1, 6 * d), lambda i: (mod_row_fn(i), 0, 0)),
            _resident((1, d)),
            _resident((d, nout)),
        ],
        out_specs=pl.BlockSpec((tm, nout), lambda i: (i, 0)),
        out_shape=jax.ShapeDtypeStruct((n, nout), BF16),
        compiler_params=_params(),
        name="qkv",
    )(x, mods3, g, w_qkv)


def _na_groups(rows):
    kh = min(NA_ROWS, rows)
    row_start = [min(max(r - kh // 2, 0), rows - kh) for r in range(rows)]
    groups, off = [], 0
    for r0 in range(0, rows, NA_QROWS):
        lo = min(row_start[r0:r0 + NA_QROWS])
        hi = max(row_start[r0:r0 + NA_QROWS]) + kh
        if (hi - lo) * GRID_W % LANES:
            if hi < rows:
                hi += 1
            else:
                lo -= 1
        nk = (hi - lo) * GRID_W
        groups.append((r0 * GRID_W, lo * GRID_W, nk, off))
        off += nk
    return tuple(groups), row_start, kh


def _na_bias(rpb, rows):
    groups, row_start, kh = _na_groups(rows)
    nh, nro, nco = rpb.shape
    w = GRID_W
    assert 2 * w == LANES
    plan = []
    for (q0, k0, nk, off) in groups:
        for qi in range(NA_QROWS):
            qr = q0 // w + qi
            rs = row_start[qr]
            tile_of = lambda kr: kr - qr + NA_ROWS - 1 if rs <= kr < rs + kh else nro
            for p in range(nk // LANES):
                kr = k0 // w + 2 * p
                plan.append((qi * w, off + p * LANES, tile_of(kr), tile_of(kr + 1)))
    nbias = sum(g[2] for g in groups)
    lead = w - NA_COLS
    padded = jnp.pad(rpb, ((0, 0), (0, 1), (lead, LANES - nco - lead)))
    return pl.pallas_call(
        functools.partial(_na_bias_kernel, plan=tuple(plan), n_tiles=nro),
        grid=(nh,),
        in_specs=[pl.BlockSpec((1, nro + 1, LANES), lambda h: (h, 0, 0))],
        out_specs=pl.BlockSpec((1, NA_QROWS * w, nbias), lambda h: (h, 0, 0)),
        out_shape=jax.ShapeDtypeStruct((nh, NA_QROWS * w, nbias), F32),
        scratch_shapes=[pltpu.VMEM((nro + 1, w, LANES), F32), pltpu.VMEM((nro + 1, w, LANES), F32)],
        compiler_params=_params(),
        name="na_bias",
    )(padded)


def _na_bias_kernel(g_ref, o_ref, lo_scr, hi_scr, *, plan, n_tiles):
    w = GRID_W
    lane = lax.broadcasted_iota(jnp.int32, (w, LANES), 1)
    qc = lax.broadcasted_iota(jnp.int32, (w, LANES), 0)
    kc = lane % w
    col_start = jnp.clip(qc - NA_COLS // 2, 0, w - NA_COLS)
    col_ok = (kc >= col_start) & (kc < col_start + NA_COLS)
    in_lo = lane < w
    neg = jnp.full((w, LANES), NEG_BIAS, F32)
    for a in range(n_tiles):
        row = jnp.broadcast_to(g_ref[0, a:a + 1, :], (w, LANES))
        lo_scr[a] = jnp.where(col_ok, pltpu.roll(row, w + 1, 1, stride=1, stride_axis=0), neg)
        hi_scr[a] = jnp.where(col_ok, pltpu.roll(row, 1, 1, stride=1, stride_axis=0), neg)
    lo_scr[n_tiles] = neg
    hi_scr[n_tiles] = neg
    for (r0, c0, a_lo, a_hi) in plan:
        o_ref[0, r0:r0 + w, c0:c0 + LANES] = jnp.where(in_lo, lo_scr[a_lo], hi_scr[a_hi])


def _nattn_kernel(q_ref, k_ref, v_ref, ck_ref, cv_ref, bias_ref, o_ref, *, groups):
    q2, k2, v2 = q_ref[0], k_ref[0], v_ref[0]
    past = ck_ref.shape[-1]
    kct = ck_ref[0, 0].reshape(LANES, past).astype(BF16)
    vct = cv_ref[0, 0].reshape(LANES, past).astype(BF16)
    q2 = q2 * ATTN_SCALE
    nq = NA_QROWS * GRID_W
    dn = (((1,), (1,)), ((), ()))
    units = [(grp, e) for grp in groups for e in range(HEADS_PER_BLOCK)]

    def scores(unit):
        (q0, k0, nk, boff), e = unit
        qe = _head_q(q2[q0:q0 + nq], e)
        s_w = (lax.dot_general(qe, k2[k0:k0 + nk], dn, preferred_element_type=F32)
               + bias_ref[e, :, boff:boff + nk])
        return s_w, jnp.dot(qe, kct, preferred_element_type=F32)

    def attend(unit, s_w, s_c):
        (_, k0, nk, _), e = unit
        mx = jnp.maximum(jnp.max(s_w, axis=-1, keepdims=True),
                         jnp.max(s_c, axis=-1, keepdims=True))
        return (jnp.dot(jnp.exp(s_w - mx).astype(BF16), _head_v(v2[k0:k0 + nk], e),
                        preferred_element_type=F32)
                + lax.dot_general(jnp.exp(s_c - mx).astype(BF16), _head_v(vct, e, axis=0), dn,
                                  preferred_element_type=F32))

    accs = []
    for unit, s in _pipelined(units, scores):
        accs.append(attend(unit, *s))
        if len(accs) == HEADS_PER_BLOCK:
            q0 = unit[0][0]
            o_ref[0, q0:q0 + nq, :] = _normalise(accs).astype(BF16)
            accs = []


def _nattn(qkv, cache_kt, cache_vt, bias, *, layer):
    b, t, _ = qkv.shape
    past = cache_kt.shape[4]
    groups, _, _ = _na_groups(t // GRID_W)
    nhp = N_HEADS // HEADS_PER_BLOCK
    nq, nbias = bias.shape[1], bias.shape[2]
    cache_spec = pl.BlockSpec((1, 1, HEADS_PER_BLOCK, HEAD_DIM, past),
                              lambda hp, bi: (bi, layer, hp, 0, 0))
    return pl.pallas_call(
        functools.partial(_nattn_kernel, groups=groups),
        grid=(nhp, b),
        in_specs=[
            pl.BlockSpec((1, t, LANES), lambda hp, bi: (bi, 0, hp)),
            pl.BlockSpec((1, t, LANES), lambda hp, bi: (bi, 0, nhp + hp)),
            pl.BlockSpec((1, t, LANES), lambda hp, bi: (bi, 0, 2 * nhp + hp)),
            cache_spec,
            cache_spec,
            pl.BlockSpec((HEADS_PER_BLOCK, nq, nbias), lambda hp, bi: (hp, 0, 0)),
        ],
        out_specs=pl.BlockSpec((1, t, LANES), lambda hp, bi: (bi, 0, hp)),
        out_shape=jax.ShapeDtypeStruct((b, t, D_MODEL), BF16),
        compiler_params=_params(2),
        name="nattn",
    )(qkv, qkv, qkv, cache_kt, cache_vt, bias)


GELU_C = float(np.float32(np.sqrt(2 / np.pi)))
GELU_K = -2.0 * GELU_C * math.log2(math.e)


def _gelu(x):
    return x / (1.0 + jnp.exp2(x * (x * x * (GELU_K * 0.044715) + GELU_K)))


def _gmlp_kernel(x_ref, mod_ref, g_ref, win_f32, gv_ref, ws_ref, bs_ref, o_ref, win_ref, vv_scr,
                 *, n_cast, tm):
    def run(_):
        _gmlp_block(x_ref, mod_ref, g_ref, win_ref, gv_ref, ws_ref, bs_ref, o_ref, vv_scr, tm=tm)

    _cast_then_run(n_cast, ((win_f32, win_ref),), run)


def _gmlp_block(x_ref, mod_ref, g_ref, win_ref, gv_ref, ws_ref, bs_ref, o_ref, vv_scr, *, tm):
    e = GMLP_WIDTH
    gd = e // GMLP_GROUPS
    mod = mod_ref[0]
    h = _rms_mod(x_ref[...], g_ref[...], _mod_chunk(mod, 0), _mod_chunk(mod, 1)).astype(BF16)
    pair = 2 * gd

    def u_proj(gp):
        return jnp.dot(h, win_ref[:, gp * pair:(gp + 1) * pair], preferred_element_type=F32)

    zv = _gelu(jnp.dot(h, win_ref[:, e:2 * e], preferred_element_type=F32))
    pipe = _pipelined(list(range(e // pair)), u_proj)
    head = next(pipe)
    xc = zv - jnp.mean(zv, axis=-1, keepdims=True)
    vv = xc * lax.rsqrt(jnp.mean(xc * xc, axis=-1, keepdims=True) + EPS) * gv_ref[...]
    vv_scr[...] = vv.astype(BF16)
    for gp, u_raw in itertools.chain([head], pipe):
        u2 = _gelu(u_raw)
        for gi in range(2):
            g = 2 * gp + gi
            cols = slice(g * gd, (g + 1) * gd)
            ws_g = ws_ref[g].astype(BF16)
            for c in range(tm // CHUNK):
                rows = slice(c * CHUNK, (c + 1) * CHUNK)
                mixed = jnp.dot(ws_g, vv_scr[rows, cols],
                                preferred_element_type=F32) + bs_ref[g]
                o_ref[rows, cols] = (u2[rows, gi * gd:(gi + 1) * gd] * mixed).astype(BF16)


def _gmlp(x, mods3, mod_row_fn, g, w_in, g_v, w_s, b_s_full, *, tm):
    n, d = x.shape
    e = GMLP_WIDTH
    nc = CAST_STEPS
    assert d % (nc * HALO) == 0
    blk, chunk = _cast_maps(nc)
    return pl.pallas_call(
        functools.partial(_gmlp_kernel, n_cast=nc, tm=tm),
        grid=(nc + n // tm,),
        in_specs=[
            pl.BlockSpec((tm, d), lambda i: (blk(i), 0)),
            pl.BlockSpec((1, 1, 6 * d), lambda i: (mod_row_fn(blk(i)), 0, 0)),
            _resident((1, d)),
            pl.BlockSpec((d // nc, 2 * e), lambda i: (chunk(i), 0)),
            _resident((1, e)),
            _resident(w_s.shape),
            _resident(b_s_full.shape),
        ],
        out_specs=pl.BlockSpec((tm, e), lambda i: (blk(i), 0)),
        out_shape=jax.ShapeDtypeStruct((n, e), BF16),
        scratch_shapes=[pltpu.VMEM((d, 2 * e), BF16), pltpu.VMEM((tm, e), BF16)],
        compiler_params=_params(),
        name="gmlp",
    )(x, mods3, g, w_in, g_v, w_s, b_s_full)


def _conv_pieces(tm, seq, halo):
    if halo:
        return ((0, tm + 2 * HALO, HALO, tm),)
    return tuple((s * seq, seq, 0, seq) for s in range(tm // seq))


def _mix_ffn_kernel(x_ref, xp_ref, xn_ref, m_ref, mp_ref, mn_ref, mod_ref, gffn_ref, gfin_ref,
                    wmo_f32, wup_f32, wconv_ref, bconv_ref, wdown_f32, o_ref,
                    wmo_ref, wup_ref, wdown_ref, act_scr, *, n_cast, **block_args):
    def run(block):
        _mix_ffn_block(x_ref, xp_ref, xn_ref, m_ref, mp_ref, mn_ref, mod_ref, gffn_ref, gfin_ref,
                       wmo_ref, wup_ref, wconv_ref, bconv_ref, wdown_ref, o_ref, act_scr,
                       block=block, **block_args)

    _cast_then_run(n_cast, ((wmo_f32, wmo_ref), (wup_f32, wup_ref), (wdown_f32, wdown_ref)), run)


def _mix_ffn_block(x_ref, xp_ref, xn_ref, m_ref, mp_ref, mn_ref, mod_ref, gffn_ref, gfin_ref,
                   wmo_ref, wup_ref, wconv_ref, bconv_ref, wdown_ref, o_ref, act_scr,
                   *, block, tm, seq, halo, final):
    mod = mod_ref[0]
    gate_mix, shift, scale, gate_ffn = (_mod_chunk(mod, k) for k in (2, 3, 4, 5))
    if halo:
        xcat = jnp.concatenate([xp_ref[0], x_ref[...], xn_ref[0]], axis=0)
        mcat = jnp.concatenate([mp_ref[0], m_ref[...], mn_ref[0]], axis=0)
        main = slice(HALO, HALO + tm)
    else:
        xcat, mcat = x_ref[...], m_ref[...]
        main = slice(0, tm)
    r = xcat.shape[0]
    x1 = xcat + gate_mix * jnp.dot(mcat, wmo_ref[...], preferred_element_type=F32)
    hcat = _rms_mod(x1, gffn_ref[...], shift, scale).astype(BF16)

    f = FFN_TILE
    if halo:
        blk = block % (seq // tm)
        hcat = jnp.concatenate([
            jnp.where(blk == 0, jnp.zeros((HALO, D_MODEL), BF16), hcat[:HALO]),
            hcat[main],
            jnp.where(blk == seq // tm - 1, jnp.zeros((HALO, D_MODEL), BF16), hcat[HALO + tm:]),
        ], axis=0)
    pieces = _conv_pieces(tm, seq, halo)
    sub = lax.broadcasted_iota(jnp.int32, (PAD, 1), 0)

    def up(c):
        return [jnp.dot(hcat, wup_ref[:, col:col + f], preferred_element_type=F32)
                for col in (c * f, FFN_DIM + c * f)]

    def conv(a, col):
        cols = slice(col, col + f)
        wc = wconv_ref[0, :, cols]
        out = []
        for (a0, n_rows, own0, own) in pieces:
            piece = a[a0:a0 + n_rows]
            own_rows = slice(own0, own0 + own)
            prev = pltpu.roll(piece, 1, 0)[own_rows]
            nxt = pltpu.roll(piece, n_rows - 1, 0)[own_rows]
            if not halo:
                prev = jnp.concatenate([jnp.where(sub == 0, 0.0, prev[:PAD]), prev[PAD:]], axis=0)
                nxt = jnp.concatenate([nxt[:-PAD], jnp.where(sub == PAD - 1, 0.0, nxt[-PAD:])], axis=0)
            out.append(prev * wc[0:1] + piece[own_rows] * wc[1:2] + nxt * wc[2:3]
                       + bconv_ref[0, :, cols])
        return out[0] if len(out) == 1 else jnp.concatenate(out, axis=0)

    for c, (a_gate, a_val) in _pipelined(list(range(FFN_DIM // f)), up):
        act = jax.nn.silu(conv(a_gate, c * f)) * conv(a_val, FFN_DIM + c * f)
        act_scr[:, c * f:(c + 1) * f] = act.astype(BF16)
    acc = jnp.dot(act_scr[...], wdown_ref[...], preferred_element_type=F32)
    out = x1[main] + gate_ffn * acc
    if final:
        out = out * lax.rsqrt(jnp.mean(out * out, axis=-1, keepdims=True) + EPS) * gfin_ref[...]
    o_ref[...] = out


def _mix_ffn(x, m, mods3, mod_row_fn, g_ffn, g_fin, w_mo, w_up, w_conv, b_conv, w_down,
             *, layer, tm, seq, final):
    n, d = x.shape
    km = m.shape[1]
    fdim = w_down.shape[1]
    halo = tm % seq != 0
    assert not halo or seq % tm == 0
    nh = n // HALO
    per = tm // HALO
    nc = CAST_STEPS
    assert all(rows % (nc * HALO) == 0 for rows in (km, d, fdim))
    blk, chunk = _cast_maps(nc)
    prev_map = lambda i: (jnp.maximum(blk(i) * per - 1, 0), 0, 0)
    next_map = lambda i: (jnp.minimum((blk(i) + 1) * per, nh - 1), 0, 0)
    kern = functools.partial(_mix_ffn_kernel, n_cast=nc, tm=tm, seq=seq, halo=halo, final=final)
    return pl.pallas_call(
        kern,
        grid=(nc + n // tm,),
        in_specs=[
            pl.BlockSpec((tm, d), lambda i: (blk(i), 0)),
            pl.BlockSpec((1, HALO, d), prev_map),
            pl.BlockSpec((1, HALO, d), next_map),
            pl.BlockSpec((tm, km), lambda i: (blk(i), 0)),
            pl.BlockSpec((1, HALO, km), prev_map),
            pl.BlockSpec((1, HALO, km), next_map),
            pl.BlockSpec((1, 1, 6 * d), lambda i: (mod_row_fn(blk(i)), 0, 0)),
            _resident((1, d)),
            _resident((1, d)),
            pl.BlockSpec((km // nc, d), lambda i: (chunk(i), 0)),
            pl.BlockSpec((1, d // nc, 2 * fdim), lambda i: (layer, chunk(i), 0)),
            _resident_layer(w_conv.shape, layer),
            _resident_layer(b_conv.shape, layer),
            pl.BlockSpec((1, fdim // nc, d), lambda i: (layer, chunk(i), 0)),
        ],
        out_specs=pl.BlockSpec((tm, d), lambda i: (blk(i), 0)),
        out_shape=jax.ShapeDtypeStruct((n, d), F32),
        scratch_shapes=[pltpu.VMEM((km, d), BF16), pltpu.VMEM((d, 2 * fdim), BF16),
                        pltpu.VMEM((fdim, d), BF16), pltpu.VMEM((tm, FFN_DIM), BF16)],
        compiler_params=_params(),
        name="mix_ffn",
    )(x, x.reshape(nh, HALO, d), x.reshape(nh, HALO, d),
      m, m.reshape(nh, HALO, km), m.reshape(nh, HALO, km),
      mods3, g_ffn, g_fin, w_mo, w_up, w_conv, b_conv, w_down)


def kernel(x_prompt, x_sample, cache_k, cache_v, c, c_ctx, w_ada, b_ada, norm_mix_g,
           norm_ffn_g, norm_final_g, w_qkv, w_attn_out, rpb, w_gmlp_in, g_gmlp_v,
           w_spatial, b_spatial, w_gmlp_out, w_ffn_up, w_ffn_conv, b_ffn_conv, w_ffn_down):
    batch, seq, d = x_prompt.shape
    dec_batch, dec_seq, _ = x_sample.shape
    depth = w_ada.shape[0]
    assert depth == 2 and d == D_MODEL

    cond = jnp.zeros((MOD_ROWS, d), F32).at[0].set(c_ctx).at[1:1 + dec_batch].set(c)
    mods3 = _ada(cond, w_ada, b_ada).reshape(depth * MOD_ROWS, 1, 6 * d)

    w_qkv_b = _to_bf16(w_qkv[0])
    w_ao = w_attn_out[0]
    w_gin = w_gmlp_in[0]
    w_gout = w_gmlp_out[0]
    w_sp = w_spatial[0]
    b_s_full = jnp.broadcast_to(b_spatial[0][:, :, None], (GMLP_GROUPS, CHUNK, CHUNK))
    ffn_w = (w_ffn_up, w_ffn_conv, b_ffn_conv[:, None, :], w_ffn_down)
    g_mix = norm_mix_g.reshape(depth, 1, d)
    g_ffn = norm_ffn_g.reshape(depth, 1, d)
    g_fin = norm_final_g.reshape(1, d)
    tm = 512

    def ctx_row(layer):
        return lambda i: layer * MOD_ROWS
    x = x_prompt.reshape(batch * seq, d)
    o, state_kt, state_vt = _ctx_attn(x, mods3, 0, g_mix[0], w_qkv_b, batch=batch, seq=seq)
    state_k, state_v = jnp.swapaxes(state_kt, 3, 4), jnp.swapaxes(state_vt, 3, 4)
    x = _mix_ffn(x, o, mods3, ctx_row(0), g_ffn[0], g_fin, w_ao, *ffn_w,
                 layer=0, tm=tm, seq=seq, final=False)
    m = _gmlp(x, mods3, ctx_row(1), g_mix[1], w_gin, g_gmlp_v[0:1], w_sp, b_s_full, tm=tm)
    y_prompt = _mix_ffn(x, m, mods3, ctx_row(1), g_ffn[1], g_fin, w_gout, *ffn_w,
                        layer=1, tm=tm, seq=seq, final=True).reshape(batch, seq, d)

    def lat_row(layer):
        return lambda i: layer * MOD_ROWS + 1 + (i * tm) // dec_seq
    x = x_sample.reshape(dec_batch * dec_seq, d)
    qkv = _qkv(x, mods3, lat_row(0), g_mix[0], w_qkv_b, tm=tm)
    bias = _na_bias(rpb[0], dec_seq // GRID_W)
    o = _nattn(qkv.reshape(dec_batch, dec_seq, 3 * d), jnp.swapaxes(cache_k, 3, 4),
               jnp.swapaxes(cache_v, 3, 4), bias, layer=0)
    x = _mix_ffn(x, o.reshape(dec_batch * dec_seq, d), mods3, lat_row(0), g_ffn[0], g_fin,
                 w_ao, *ffn_w, layer=0, tm=tm, seq=dec_seq, final=False)
    m = _gmlp(x, mods3, lat_row(1), g_mix[1], w_gin, g_gmlp_v[0:1], w_sp, b_s_full, tm=tm)
    y_sample = _mix_ffn(x, m, mods3, lat_row(1), g_ffn[1], g_fin, w_gout, *ffn_w,
                        layer=1, tm=tm, seq=dec_seq, final=True).reshape(dec_batch, dec_seq, d)

    return (y_prompt, y_sample, state_k, state_v)
```

```python
import functools
import itertools
import math

import numpy as np
import jax
import jax.numpy as jnp
from jax import lax
from jax.experimental import pallas as pl
from jax.experimental.pallas import tpu as pltpu

D_MODEL = 1024
N_HEADS = 16
HEAD_DIM = D_MODEL // N_HEADS
GRID_W = 64
NA_ROWS = 8
NA_COLS = 16
CHUNK = 128
GMLP_WIDTH = 2 * D_MODEL
GMLP_GROUPS = 16
FFN_DIM = 2816
CONV_W = 3
EPS = 1e-6
ATTN_SCALE = HEAD_DIM ** -0.5

LANES = 128
HEADS_PER_BLOCK = LANES // HEAD_DIM
MOD_ROWS = 8
HALO = 16
PAD = 8
FFN_TILE = 256
CAST_STEPS = 8
NA_QROWS = 4
LOOKAHEAD = 1
NEG_BIAS = -1e30
VMEM_LIMIT = 56 * 1024 * 1024

BF16 = jnp.bfloat16
F32 = jnp.float32


def _params(n_axes=1):
    return pltpu.CompilerParams(
        dimension_semantics=("arbitrary",) * n_axes, vmem_limit_bytes=VMEM_LIMIT)


def _resident(shape):
    nd = len(shape)
    return pl.BlockSpec(shape, lambda *_: (0,) * nd, pipeline_mode=pl.Buffered(1))


def _resident_layer(shape, layer):
    nd = len(shape)
    return pl.BlockSpec((1,) + tuple(shape[1:]), lambda *_: (layer,) + (0,) * (nd - 1),
                        pipeline_mode=pl.Buffered(1))


def _pipelined(items, first_stage, lookahead=LOOKAHEAD):
    ready = [first_stage(it) for it in items[:lookahead]]
    for i, item in enumerate(items):
        if i + lookahead < len(items):
            ready.append(first_stage(items[i + lookahead]))
        yield item, ready.pop(0)


def _cast_then_run(n_cast, weights, run):
    step = pl.program_id(0)

    @pl.when(step < n_cast)
    def _():
        for src, dst in weights:
            rows = src.shape[-2]
            dst[pl.ds(pl.multiple_of(step * rows, rows), rows), :] = (
                src[...].reshape(rows, src.shape[-1]).astype(BF16))

    @pl.when(step >= n_cast)
    def _():
        run(step - n_cast)


def _cast_maps(n_cast):
    return (lambda i: jnp.maximum(i - n_cast, 0)), (lambda i: jnp.minimum(i, n_cast - 1))


def _rms_mod(x, g, shift, scale):
    y = x * lax.rsqrt(jnp.mean(x * x, axis=-1, keepdims=True) + EPS)
    return (y * g) * (1 + scale) + shift


def _mod_chunk(mod, k):
    return mod[:, k * D_MODEL:(k + 1) * D_MODEL]


CAST_BLOCK_BYTES = 8 * 1024 * 1024


def _cast_kernel(w_ref, o_ref):
    o_ref[...] = w_ref[...].astype(o_ref.dtype)


def _to_bf16(w):
    shape = w.shape
    cols = shape[-1]
    rows = w.size // cols
    per = 16
    assert rows % per == 0
    blk = max(b for b in range(per, rows + 1, per)
              if rows % b == 0 and (b * cols * 4 <= CAST_BLOCK_BYTES or b == per))
    out = pl.pallas_call(
        _cast_kernel,
        grid=(rows // blk,),
        in_specs=[pl.BlockSpec((blk, cols), lambda i: (i, 0))],
        out_specs=pl.BlockSpec((blk, cols), lambda i: (i, 0)),
        out_shape=jax.ShapeDtypeStruct((rows, cols), BF16),
        compiler_params=_params(),
        name="to_bf16",
    )(w.reshape(rows, cols))
    return out.reshape(shape)


def _ada_kernel(cond_ref, w_ref, b_ref, o_ref):
    s = jax.nn.silu(cond_ref[...]).astype(BF16)
    o_ref[0] = jnp.dot(s, w_ref[0].astype(BF16), preferred_element_type=F32) + b_ref[0]


def _ada(cond, w_ada, b_ada):
    depth, d, n = w_ada.shape
    tn = 1536
    return pl.pallas_call(
        _ada_kernel,
        grid=(depth, n // tn),
        in_specs=[
            pl.BlockSpec((MOD_ROWS, d), lambda l, j: (0, 0)),
            pl.BlockSpec((1, d, tn), lambda l, j: (l, 0, j)),
            pl.BlockSpec((1, 1, tn), lambda l, j: (l, 0, j)),
        ],
        out_specs=pl.BlockSpec((1, MOD_ROWS, tn), lambda l, j: (l, 0, j)),
        out_shape=jax.ShapeDtypeStruct((depth, MOD_ROWS, n), F32),
        compiler_params=_params(2),
        name="ada",
    )(cond, w_ada, b_ada.reshape(depth, 1, n))


def _den_lane(e):
    return (1 - e) * HEAD_DIM


def _head_lanes(e, axis=1):
    ch = lax.broadcasted_iota(jnp.int32, (1, LANES) if axis == 1 else (LANES, 1), axis)
    own = (ch < HEAD_DIM) if e == 0 else (ch >= HEAD_DIM)
    return own, ch == _den_lane(e)


def _head_q(q2, e):
    own, _ = _head_lanes(e)
    return jnp.where(own, q2, jnp.zeros_like(q2))


def _head_v(v2, e, axis=1):
    own, is_den = _head_lanes(e, axis)
    return jnp.where(own, v2, jnp.broadcast_to(is_den.astype(v2.dtype), v2.shape))


def _normalise(accs):
    outs = [acc / acc[:, _den_lane(e):_den_lane(e) + 1] for e, acc in enumerate(accs)]
    own0, _ = _head_lanes(0)
    return jnp.where(own0, outs[0], outs[1])


def _ctx_attn_kernel(x_ref, mod_ref, g_ref, w_ref, o_ref, sk_ref, sv_ref, qkv_scr, *, nb, seq):
    mod = mod_ref[0]
    h = _rms_mod(x_ref[...], g_ref[...], _mod_chunk(mod, 0), _mod_chunk(mod, 1)).astype(BF16)
    qkv_scr[...] = jnp.dot(h, w_ref[...], preferred_element_type=F32)
    for b in range(nb):
        rows = slice(b * seq, (b + 1) * seq)
        for hp in range(N_HEADS // HEADS_PER_BLOCK):
            heads = slice(hp * HEADS_PER_BLOCK, (hp + 1) * HEADS_PER_BLOCK)
            for part, s_ref in ((1, sk_ref), (2, sv_ref)):
                cols = slice(part * D_MODEL + hp * LANES, part * D_MODEL + (hp + 1) * LANES)
                s_ref[b, 0, heads] = qkv_scr[rows, cols].T.reshape(HEADS_PER_BLOCK, HEAD_DIM, seq)

    def block(unit, part):
        b, hp = unit
        return (slice(b * seq, (b + 1) * seq),
                slice(part * D_MODEL + hp * LANES, part * D_MODEL + (hp + 1) * LANES))

    def scores(unit):
        q2 = (qkv_scr[block(unit, 0)] * ATTN_SCALE).astype(BF16)
        k2 = qkv_scr[block(unit, 1)].astype(BF16)
        return [lax.dot_general(_head_q(q2, e), k2, (((1,), (1,)), ((), ())),
                                preferred_element_type=F32) for e in range(HEADS_PER_BLOCK)]

    units = [(b, hp) for b in range(nb) for hp in range(N_HEADS // HEADS_PER_BLOCK)]
    for unit, per_head in _pipelined(units, scores):
        v2 = qkv_scr[block(unit, 2)].astype(BF16)
        o2 = None
        for e, s in enumerate(per_head):
            own, _ = _head_lanes(e)
            ve = jnp.where(own, v2, jnp.zeros_like(v2))
            p = jnp.exp(s - jnp.max(s, axis=-1, keepdims=True))
            oe = (jnp.dot(p.astype(BF16), ve, preferred_element_type=F32)
                  / jnp.sum(p, axis=-1, keepdims=True))
            o2 = oe if o2 is None else o2 + oe
        o_ref[block(unit, 0)] = o2.astype(BF16)


def _ctx_attn(x, mods3, mod_row, g, w_qkv, *, batch, seq):
    n, d = x.shape
    nb = 2
    tm = nb * seq
    kern = functools.partial(_ctx_attn_kernel, nb=nb, seq=seq)
    state = jax.ShapeDtypeStruct((batch, 1, N_HEADS, HEAD_DIM, seq), F32)
    state_spec = pl.BlockSpec((nb, 1, N_HEADS, HEAD_DIM, seq), lambda i: (i, 0, 0, 0, 0))
    return pl.pallas_call(
        kern,
        grid=(n // tm,),
        in_specs=[
            pl.BlockSpec((tm, d), lambda i: (i, 0)),
            pl.BlockSpec((1, 1, 6 * d), lambda i: (mod_row, 0, 0)),
            _resident((1, d)),
            _resident((d, 3 * d)),
        ],
        out_specs=[pl.BlockSpec((tm, d), lambda i: (i, 0)), state_spec, state_spec],
        out_shape=[jax.ShapeDtypeStruct((n, d), BF16), state, state],
        scratch_shapes=[pltpu.VMEM((tm, 3 * d), F32)],
        compiler_params=_params(),
        name="ctx_attn",
    )(x, mods3, g, w_qkv)


def _qkv_kernel(x_ref, mod_ref, g_ref, w_ref, o_ref, *, tn):
    mod = mod_ref[0]
    h = _rms_mod(x_ref[...], g_ref[...], _mod_chunk(mod, 0), _mod_chunk(mod, 1)).astype(BF16)
    for c in range(w_ref.shape[1] // tn):
        cols = slice(c * tn, (c + 1) * tn)
        o_ref[:, cols] = jnp.dot(h, w_ref[:, cols], preferred_element_type=F32).astype(BF16)


def _qkv(x, mods3, mod_row_fn, g, w_qkv, *, tm):
    n, d = x.shape
    nout = w_qkv.shape[1]
    return pl.pallas_call(
        functools.partial(_qkv_kernel, tn=512),
        grid=(n // tm,),
        in_specs=[
            pl.BlockSpec((tm, d), lambda i: (i, 0)),
            pl.BlockSpec((1, 1, 6 * d), lambda i: (mod_row_fn(i), 0, 0)),
            _resident((1, d)),
            _resident((d, nout)),
        ],
        out_specs=pl.BlockSpec((tm, nout), lambda i: (i, 0)),
        out_shape=jax.ShapeDtypeStruct((n, nout), BF16),
        compiler_params=_params(),
        name="qkv",
    )(x, mods3, g, w_qkv)


def _na_groups(rows):
    kh = min(NA_ROWS, rows)
    row_start = [min(max(r - kh // 2, 0), rows - kh) for r in range(rows)]
    groups, off = [], 0
    for r0 in range(0, rows, NA_QROWS):
        lo = min(row_start[r0:r0 + NA_QROWS])
        hi = max(row_start[r0:r0 + NA_QROWS]) + kh
        if (hi - lo) * GRID_W % LANES:
            if hi < rows:
                hi += 1
            else:
                lo -= 1
        nk = (hi - lo) * GRID_W
        groups.append((r0 * GRID_W, lo * GRID_W, nk, off))
        off += nk
    return tuple(groups), row_start, kh


def _na_bias_plan(rows):
    groups, row_start, kh = _na_groups(rows)
    nro = 2 * NA_ROWS - 1
    w = GRID_W
    assert 2 * w == LANES
    plan = []
    for (q0, k0, nk, off) in groups:
        for qi in range(NA_QROWS):
            qr = q0 // w + qi
            rs = row_start[qr]
            tile_of = lambda kr: kr - qr + NA_ROWS - 1 if rs <= kr < rs + kh else nro
            for p in range(nk // LANES):
                kr = k0 // w + 2 * p
                plan.append((qi * w, off + p * LANES, tile_of(kr), tile_of(kr + 1)))
    return tuple(plan), sum(g[2] for g in groups)


def _pad_rpb(rpb):
    lead = GRID_W - NA_COLS
    return jnp.pad(rpb, ((0, 0), (0, 1), (lead, LANES - rpb.shape[2] - lead)))


def _fill_na_bias(g_ref, o_ref, lo_scr, hi_scr, plan):
    w = GRID_W
    n_tiles = 2 * NA_ROWS - 1
    lane = lax.broadcasted_iota(jnp.int32, (w, LANES), 1)
    qc = lax.broadcasted_iota(jnp.int32, (w, LANES), 0)
    kc = lane % w
    col_start = jnp.clip(qc - NA_COLS // 2, 0, w - NA_COLS)
    col_ok = (kc >= col_start) & (kc < col_start + NA_COLS)
    in_lo = lane < w
    neg = jnp.full((w, LANES), NEG_BIAS, F32)
    for a in range(n_tiles):
        row = jnp.broadcast_to(g_ref[a:a + 1, :], (w, LANES))
        lo_scr[a] = jnp.where(col_ok, pltpu.roll(row, w + 1, 1, stride=1, stride_axis=0), neg)
        hi_scr[a] = jnp.where(col_ok, pltpu.roll(row, 1, 1, stride=1, stride_axis=0), neg)
    lo_scr[n_tiles] = neg
    hi_scr[n_tiles] = neg
    for (r0, c0, a_lo, a_hi) in plan:
        o_ref[r0:r0 + w, c0:c0 + LANES] = jnp.where(in_lo, lo_scr[a_lo], hi_scr[a_hi])


def _nattn_kernel(q_ref, k_ref, v_ref, ck_ref, cv_ref, g_ref, o_ref, bias_ref, lo_scr, hi_scr,
                  *, groups, plan):
    @pl.when(pl.program_id(1) == 0)
    def _():
        for e in range(HEADS_PER_BLOCK):
            _fill_na_bias(g_ref.at[e], bias_ref.at[e], lo_scr, hi_scr, plan)

    q2, k2, v2 = q_ref[0], k_ref[0], v_ref[0]
    past = ck_ref.shape[-1]
    kct = ck_ref[0, 0].reshape(LANES, past).astype(BF16)
    vct = cv_ref[0, 0].reshape(LANES, past).astype(BF16)
    q2 = q2 * ATTN_SCALE
    nq = NA_QROWS * GRID_W
    dn = (((1,), (1,)), ((), ()))
    units = [(grp, e) for grp in groups for e in range(HEADS_PER_BLOCK)]

    def scores(unit):
        (q0, k0, nk, boff), e = unit
        qe = _head_q(q2[q0:q0 + nq], e)
        s_w = (lax.dot_general(qe, k2[k0:k0 + nk], dn, preferred_element_type=F32)
               + bias_ref[e, :, boff:boff + nk])
        return s_w, jnp.dot(qe, kct, preferred_element_type=F32)

    def attend(unit, s_w, s_c):
        (_, k0, nk, _), e = unit
        mx = jnp.maximum(jnp.max(s_w, axis=-1, keepdims=True),
                         jnp.max(s_c, axis=-1, keepdims=True))
        return (jnp.dot(jnp.exp(s_w - mx).astype(BF16), _head_v(v2[k0:k0 + nk], e),
                        preferred_element_type=F32)
                + lax.dot_general(jnp.exp(s_c - mx).astype(BF16), _head_v(vct, e, axis=0), dn,
                                  preferred_element_type=F32))

    accs = []
    for unit, s in _pipelined(units, scores):
        accs.append(attend(unit, *s))
        if len(accs) == HEADS_PER_BLOCK:
            q0 = unit[0][0]
            o_ref[0, q0:q0 + nq, :] = _normalise(accs).astype(BF16)
            accs = []


def _nattn(qkv, cache_kt, cache_vt, rpb, *, layer):
    b, t, _ = qkv.shape
    past = cache_kt.shape[4]
    groups, _, _ = _na_groups(t // GRID_W)
    plan, nbias = _na_bias_plan(t // GRID_W)
    nhp = N_HEADS // HEADS_PER_BLOCK
    nq = NA_QROWS * GRID_W
    n_tiles = 2 * NA_ROWS
    cache_spec = pl.BlockSpec((1, 1, HEADS_PER_BLOCK, HEAD_DIM, past),
                              lambda hp, bi: (bi, layer, hp, 0, 0))
    return pl.pallas_call(
        functools.partial(_nattn_kernel, groups=groups, plan=plan),
        grid=(nhp, b),
        in_specs=[
            pl.BlockSpec((1, t, LANES), lambda hp, bi: (bi, 0, hp)),
            pl.BlockSpec((1, t, LANES), lambda hp, bi: (bi, 0, nhp + hp)),
            pl.BlockSpec((1, t, LANES), lambda hp, bi: (bi, 0, 2 * nhp + hp)),
            cache_spec,
            cache_spec,
            pl.BlockSpec((HEADS_PER_BLOCK, n_tiles, LANES), lambda hp, bi: (hp, 0, 0)),
        ],
        out_specs=pl.BlockSpec((1, t, LANES), lambda hp, bi: (bi, 0, hp)),
        out_shape=jax.ShapeDtypeStruct((b, t, D_MODEL), BF16),
        scratch_shapes=[pltpu.VMEM((HEADS_PER_BLOCK, nq, nbias), F32),
                        pltpu.VMEM((n_tiles, GRID_W, LANES), F32),
                        pltpu.VMEM((n_tiles, GRID_W, LANES), F32)],
        compiler_params=_params(2),
        name="nattn",
    )(qkv, qkv, qkv, cache_kt, cache_vt, _pad_rpb(rpb))


GELU_C = float(np.float32(np.sqrt(2 / np.pi)))
GELU_K = -2.0 * GELU_C * math.log2(math.e)


def _gelu(x):
    return x / (1.0 + jnp.exp2(x * (x * x * (GELU_K * 0.044715) + GELU_K)))


def _gmlp_kernel(x_ref, mod_ref, g_ref, win_f32, gv_ref, ws_ref, bs_ref, o_ref, win_ref, vv_scr,
                 *, n_cast, tm):
    def run(_):
        _gmlp_block(x_ref, mod_ref, g_ref, win_ref, gv_ref, ws_ref, bs_ref, o_ref, vv_scr, tm=tm)

    _cast_then_run(n_cast, ((win_f32, win_ref),), run)


def _gmlp_block(x_ref, mod_ref, g_ref, win_ref, gv_ref, ws_ref, bs_ref, o_ref, vv_scr, *, tm):
    e = GMLP_WIDTH
    gd = e // GMLP_GROUPS
    mod = mod_ref[0]
    h = _rms_mod(x_ref[...], g_ref[...], _mod_chunk(mod, 0), _mod_chunk(mod, 1)).astype(BF16)
    pair = 2 * gd

    def u_proj(gp):
        return jnp.dot(h, win_ref[:, gp * pair:(gp + 1) * pair], preferred_element_type=F32)

    zv = _gelu(jnp.dot(h, win_ref[:, e:2 * e], preferred_element_type=F32))
    pipe = _pipelined(list(range(e // pair)), u_proj)
    head = next(pipe)
    xc = zv - jnp.mean(zv, axis=-1, keepdims=True)
    vv = xc * lax.rsqrt(jnp.mean(xc * xc, axis=-1, keepdims=True) + EPS) * gv_ref[...]
    vv_scr[...] = vv.astype(BF16)
    for gp, u_raw in itertools.chain([head], pipe):
        u2 = _gelu(u_raw)
        for gi in range(2):
            g = 2 * gp + gi
            cols = slice(g * gd, (g + 1) * gd)
            nc = tm // CHUNK
            vcat = jnp.concatenate(
                [vv_scr[c * CHUNK:(c + 1) * CHUNK, cols] for c in range(nc)], axis=1)
            mixed = jnp.dot(ws_ref[g].astype(BF16), vcat, preferred_element_type=F32)
            for c in range(nc):
                rows = slice(c * CHUNK, (c + 1) * CHUNK)
                o_ref[rows, cols] = (u2[rows, gi * gd:(gi + 1) * gd]
                                     * (mixed[:, c * gd:(c + 1) * gd] + bs_ref[g])).astype(BF16)


def _gmlp(x, mods3, mod_row_fn, g, w_in, g_v, w_s, b_s_full, *, tm):
    n, d = x.shape
    e = GMLP_WIDTH
    nc = CAST_STEPS
    assert d % (nc * HALO) == 0
    blk, chunk = _cast_maps(nc)
    return pl.pallas_call(
        functools.partial(_gmlp_kernel, n_cast=nc, tm=tm),
        grid=(nc + n // tm,),
        in_specs=[
            pl.BlockSpec((tm, d), lambda i: (blk(i), 0)),
            pl.BlockSpec((1, 1, 6 * d), lambda i: (mod_row_fn(blk(i)), 0, 0)),
            _resident((1, d)),
            pl.BlockSpec((d // nc, 2 * e), lambda i: (chunk(i), 0)),
            _resident((1, e)),
            _resident(w_s.shape),
            _resident(b_s_full.shape),
        ],
        out_specs=pl.BlockSpec((tm, e), lambda i: (blk(i), 0)),
        out_shape=jax.ShapeDtypeStruct((n, e), BF16),
        scratch_shapes=[pltpu.VMEM((d, 2 * e), BF16), pltpu.VMEM((tm, e), BF16)],
        compiler_params=_params(),
        name="gmlp",
    )(x, mods3, g, w_in, g_v, w_s, b_s_full)


def _conv_pieces(tm, seq, halo):
    if halo:
        return ((0, tm + 2 * HALO, HALO, tm),)
    return tuple((s * seq, seq, 0, seq) for s in range(tm // seq))


def _mix_ffn_kernel(x_ref, xp_ref, xn_ref, m_ref, mp_ref, mn_ref, mod_ref, gffn_ref, gfin_ref,
                    wmo_f32, wup_f32, wconv_ref, bconv_ref, wdown_f32, o_ref,
                    wmo_ref, wup_ref, wdown_ref, act_scr, *, n_cast, **block_args):
    def run(block):
        _mix_ffn_block(x_ref, xp_ref, xn_ref, m_ref, mp_ref, mn_ref, mod_ref, gffn_ref, gfin_ref,
                       wmo_ref, wup_ref, wconv_ref, bconv_ref, wdown_ref, o_ref, act_scr,
                       block=block, **block_args)

    _cast_then_run(n_cast, ((wmo_f32, wmo_ref), (wup_f32, wup_ref), (wdown_f32, wdown_ref)), run)


def _mix_ffn_block(x_ref, xp_ref, xn_ref, m_ref, mp_ref, mn_ref, mod_ref, gffn_ref, gfin_ref,
                   wmo_ref, wup_ref, wconv_ref, bconv_ref, wdown_ref, o_ref, act_scr,
                   *, block, tm, seq, halo, final):
    mod = mod_ref[0]
    gate_mix, shift, scale, gate_ffn = (_mod_chunk(mod, k) for k in (2, 3, 4, 5))
    if halo:
        xcat = jnp.concatenate([xp_ref[0], x_ref[...], xn_ref[0]], axis=0)
        mcat = jnp.concatenate([mp_ref[0], m_ref[...], mn_ref[0]], axis=0)
        main = slice(HALO, HALO + tm)
    else:
        xcat, mcat = x_ref[...], m_ref[...]
        main = slice(0, tm)
    r = xcat.shape[0]
    x1 = xcat + gate_mix * jnp.dot(mcat, wmo_ref[...], preferred_element_type=F32)
    hcat = _rms_mod(x1, gffn_ref[...], shift, scale).astype(BF16)

    f = FFN_TILE
    if halo:
        blk = block % (seq // tm)
        hcat = jnp.concatenate([
            jnp.where(blk == 0, jnp.zeros((HALO, D_MODEL), BF16), hcat[:HALO]),
            hcat[main],
            jnp.where(blk == seq // tm - 1, jnp.zeros((HALO, D_MODEL), BF16), hcat[HALO + tm:]),
        ], axis=0)
    pieces = _conv_pieces(tm, seq, halo)
    sub = lax.broadcasted_iota(jnp.int32, (PAD, 1), 0)

    def up(c):
        return [jnp.dot(hcat, wup_ref[:, col:col + f], preferred_element_type=F32)
                for col in (c * f, FFN_DIM + c * f)]

    def conv(a, col):
        cols = slice(col, col + f)
        wc = wconv_ref[0, :, cols]
        out = []
        for (a0, n_rows, own0, own) in pieces:
            piece = a[a0:a0 + n_rows]
            own_rows = slice(own0, own0 + own)
            prev = pltpu.roll(piece, 1, 0)[own_rows]
            nxt = pltpu.roll(piece, n_rows - 1, 0)[own_rows]
            if not halo:
                prev = jnp.concatenate([jnp.where(sub == 0, 0.0, prev[:PAD]), prev[PAD:]], axis=0)
                nxt = jnp.concatenate([nxt[:-PAD], jnp.where(sub == PAD - 1, 0.0, nxt[-PAD:])], axis=0)
            out.append(prev * wc[0:1] + piece[own_rows] * wc[1:2] + nxt * wc[2:3]
                       + bconv_ref[0, :, cols])
        return out[0] if len(out) == 1 else jnp.concatenate(out, axis=0)

    for c, (a_gate, a_val) in _pipelined(list(range(FFN_DIM // f)), up):
        act = jax.nn.silu(conv(a_gate, c * f)) * conv(a_val, FFN_DIM + c * f)
        act_scr[:, c * f:(c + 1) * f] = act.astype(BF16)
    acc = jnp.dot(act_scr[...], wdown_ref[...], preferred_element_type=F32)
    out = x1[main] + gate_ffn * acc
    if final:
        out = out * lax.rsqrt(jnp.mean(out * out, axis=-1, keepdims=True) + EPS) * gfin_ref[...]
    o_ref[...] = out


def _mix_ffn(x, m, mods3, mod_row_fn, g_ffn, g_fin, w_mo, w_up, w_conv, b_conv, w_down,
             *, layer, tm, seq, final):
    n, d = x.shape
    km = m.shape[1]
    fdim = w_down.shape[1]
    halo = tm % seq != 0
    assert not halo or seq % tm == 0
    nh = n // HALO
    per = tm // HALO
    nc = CAST_STEPS
    assert all(rows % (nc * HALO) == 0 for rows in (km, d, fdim))
    blk, chunk = _cast_maps(nc)
    prev_map = lambda i: (jnp.maximum(blk(i) * per - 1, 0), 0, 0)
    next_map = lambda i: (jnp.minimum((blk(i) + 1) * per, nh - 1), 0, 0)
    kern = functools.partial(_mix_ffn_kernel, n_cast=nc, tm=tm, seq=seq, halo=halo, final=final)
    return pl.pallas_call(
        kern,
        grid=(nc + n // tm,),
        in_specs=[
            pl.BlockSpec((tm, d), lambda i: (blk(i), 0)),
            pl.BlockSpec((1, HALO, d), prev_map),
            pl.BlockSpec((1, HALO, d), next_map),
            pl.BlockSpec((tm, km), lambda i: (blk(i), 0)),
            pl.BlockSpec((1, HALO, km), prev_map),
            pl.BlockSpec((1, HALO, km), next_map),
            pl.BlockSpec((1, 1, 6 * d), lambda i: (mod_row_fn(blk(i)), 0, 0)),
            _resident((1, d)),
            _resident((1, d)),
            pl.BlockSpec((km // nc, d), lambda i: (chunk(i), 0)),
            pl.BlockSpec((1, d // nc, 2 * fdim), lambda i: (layer, chunk(i), 0)),
            _resident_layer(w_conv.shape, layer),
            _resident_layer(b_conv.shape, layer),
            pl.BlockSpec((1, fdim // nc, d), lambda i: (layer, chunk(i), 0)),
        ],
        out_specs=pl.BlockSpec((tm, d), lambda i: (blk(i), 0)),
        out_shape=jax.ShapeDtypeStruct((n, d), F32),
        scratch_shapes=[pltpu.VMEM((km, d), BF16), pltpu.VMEM((d, 2 * fdim), BF16),
                        pltpu.VMEM((fdim, d), BF16), pltpu.VMEM((tm, FFN_DIM), BF16)],
        compiler_params=_params(),
        name="mix_ffn",
    )(x, x.reshape(nh, HALO, d), x.reshape(nh, HALO, d),
      m, m.reshape(nh, HALO, km), m.reshape(nh, HALO, km),
      mods3, g_ffn, g_fin, w_mo, w_up, w_conv, b_conv, w_down)


def kernel(x_prompt, x_sample, cache_k, cache_v, c, c_ctx, w_ada, b_ada, norm_mix_g,
           norm_ffn_g, norm_final_g, w_qkv, w_attn_out, rpb, w_gmlp_in, g_gmlp_v,
           w_spatial, b_spatial, w_gmlp_out, w_ffn_up, w_ffn_conv, b_ffn_conv, w_ffn_down):
    batch, seq, d = x_prompt.shape
    dec_batch, dec_seq, _ = x_sample.shape
    depth = w_ada.shape[0]
    assert depth == 2 and d == D_MODEL

    cond = jnp.zeros((MOD_ROWS, d), F32).at[0].set(c_ctx).at[1:1 + dec_batch].set(c)
    mods3 = _ada(cond, w_ada, b_ada).reshape(depth * MOD_ROWS, 1, 6 * d)

    w_qkv_b = _to_bf16(w_qkv[0])
    w_ao = w_attn_out[0]
    w_gin = w_gmlp_in[0]
    w_gout = w_gmlp_out[0]
    w_sp = w_spatial[0]
    b_s_full = jnp.broadcast_to(b_spatial[0][:, :, None], (GMLP_GROUPS, CHUNK, CHUNK))
    ffn_w = (w_ffn_up, w_ffn_conv, b_ffn_conv[:, None, :], w_ffn_down)
    g_mix = norm_mix_g.reshape(depth, 1, d)
    g_ffn = norm_ffn_g.reshape(depth, 1, d)
    g_fin = norm_final_g.reshape(1, d)
    tm = 512

    def ctx_row(layer):
        return lambda i: layer * MOD_ROWS
    x = x_prompt.reshape(batch * seq, d)
    o, state_kt, state_vt = _ctx_attn(x, mods3, 0, g_mix[0], w_qkv_b, batch=batch, seq=seq)
    state_k, state_v = jnp.swapaxes(state_kt, 3, 4), jnp.swapaxes(state_vt, 3, 4)
    x = _mix_ffn(x, o, mods3, ctx_row(0), g_ffn[0], g_fin, w_ao, *ffn_w,
                 layer=0, tm=tm, seq=seq, final=False)
    m = _gmlp(x, mods3, ctx_row(1), g_mix[1], w_gin, g_gmlp_v[0:1], w_sp, b_s_full, tm=tm)
    y_prompt = _mix_ffn(x, m, mods3, ctx_row(1), g_ffn[1], g_fin, w_gout, *ffn_w,
                        layer=1, tm=tm, seq=seq, final=True).reshape(batch, seq, d)

    def lat_row(layer):
        return lambda i: layer * MOD_ROWS + 1 + (i * tm) // dec_seq
    x = x_sample.reshape(dec_batch * dec_seq, d)
    qkv = _qkv(x, mods3, lat_row(0), g_mix[0], w_qkv_b, tm=tm)
    o = _nattn(qkv.reshape(dec_batch, dec_seq, 3 * d), jnp.swapaxes(cache_k, 3, 4),
               jnp.swapaxes(cache_v, 3, 4), rpb[0], layer=0)
    x = _mix_ffn(x, o.reshape(dec_batch * dec_seq, d), mods3, lat_row(0), g_ffn[0], g_fin,
                 w_ao, *ffn_w, layer=0, tm=tm, seq=dec_seq, final=False)
    m = _gmlp(x, mods3, lat_row(1), g_mix[1], w_gin, g_gmlp_v[0:1], w_sp, b_s_full, tm=tm)
    y_sample = _mix_ffn(x, m, mods3, lat_row(1), g_ffn[1], g_fin, w_gout, *ffn_w,
                        layer=1, tm=tm, seq=dec_seq, final=True).reshape(dec_batch, dec_seq, d)

    return (y_prompt, y_sample, state_k, state_v)
```

```python
import functools
import itertools
import math

import numpy as np
import jax
import jax.numpy as jnp
from jax import lax
from jax.experimental import pallas as pl
from jax.experimental.pallas import tpu as pltpu

D_MODEL = 1024
N_HEADS = 16
HEAD_DIM = D_MODEL // N_HEADS
GRID_W = 64
NA_ROWS = 8
NA_COLS = 16
CHUNK = 128
GMLP_WIDTH = 2 * D_MODEL
GMLP_GROUPS = 16
FFN_DIM = 2816
CONV_W = 3
EPS = 1e-6
ATTN_SCALE = HEAD_DIM ** -0.5

LANES = 128
HEADS_PER_BLOCK = LANES // HEAD_DIM
MOD_ROWS = 8
HALO = 16
PAD = 8
FFN_TILE = 256
CAST_STEPS = 8
NA_QROWS = 4
LOOKAHEAD = 1
NEG_BIAS = -1e30
VMEM_LIMIT = 56 * 1024 * 1024

BF16 = jnp.bfloat16
F32 = jnp.float32


def _params(n_axes=1):
    return pltpu.CompilerParams(
        dimension_semantics=("arbitrary",) * n_axes, vmem_limit_bytes=VMEM_LIMIT)


def _resident(shape):
    nd = len(shape)
    return pl.BlockSpec(shape, lambda *_: (0,) * nd, pipeline_mode=pl.Buffered(1))


def _resident_layer(shape, layer):
    nd = len(shape)
    return pl.BlockSpec((1,) + tuple(shape[1:]), lambda *_: (layer,) + (0,) * (nd - 1),
                        pipeline_mode=pl.Buffered(1))


def _pipelined(items, first_stage):
    ready = [first_stage(it) for it in items[:LOOKAHEAD]]
    for i, item in enumerate(items):
        if i + LOOKAHEAD < len(items):
            ready.append(first_stage(items[i + LOOKAHEAD]))
        yield item, ready.pop(0)


def _cast_then_run(n_cast, weights, run):
    step = pl.program_id(0)

    @pl.when(step < n_cast)
    def _():
        for src, dst in weights:
            rows = src.shape[-2]
            dst[pl.ds(pl.multiple_of(step * rows, rows), rows), :] = (
                src[...].reshape(rows, src.shape[-1]).astype(BF16))

    @pl.when(step >= n_cast)
    def _():
        run(step - n_cast)


def _side_cast_specs(weights, n_steps):
    ins, outs, shapes = [], [], []
    for w, layer in weights:
        _, rows, cols = w.shape
        assert rows % (n_steps * HALO) == 0
        ins.append(pl.BlockSpec((1, rows // n_steps, cols), lambda i, layer=layer: (layer, i, 0)))
        outs.append(pl.BlockSpec((rows // n_steps, cols), lambda i: (i, 0)))
        shapes.append(jax.ShapeDtypeStruct((rows, cols), BF16))
    return ins, outs, shapes


def _side_cast(src_refs, dst_refs):
    for src, dst in zip(src_refs, dst_refs):
        dst[...] = src[0].astype(BF16)


def _cast_maps(n_cast):
    return (lambda i: jnp.maximum(i - n_cast, 0)), (lambda i: jnp.minimum(i, n_cast - 1))


def _rms_mod(x, g, shift, scale):
    y = x * lax.rsqrt(jnp.mean(x * x, axis=-1, keepdims=True) + EPS)
    return (y * g) * (1 + scale) + shift


def _mod_chunk(mod, k):
    return mod[:, k * D_MODEL:(k + 1) * D_MODEL]


CAST_BLOCK_BYTES = 8 * 1024 * 1024


def _cast_kernel(w_ref, o_ref):
    o_ref[...] = w_ref[...].astype(o_ref.dtype)


def _to_bf16(w):
    shape = w.shape
    cols = shape[-1]
    rows = w.size // cols
    per = 16
    assert rows % per == 0
    blk = max(b for b in range(per, rows + 1, per)
              if rows % b == 0 and (b * cols * 4 <= CAST_BLOCK_BYTES or b == per))
    out = pl.pallas_call(
        _cast_kernel,
        grid=(rows // blk,),
        in_specs=[pl.BlockSpec((blk, cols), lambda i: (i, 0))],
        out_specs=pl.BlockSpec((blk, cols), lambda i: (i, 0)),
        out_shape=jax.ShapeDtypeStruct((rows, cols), BF16),
        compiler_params=_params(),
        name="to_bf16",
    )(w.reshape(rows, cols))
    return out.reshape(shape)


def _ada_kernel(cond_ref, w_ref, b_ref, o_ref):
    s = jax.nn.silu(cond_ref[...]).astype(BF16)
    o_ref[0] = jnp.dot(s, w_ref[0].astype(BF16), preferred_element_type=F32) + b_ref[0]


def _ada(cond, w_ada, b_ada):
    depth, d, n = w_ada.shape
    tn = 1536
    return pl.pallas_call(
        _ada_kernel,
        grid=(depth, n // tn),
        in_specs=[
            pl.BlockSpec((MOD_ROWS, d), lambda l, j: (0, 0)),
            pl.BlockSpec((1, d, tn), lambda l, j: (l, 0, j)),
            pl.BlockSpec((1, 1, tn), lambda l, j: (l, 0, j)),
        ],
        out_specs=pl.BlockSpec((1, MOD_ROWS, tn), lambda l, j: (l, 0, j)),
        out_shape=jax.ShapeDtypeStruct((depth, MOD_ROWS, n), F32),
        compiler_params=_params(2),
        name="ada",
    )(cond, w_ada, b_ada.reshape(depth, 1, n))


def _den_lane(e):
    return (1 - e) * HEAD_DIM


def _head_lanes(e, axis=1):
    ch = lax.broadcasted_iota(jnp.int32, (1, LANES) if axis == 1 else (LANES, 1), axis)
    own = (ch < HEAD_DIM) if e == 0 else (ch >= HEAD_DIM)
    return own, ch == _den_lane(e)


def _head_q(q2, e):
    own, _ = _head_lanes(e)
    return jnp.where(own, q2, jnp.zeros_like(q2))


def _head_v(v2, e, axis=1):
    own, is_den = _head_lanes(e, axis)
    return jnp.where(own, v2, jnp.broadcast_to(is_den.astype(v2.dtype), v2.shape))


def _normalise(accs):
    outs = [acc / acc[:, _den_lane(e):_den_lane(e) + 1] for e, acc in enumerate(accs)]
    own0, _ = _head_lanes(0)
    return jnp.where(own0, outs[0], outs[1])


def _ctx_attn_kernel(x_ref, mod_ref, g_ref, w_ref, *rest, nb, seq, n_side):
    side_in, (o_ref, sk_ref, sv_ref), side_out, (qkv_scr,) = (
        rest[:n_side], rest[n_side:n_side + 3], rest[n_side + 3:2 * n_side + 3], rest[2 * n_side + 3:])
    _side_cast(side_in, side_out)
    mod = mod_ref[0]
    h = _rms_mod(x_ref[...], g_ref[...], _mod_chunk(mod, 0), _mod_chunk(mod, 1)).astype(BF16)
    qkv_scr[...] = jnp.dot(h, w_ref[...], preferred_element_type=F32)
    for b in range(nb):
        rows = slice(b * seq, (b + 1) * seq)
        for hp in range(N_HEADS // HEADS_PER_BLOCK):
            heads = slice(hp * HEADS_PER_BLOCK, (hp + 1) * HEADS_PER_BLOCK)
            for part, s_ref in ((1, sk_ref), (2, sv_ref)):
                cols = slice(part * D_MODEL + hp * LANES, part * D_MODEL + (hp + 1) * LANES)
                s_ref[b, 0, heads] = qkv_scr[rows, cols].T.reshape(HEADS_PER_BLOCK, HEAD_DIM, seq)

    def block(unit, part):
        b, hp = unit
        return (slice(b * seq, (b + 1) * seq),
                slice(part * D_MODEL + hp * LANES, part * D_MODEL + (hp + 1) * LANES))

    def scores(unit):
        q2 = (qkv_scr[block(unit, 0)] * ATTN_SCALE).astype(BF16)
        k2 = qkv_scr[block(unit, 1)].astype(BF16)
        return [lax.dot_general(_head_q(q2, e), k2, (((1,), (1,)), ((), ())),
                                preferred_element_type=F32) for e in range(HEADS_PER_BLOCK)]

    units = [(b, hp) for b in range(nb) for hp in range(N_HEADS // HEADS_PER_BLOCK)]
    for unit, per_head in _pipelined(units, scores):
        v2 = qkv_scr[block(unit, 2)].astype(BF16)
        o2 = None
        for e, s in enumerate(per_head):
            own, _ = _head_lanes(e)
            ve = jnp.where(own, v2, jnp.zeros_like(v2))
            p = jnp.exp(s - jnp.max(s, axis=-1, keepdims=True))
            oe = (jnp.dot(p.astype(BF16), ve, preferred_element_type=F32)
                  / jnp.sum(p, axis=-1, keepdims=True))
            o2 = oe if o2 is None else o2 + oe
        o_ref[block(unit, 0)] = o2.astype(BF16)


def _ctx_attn(x, mods3, mod_row, g, w_qkv, side, *, batch, seq):
    n, d = x.shape
    nb = 2
    tm = nb * seq
    side_in, side_out, side_shapes = _side_cast_specs(side, n // tm)
    kern = functools.partial(_ctx_attn_kernel, nb=nb, seq=seq, n_side=len(side))
    state = jax.ShapeDtypeStruct((batch, 1, N_HEADS, HEAD_DIM, seq), F32)
    state_spec = pl.BlockSpec((nb, 1, N_HEADS, HEAD_DIM, seq), lambda i: (i, 0, 0, 0, 0))
    return pl.pallas_call(
        kern,
        grid=(n // tm,),
        in_specs=[
            pl.BlockSpec((tm, d), lambda i: (i, 0)),
            pl.BlockSpec((1, 1, 6 * d), lambda i: (mod_row, 0, 0)),
            _resident((1, d)),
            _resident((d, 3 * d)),
        ] + side_in,
        out_specs=[pl.BlockSpec((tm, d), lambda i: (i, 0)), state_spec, state_spec] + side_out,
        out_shape=[jax.ShapeDtypeStruct((n, d), BF16), state, state] + side_shapes,
        scratch_shapes=[pltpu.VMEM((tm, 3 * d), F32)],
        compiler_params=_params(),
        name="ctx_attn",
    )(x, mods3, g, w_qkv, *(w for w, _ in side))


def _qkv_kernel(x_ref, mod_ref, g_ref, w_ref, o_ref, *, tn):
    mod = mod_ref[0]
    h = _rms_mod(x_ref[...], g_ref[...], _mod_chunk(mod, 0), _mod_chunk(mod, 1)).astype(BF16)
    for c in range(w_ref.shape[1] // tn):
        cols = slice(c * tn, (c + 1) * tn)
        o_ref[:, cols] = jnp.dot(h, w_ref[:, cols], preferred_element_type=F32).astype(BF16)


def _qkv(x, mods3, mod_row_fn, g, w_qkv, *, tm):
    n, d = x.shape
    nout = w_qkv.shape[1]
    return pl.pallas_call(
        functools.partial(_qkv_kernel, tn=512),
        grid=(n // tm,),
        in_specs=[
            pl.BlockSpec((tm, d), lambda i: (i, 0)),
            pl.BlockSpec((1, 1, 6 * d), lambda i: (mod_row_fn(i), 0, 0)),
            _resident((1, d)),
            _resident((d, nout)),
        ],
        out_specs=pl.BlockSpec((tm, nout), lambda i: (i, 0)),
        out_shape=jax.ShapeDtypeStruct((n, nout), BF16),
        compiler_params=_params(),
        name="qkv",
    )(x, mods3, g, w_qkv)


def _na_groups(rows):
    kh = min(NA_ROWS, rows)
    row_start = [min(max(r - kh // 2, 0), rows - kh) for r in range(rows)]
    groups, off = [], 0
    for r0 in range(0, rows, NA_QROWS):
        lo = min(row_start[r0:r0 + NA_QROWS])
        hi = max(row_start[r0:r0 + NA_QROWS]) + kh
        if (hi - lo) * GRID_W % LANES:
            if hi < rows:
                hi += 1
            else:
                lo -= 1
        nk = (hi - lo) * GRID_W
        groups.append((r0 * GRID_W, lo * GRID_W, nk, off))
        off += nk
    return tuple(groups), row_start, kh


def _na_bias_plan(rows):
    groups, row_start, kh = _na_groups(rows)
    nro = 2 * NA_ROWS - 1
    w = GRID_W
    assert 2 * w == LANES
    plan = []
    for (q0, k0, nk, off) in groups:
        for qi in range(NA_QROWS):
            qr = q0 // w + qi
            rs = row_start[qr]
            tile_of = lambda kr: kr - qr + NA_ROWS - 1 if rs <= kr < rs + kh else nro
            for p in range(nk // LANES):
                kr = k0 // w + 2 * p
                plan.append((qi * w, off + p * LANES, tile_of(kr), tile_of(kr + 1)))
    return tuple(plan), sum(g[2] for g in groups)


def _pad_rpb(rpb):
    lead = GRID_W - NA_COLS
    return jnp.pad(rpb, ((0, 0), (0, 1), (lead, LANES - rpb.shape[2] - lead)))


def _fill_na_bias(g_ref, o_ref, lo_scr, hi_scr, plan):
    w = GRID_W
    n_tiles = 2 * NA_ROWS - 1
    lane = lax.broadcasted_iota(jnp.int32, (w, LANES), 1)
    qc = lax.broadcasted_iota(jnp.int32, (w, LANES), 0)
    kc = lane % w
    col_start = jnp.clip(qc - NA_COLS // 2, 0, w - NA_COLS)
    col_ok = (kc >= col_start) & (kc < col_start + NA_COLS)
    in_lo = lane < w
    neg = jnp.full((w, LANES), NEG_BIAS, F32)
    for a in range(n_tiles):
        row = jnp.broadcast_to(g_ref[a:a + 1, :], (w, LANES))
        lo_scr[a] = jnp.where(col_ok, pltpu.roll(row, w + 1, 1, stride=1, stride_axis=0), neg)
        hi_scr[a] = jnp.where(col_ok, pltpu.roll(row, 1, 1, stride=1, stride_axis=0), neg)
    lo_scr[n_tiles] = neg
    hi_scr[n_tiles] = neg
    for (r0, c0, a_lo, a_hi) in plan:
        o_ref[r0:r0 + w, c0:c0 + LANES] = jnp.where(in_lo, lo_scr[a_lo], hi_scr[a_hi])


def _nattn_kernel(q_ref, k_ref, v_ref, ck_ref, cv_ref, g_ref, o_ref, bias_ref, lo_scr, hi_scr,
                  *, groups, plan):
    @pl.when(pl.program_id(1) == 0)
    def _():
        for e in range(HEADS_PER_BLOCK):
            _fill_na_bias(g_ref.at[e], bias_ref.at[e], lo_scr, hi_scr, plan)

    q2, k2, v2 = q_ref[0], k_ref[0], v_ref[0]
    past = ck_ref.shape[-1]
    kct = ck_ref[0, 0].reshape(LANES, past).astype(BF16)
    vct = cv_ref[0, 0].reshape(LANES, past).astype(BF16)
    q2 = q2 * ATTN_SCALE
    nq = NA_QROWS * GRID_W
    dn = (((1,), (1,)), ((), ()))
    units = [(grp, e) for grp in groups for e in range(HEADS_PER_BLOCK)]

    def scores(unit):
        (q0, k0, nk, boff), e = unit
        qe = _head_q(q2[q0:q0 + nq], e)
        s_w = (lax.dot_general(qe, k2[k0:k0 + nk], dn, preferred_element_type=F32)
               + bias_ref[e, :, boff:boff + nk])
        return s_w, jnp.dot(qe, kct, preferred_element_type=F32)

    def attend(unit, s_w, s_c):
        (_, k0, nk, _), e = unit
        mx = jnp.maximum(jnp.max(s_w, axis=-1, keepdims=True),
                         jnp.max(s_c, axis=-1, keepdims=True))
        return (jnp.dot(jnp.exp(s_w - mx).astype(BF16), _head_v(v2[k0:k0 + nk], e),
                        preferred_element_type=F32)
                + lax.dot_general(jnp.exp(s_c - mx).astype(BF16), _head_v(vct, e, axis=0), dn,
                                  preferred_element_type=F32))

    accs = []
    for unit, s in _pipelined(units, scores):
        accs.append(attend(unit, *s))
        if len(accs) == HEADS_PER_BLOCK:
            q0 = unit[0][0]
            o_ref[0, q0:q0 + nq, :] = _normalise(accs).astype(BF16)
            accs = []


def _nattn(qkv, cache_kt, cache_vt, rpb, *, layer):
    b, t, _ = qkv.shape
    past = cache_kt.shape[4]
    groups, _, _ = _na_groups(t // GRID_W)
    plan, nbias = _na_bias_plan(t // GRID_W)
    nhp = N_HEADS // HEADS_PER_BLOCK
    nq = NA_QROWS * GRID_W
    n_tiles = 2 * NA_ROWS
    cache_spec = pl.BlockSpec((1, 1, HEADS_PER_BLOCK, HEAD_DIM, past),
                              lambda hp, bi: (bi, layer, hp, 0, 0))
    return pl.pallas_call(
        functools.partial(_nattn_kernel, groups=groups, plan=plan),
        grid=(nhp, b),
        in_specs=[
            pl.BlockSpec((1, t, LANES), lambda hp, bi: (bi, 0, hp)),
            pl.BlockSpec((1, t, LANES), lambda hp, bi: (bi, 0, nhp + hp)),
            pl.BlockSpec((1, t, LANES), lambda hp, bi: (bi, 0, 2 * nhp + hp)),
            cache_spec,
            cache_spec,
            pl.BlockSpec((HEADS_PER_BLOCK, n_tiles, LANES), lambda hp, bi: (hp, 0, 0)),
        ],
        out_specs=pl.BlockSpec((1, t, LANES), lambda hp, bi: (bi, 0, hp)),
        out_shape=jax.ShapeDtypeStruct((b, t, D_MODEL), BF16),
        scratch_shapes=[pltpu.VMEM((HEADS_PER_BLOCK, nq, nbias), F32),
                        pltpu.VMEM((n_tiles, GRID_W, LANES), F32),
                        pltpu.VMEM((n_tiles, GRID_W, LANES), F32)],
        compiler_params=_params(2),
        name="nattn",
    )(qkv, qkv, qkv, cache_kt, cache_vt, _pad_rpb(rpb))


GELU_C = float(np.float32(np.sqrt(2 / np.pi)))
GELU_K = -2.0 * GELU_C * math.log2(math.e)


def _gelu(x):
    return x / (1.0 + jnp.exp2(x * (x * x * (GELU_K * 0.044715) + GELU_K)))


def _gmlp_kernel(x_ref, mod_ref, g_ref, win_f32, gv_ref, ws_ref, bs_ref, *rest, n_cast, tm, n_side):
    side_in, (o_ref,), side_out, (win_ref, vv_scr) = (
        rest[:n_side], rest[n_side:n_side + 1], rest[n_side + 1:2 * n_side + 1], rest[2 * n_side + 1:])
    _side_cast(side_in, side_out)

    def run(_):
        _gmlp_block(x_ref, mod_ref, g_ref, win_ref, gv_ref, ws_ref, bs_ref, o_ref, vv_scr, tm=tm)

    _cast_then_run(n_cast, ((win_f32, win_ref),), run)


def _gmlp_block(x_ref, mod_ref, g_ref, win_ref, gv_ref, ws_ref, bs_ref, o_ref, vv_scr, *, tm):
    e = GMLP_WIDTH
    gd = e // GMLP_GROUPS
    mod = mod_ref[0]
    h = _rms_mod(x_ref[...], g_ref[...], _mod_chunk(mod, 0), _mod_chunk(mod, 1)).astype(BF16)
    pair = 2 * gd

    def u_proj(gp):
        return jnp.dot(h, win_ref[:, gp * pair:(gp + 1) * pair], preferred_element_type=F32)

    zv = _gelu(jnp.dot(h, win_ref[:, e:2 * e], preferred_element_type=F32))
    pipe = _pipelined(list(range(e // pair)), u_proj)
    head = next(pipe)
    xc = zv - jnp.mean(zv, axis=-1, keepdims=True)
    vv = xc * lax.rsqrt(jnp.mean(xc * xc, axis=-1, keepdims=True) + EPS) * gv_ref[...]
    vv_scr[...] = vv.astype(BF16)
    for gp, u_raw in itertools.chain([head], pipe):
        u2 = _gelu(u_raw)
        for gi in range(2):
            g = 2 * gp + gi
            cols = slice(g * gd, (g + 1) * gd)
            nc = tm // CHUNK
            vcat = jnp.concatenate(
                [vv_scr[c * CHUNK:(c + 1) * CHUNK, cols] for c in range(nc)], axis=1)
            mixed = jnp.dot(ws_ref[g].astype(BF16), vcat, preferred_element_type=F32)
            for c in range(nc):
                rows = slice(c * CHUNK, (c + 1) * CHUNK)
                o_ref[rows, cols] = (u2[rows, gi * gd:(gi + 1) * gd]
                                     * (mixed[:, c * gd:(c + 1) * gd] + bs_ref[g])).astype(BF16)


def _gmlp(x, mods3, mod_row_fn, g, w_in, g_v, w_s, b_s_full, side=(), *, tm):
    n, d = x.shape
    e = GMLP_WIDTH
    nc = CAST_STEPS
    assert d % (nc * HALO) == 0
    blk, chunk = _cast_maps(nc)
    side_in, side_out, side_shapes = _side_cast_specs(side, nc + n // tm)
    return pl.pallas_call(
        functools.partial(_gmlp_kernel, n_cast=nc, tm=tm, n_side=len(side)),
        grid=(nc + n // tm,),
        in_specs=[
            pl.BlockSpec((tm, d), lambda i: (blk(i), 0)),
            pl.BlockSpec((1, 1, 6 * d), lambda i: (mod_row_fn(blk(i)), 0, 0)),
            _resident((1, d)),
            pl.BlockSpec((d // nc, 2 * e), lambda i: (chunk(i), 0)),
            _resident((1, e)),
            _resident(w_s.shape),
            _resident(b_s_full.shape),
        ] + side_in,
        out_specs=[pl.BlockSpec((tm, e), lambda i: (blk(i), 0))] + side_out,
        out_shape=[jax.ShapeDtypeStruct((n, e), BF16)] + side_shapes,
        scratch_shapes=[pltpu.VMEM((d, 2 * e), BF16), pltpu.VMEM((tm, e), BF16)],
        compiler_params=_params(),
        name="gmlp",
    )(x, mods3, g, w_in, g_v, w_s, b_s_full, *(w for w, _ in side))


def _conv_pieces(tm, seq, halo):
    if halo:
        return ((0, tm + 2 * HALO, HALO, tm),)
    return tuple((s * seq, seq, 0, seq) for s in range(tm // seq))


def _mix_ffn_kernel(x_ref, xp_ref, xn_ref, m_ref, mp_ref, mn_ref, mod_ref, gffn_ref, gfin_ref,
                    wmo_ref, wup_ref, wconv_ref, bconv_ref, wdown_ref, o_ref, act_scr,
                    *, tm, seq, halo, final):
    mod = mod_ref[0]
    gate_mix, shift, scale, gate_ffn = (_mod_chunk(mod, k) for k in (2, 3, 4, 5))
    if halo:
        xcat = jnp.concatenate([xp_ref[0], x_ref[...], xn_ref[0]], axis=0)
        mcat = jnp.concatenate([mp_ref[0], m_ref[...], mn_ref[0]], axis=0)
        main = slice(HALO, HALO + tm)
    else:
        xcat, mcat = x_ref[...], m_ref[...]
        main = slice(0, tm)
    r = xcat.shape[0]
    x1 = xcat + gate_mix * jnp.dot(mcat, wmo_ref[...], preferred_element_type=F32)
    hcat = _rms_mod(x1, gffn_ref[...], shift, scale).astype(BF16)

    f = FFN_TILE
    if halo:
        blk = pl.program_id(0) % (seq // tm)
        hcat = jnp.concatenate([
            jnp.where(blk == 0, jnp.zeros((HALO, D_MODEL), BF16), hcat[:HALO]),
            hcat[main],
            jnp.where(blk == seq // tm - 1, jnp.zeros((HALO, D_MODEL), BF16), hcat[HALO + tm:]),
        ], axis=0)
    pieces = _conv_pieces(tm, seq, halo)
    sub = lax.broadcasted_iota(jnp.int32, (PAD, 1), 0)

    def up(c):
        return [jnp.dot(hcat, wup_ref[:, col:col + f], preferred_element_type=F32)
                for col in (c * f, FFN_DIM + c * f)]

    def conv(a, col):
        cols = slice(col, col + f)
        wc = wconv_ref[0, :, cols]
        out = []
        for (a0, n_rows, own0, own) in pieces:
            piece = a[a0:a0 + n_rows]
            own_rows = slice(own0, own0 + own)
            prev = pltpu.roll(piece, 1, 0)[own_rows]
            nxt = pltpu.roll(piece, n_rows - 1, 0)[own_rows]
            if not halo:
                prev = jnp.concatenate([jnp.where(sub == 0, 0.0, prev[:PAD]), prev[PAD:]], axis=0)
                nxt = jnp.concatenate([nxt[:-PAD], jnp.where(sub == PAD - 1, 0.0, nxt[-PAD:])], axis=0)
            out.append(prev * wc[0:1] + piece[own_rows] * wc[1:2] + nxt * wc[2:3]
                       + bconv_ref[0, :, cols])
        return out[0] if len(out) == 1 else jnp.concatenate(out, axis=0)

    for c, (a_gate, a_val) in _pipelined(list(range(FFN_DIM // f)), up):
        act = jax.nn.silu(conv(a_gate, c * f)) * conv(a_val, FFN_DIM + c * f)
        act_scr[:, c * f:(c + 1) * f] = act.astype(BF16)
    acc = jnp.dot(act_scr[...], wdown_ref[...], preferred_element_type=F32)
    out = x1[main] + gate_ffn * acc
    if final:
        out = out * lax.rsqrt(jnp.mean(out * out, axis=-1, keepdims=True) + EPS) * gfin_ref[...]
    o_ref[...] = out


def _mix_ffn(x, m, mods3, mod_row_fn, g_ffn, g_fin, w_mo, w_up, w_conv, b_conv, w_down,
             *, layer, tm, seq, final):
    n, d = x.shape
    km = m.shape[1]
    halo = tm % seq != 0
    assert not halo or seq % tm == 0
    nh = n // HALO
    per = tm // HALO
    prev_map = lambda i: (jnp.maximum(i * per - 1, 0), 0, 0)
    next_map = lambda i: (jnp.minimum((i + 1) * per, nh - 1), 0, 0)
    kern = functools.partial(_mix_ffn_kernel, tm=tm, seq=seq, halo=halo, final=final)
    return pl.pallas_call(
        kern,
        grid=(n // tm,),
        in_specs=[
            pl.BlockSpec((tm, d), lambda i: (i, 0)),
            pl.BlockSpec((1, HALO, d), prev_map),
            pl.BlockSpec((1, HALO, d), next_map),
            pl.BlockSpec((tm, km), lambda i: (i, 0)),
            pl.BlockSpec((1, HALO, km), prev_map),
            pl.BlockSpec((1, HALO, km), next_map),
            pl.BlockSpec((1, 1, 6 * d), lambda i: (mod_row_fn(i), 0, 0)),
            _resident((1, d)),
            _resident((1, d)),
            _resident(w_mo.shape),
            _resident(w_up.shape),
            _resident_layer(w_conv.shape, layer),
            _resident_layer(b_conv.shape, layer),
            _resident(w_down.shape),
        ],
        out_specs=pl.BlockSpec((tm, d), lambda i: (i, 0)),
        out_shape=jax.ShapeDtypeStruct((n, d), F32),
        scratch_shapes=[pltpu.VMEM((tm, FFN_DIM), BF16)],
        compiler_params=_params(),
        name="mix_ffn",
    )(x, x.reshape(nh, HALO, d), x.reshape(nh, HALO, d),
      m, m.reshape(nh, HALO, km), m.reshape(nh, HALO, km),
      mods3, g_ffn, g_fin, w_mo, w_up, w_conv, b_conv, w_down)


def kernel(x_prompt, x_sample, cache_k, cache_v, c, c_ctx, w_ada, b_ada, norm_mix_g,
           norm_ffn_g, norm_final_g, w_qkv, w_attn_out, rpb, w_gmlp_in, g_gmlp_v,
           w_spatial, b_spatial, w_gmlp_out, w_ffn_up, w_ffn_conv, b_ffn_conv, w_ffn_down):
    batch, seq, d = x_prompt.shape
    dec_batch, dec_seq, _ = x_sample.shape
    depth = w_ada.shape[0]
    assert depth == 2 and d == D_MODEL

    cond = jnp.zeros((MOD_ROWS, d), F32).at[0].set(c_ctx).at[1:1 + dec_batch].set(c)
    mods3 = _ada(cond, w_ada, b_ada).reshape(depth * MOD_ROWS, 1, 6 * d)

    w_qkv_b = _to_bf16(w_qkv[0])
    w_gin = w_gmlp_in[0]
    w_sp = w_spatial[0]
    b_s_full = jnp.broadcast_to(b_spatial[0][:, :, None], (GMLP_GROUPS, CHUNK, CHUNK))
    conv_w = (w_ffn_conv, b_ffn_conv[:, None, :])
    g_mix = norm_mix_g.reshape(depth, 1, d)
    g_ffn = norm_ffn_g.reshape(depth, 1, d)
    g_fin = norm_final_g.reshape(1, d)
    tm = 512

    def ctx_row(layer):
        return lambda i: layer * MOD_ROWS
    x = x_prompt.reshape(batch * seq, d)
    o, state_kt, state_vt, w_ao, w_up0, w_down0 = _ctx_attn(
        x, mods3, 0, g_mix[0], w_qkv_b, [(w_attn_out, 0), (w_ffn_up, 0), (w_ffn_down, 0)],
        batch=batch, seq=seq)
    state_k, state_v = jnp.swapaxes(state_kt, 3, 4), jnp.swapaxes(state_vt, 3, 4)
    x = _mix_ffn(x, o, mods3, ctx_row(0), g_ffn[0], g_fin, w_ao, w_up0, *conv_w, w_down0,
                 layer=0, tm=tm, seq=seq, final=False)
    m, w_gout, w_up1, w_down1 = _gmlp(
        x, mods3, ctx_row(1), g_mix[1], w_gin, g_gmlp_v[0:1], w_sp, b_s_full,
        [(w_gmlp_out, 0), (w_ffn_up, 1), (w_ffn_down, 1)], tm=2 * tm)
    y_prompt = _mix_ffn(x, m, mods3, ctx_row(1), g_ffn[1], g_fin, w_gout, w_up1, *conv_w, w_down1,
                        layer=1, tm=tm, seq=seq, final=True).reshape(batch, seq, d)

    def lat_row(layer):
        return lambda i: layer * MOD_ROWS + 1 + (i * tm) // dec_seq
    x = x_sample.reshape(dec_batch * dec_seq, d)
    qkv = _qkv(x, mods3, lat_row(0), g_mix[0], w_qkv_b, tm=tm)
    o = _nattn(qkv.reshape(dec_batch, dec_seq, 3 * d), jnp.swapaxes(cache_k, 3, 4),
               jnp.swapaxes(cache_v, 3, 4), rpb[0], layer=0)
    x = _mix_ffn(x, o.reshape(dec_batch * dec_seq, d), mods3, lat_row(0), g_ffn[0], g_fin,
                 w_ao, w_up0, *conv_w, w_down0, layer=0, tm=tm, seq=dec_seq, final=False)
    m, = _gmlp(x, mods3, lat_row(1), g_mix[1], w_gin, g_gmlp_v[0:1], w_sp, b_s_full, tm=tm)
    y_sample = _mix_ffn(x, m, mods3, lat_row(1), g_ffn[1], g_fin, w_gout, w_up1, *conv_w, w_down1,
                        layer=1, tm=tm, seq=dec_seq, final=True).reshape(dec_batch, dec_seq, d)

    return (y_prompt, y_sample, state_k, state_v)
```

```python
import functools
import itertools
import math

import numpy as np
import jax
import jax.numpy as jnp
from jax import lax
from jax.experimental import pallas as pl
from jax.experimental.pallas import tpu as pltpu

D_MODEL = 1024
N_HEADS = 16
HEAD_DIM = D_MODEL // N_HEADS
GRID_W = 64
NA_ROWS = 8
NA_COLS = 16
CHUNK = 128
GMLP_WIDTH = 2 * D_MODEL
GMLP_GROUPS = 16
FFN_DIM = 2816
CONV_W = 3
EPS = 1e-6
ATTN_SCALE = HEAD_DIM ** -0.5

LANES = 128
HEADS_PER_BLOCK = LANES // HEAD_DIM
MOD_ROWS = 8
HALO = 16
PAD = 8
FFN_TILE = 256
CAST_STEPS = 8
NA_QROWS = 4
LOOKAHEAD = 1
NEG_BIAS = -1e30
VMEM_LIMIT = 56 * 1024 * 1024

BF16 = jnp.bfloat16
F32 = jnp.float32


def _params(n_axes=1):
    return pltpu.CompilerParams(
        dimension_semantics=("arbitrary",) * n_axes, vmem_limit_bytes=VMEM_LIMIT)


def _resident(shape):
    nd = len(shape)
    return pl.BlockSpec(shape, lambda *_: (0,) * nd, pipeline_mode=pl.Buffered(1))


def _resident_layer(shape, layer):
    nd = len(shape)
    return pl.BlockSpec((1,) + tuple(shape[1:]), lambda *_: (layer,) + (0,) * (nd - 1),
                        pipeline_mode=pl.Buffered(1))


def _pipelined(items, first_stage):
    ready = [first_stage(it) for it in items[:LOOKAHEAD]]
    for i, item in enumerate(items):
        if i + LOOKAHEAD < len(items):
            ready.append(first_stage(items[i + LOOKAHEAD]))
        yield item, ready.pop(0)


def _cast_then_run(n_cast, weights, run):
    step = pl.program_id(0)

    @pl.when(step < n_cast)
    def _():
        for src, dst in weights:
            rows = src.shape[-2]
            dst[pl.ds(pl.multiple_of(step * rows, rows), rows), :] = (
                src[...].reshape(rows, src.shape[-1]).astype(BF16))

    @pl.when(step >= n_cast)
    def _():
        run(step - n_cast)


def _side_cast_specs(weights, n_steps, step_of=lambda i: i):
    ins, outs, shapes = [], [], []
    for w, layer in weights:
        _, rows, cols = w.shape
        assert rows % (n_steps * HALO) == 0
        ins.append(pl.BlockSpec((1, rows // n_steps, cols),
                                lambda i, layer=layer: (layer, step_of(i), 0)))
        outs.append(pl.BlockSpec((rows // n_steps, cols), lambda i: (step_of(i), 0)))
        shapes.append(jax.ShapeDtypeStruct((rows, cols), BF16))
    return ins, outs, shapes


def _side_cast(src_refs, dst_refs):
    for src, dst in zip(src_refs, dst_refs):
        dst[...] = src[0].astype(BF16)


def _cast_maps(n_cast):
    return (lambda i: jnp.maximum(i - n_cast, 0)), (lambda i: jnp.minimum(i, n_cast - 1))


def _rms_mod(x, g, shift, scale):
    y = x * lax.rsqrt(jnp.mean(x * x, axis=-1, keepdims=True) + EPS)
    return (y * g) * (1 + scale) + shift


def _mod_chunk(mod, k):
    return mod[:, k * D_MODEL:(k + 1) * D_MODEL]


CAST_BLOCK_BYTES = 8 * 1024 * 1024


def _cast_kernel(w_ref, o_ref):
    o_ref[...] = w_ref[...].astype(o_ref.dtype)


def _to_bf16(w):
    shape = w.shape
    cols = shape[-1]
    rows = w.size // cols
    per = 16
    assert rows % per == 0
    blk = max(b for b in range(per, rows + 1, per)
              if rows % b == 0 and (b * cols * 4 <= CAST_BLOCK_BYTES or b == per))
    out = pl.pallas_call(
        _cast_kernel,
        grid=(rows // blk,),
        in_specs=[pl.BlockSpec((blk, cols), lambda i: (i, 0))],
        out_specs=pl.BlockSpec((blk, cols), lambda i: (i, 0)),
        out_shape=jax.ShapeDtypeStruct((rows, cols), BF16),
        compiler_params=_params(),
        name="to_bf16",
    )(w.reshape(rows, cols))
    return out.reshape(shape)


def _ada_kernel(cond_ref, w_ref, b_ref, o_ref):
    s = jax.nn.silu(cond_ref[...]).astype(BF16)
    o_ref[0] = jnp.dot(s, w_ref[0].astype(BF16), preferred_element_type=F32) + b_ref[0]


def _ada(cond, w_ada, b_ada):
    depth, d, n = w_ada.shape
    tn = 1536
    return pl.pallas_call(
        _ada_kernel,
        grid=(depth, n // tn),
        in_specs=[
            pl.BlockSpec((MOD_ROWS, d), lambda l, j: (0, 0)),
            pl.BlockSpec((1, d, tn), lambda l, j: (l, 0, j)),
            pl.BlockSpec((1, 1, tn), lambda l, j: (l, 0, j)),
        ],
        out_specs=pl.BlockSpec((1, MOD_ROWS, tn), lambda l, j: (l, 0, j)),
        out_shape=jax.ShapeDtypeStruct((depth, MOD_ROWS, n), F32),
        compiler_params=_params(2),
        name="ada",
    )(cond, w_ada, b_ada.reshape(depth, 1, n))


def _den_lane(e):
    return (1 - e) * HEAD_DIM


def _head_lanes(e, axis=1):
    ch = lax.broadcasted_iota(jnp.int32, (1, LANES) if axis == 1 else (LANES, 1), axis)
    own = (ch < HEAD_DIM) if e == 0 else (ch >= HEAD_DIM)
    return own, ch == _den_lane(e)


def _head_q(q2, e):
    own, _ = _head_lanes(e)
    return jnp.where(own, q2, jnp.zeros_like(q2))


def _head_v(v2, e, axis=1):
    own, is_den = _head_lanes(e, axis)
    return jnp.where(own, v2, jnp.broadcast_to(is_den.astype(v2.dtype), v2.shape))


def _normalise(accs):
    outs = [acc / acc[:, _den_lane(e):_den_lane(e) + 1] for e, acc in enumerate(accs)]
    own0, _ = _head_lanes(0)
    return jnp.where(own0, outs[0], outs[1])


def _ctx_attn_kernel(x_ref, mod_ref, g_ref, w_ref, *rest, nb, seq, n_side):
    side_in, (o_ref, sk_ref, sv_ref), side_out, (qkv_scr,) = (
        rest[:n_side], rest[n_side:n_side + 3], rest[n_side + 3:2 * n_side + 3], rest[2 * n_side + 3:])
    _side_cast(side_in, side_out)
    mod = mod_ref[0]
    h = _rms_mod(x_ref[...], g_ref[...], _mod_chunk(mod, 0), _mod_chunk(mod, 1)).astype(BF16)
    qkv_scr[...] = jnp.dot(h, w_ref[...], preferred_element_type=F32)
    for b in range(nb):
        rows = slice(b * seq, (b + 1) * seq)
        for hp in range(N_HEADS // HEADS_PER_BLOCK):
            heads = slice(hp * HEADS_PER_BLOCK, (hp + 1) * HEADS_PER_BLOCK)
            for part, s_ref in ((1, sk_ref), (2, sv_ref)):
                cols = slice(part * D_MODEL + hp * LANES, part * D_MODEL + (hp + 1) * LANES)
                s_ref[b, 0, heads] = qkv_scr[rows, cols].T.reshape(HEADS_PER_BLOCK, HEAD_DIM, seq)

    def block(unit, part):
        b, hp = unit
        return (slice(b * seq, (b + 1) * seq),
                slice(part * D_MODEL + hp * LANES, part * D_MODEL + (hp + 1) * LANES))

    def scores(unit):
        q2 = (qkv_scr[block(unit, 0)] * ATTN_SCALE).astype(BF16)
        k2 = qkv_scr[block(unit, 1)].astype(BF16)
        return [lax.dot_general(_head_q(q2, e), k2, (((1,), (1,)), ((), ())),
                                preferred_element_type=F32) for e in range(HEADS_PER_BLOCK)]

    units = [(b, hp) for b in range(nb) for hp in range(N_HEADS // HEADS_PER_BLOCK)]
    for unit, per_head in _pipelined(units, scores):
        v2 = qkv_scr[block(unit, 2)].astype(BF16)
        o2 = None
        for e, s in enumerate(per_head):
            own, _ = _head_lanes(e)
            ve = jnp.where(own, v2, jnp.zeros_like(v2))
            p = jnp.exp(s - jnp.max(s, axis=-1, keepdims=True))
            oe = (jnp.dot(p.astype(BF16), ve, preferred_element_type=F32)
                  / jnp.sum(p, axis=-1, keepdims=True))
            o2 = oe if o2 is None else o2 + oe
        o_ref[block(unit, 0)] = o2.astype(BF16)


def _ctx_attn(x, mods3, mod_row, g, w_qkv, side, *, batch, seq):
    n, d = x.shape
    nb = 2
    tm = nb * seq
    side_in, side_out, side_shapes = _side_cast_specs(side, n // tm)
    kern = functools.partial(_ctx_attn_kernel, nb=nb, seq=seq, n_side=len(side))
    state = jax.ShapeDtypeStruct((batch, 1, N_HEADS, HEAD_DIM, seq), F32)
    state_spec = pl.BlockSpec((nb, 1, N_HEADS, HEAD_DIM, seq), lambda i: (i, 0, 0, 0, 0))
    return pl.pallas_call(
        kern,
        grid=(n // tm,),
        in_specs=[
            pl.BlockSpec((tm, d), lambda i: (i, 0)),
            pl.BlockSpec((1, 1, 6 * d), lambda i: (mod_row, 0, 0)),
            _resident((1, d)),
            _resident((d, 3 * d)),
        ] + side_in,
        out_specs=[pl.BlockSpec((tm, d), lambda i: (i, 0)), state_spec, state_spec] + side_out,
        out_shape=[jax.ShapeDtypeStruct((n, d), BF16), state, state] + side_shapes,
        scratch_shapes=[pltpu.VMEM((tm, 3 * d), F32)],
        compiler_params=_params(),
        name="ctx_attn",
    )(x, mods3, g, w_qkv, *(w for w, _ in side))


def _qkv_kernel(x_ref, mod_ref, g_ref, w_ref, o_ref, *, tn):
    mod = mod_ref[0]
    h = _rms_mod(x_ref[...], g_ref[...], _mod_chunk(mod, 0), _mod_chunk(mod, 1)).astype(BF16)
    for c in range(w_ref.shape[1] // tn):
        cols = slice(c * tn, (c + 1) * tn)
        o_ref[:, cols] = jnp.dot(h, w_ref[:, cols], preferred_element_type=F32).astype(BF16)


def _qkv(x, mods3, mod_row_fn, g, w_qkv, *, tm):
    n, d = x.shape
    nout = w_qkv.shape[1]
    return pl.pallas_call(
        functools.partial(_qkv_kernel, tn=512),
        grid=(n // tm,),
        in_specs=[
            pl.BlockSpec((tm, d), lambda i: (i, 0)),
            pl.BlockSpec((1, 1, 6 * d), lambda i: (mod_row_fn(i), 0, 0)),
            _resident((1, d)),
            _resident((d, nout)),
        ],
        out_specs=pl.BlockSpec((tm, nout), lambda i: (i, 0)),
        out_shape=jax.ShapeDtypeStruct((n, nout), BF16),
        compiler_params=_params(),
        name="qkv",
    )(x, mods3, g, w_qkv)


def _na_groups(rows):
    kh = min(NA_ROWS, rows)
    row_start = [min(max(r - kh // 2, 0), rows - kh) for r in range(rows)]
    groups, off = [], 0
    for r0 in range(0, rows, NA_QROWS):
        lo = min(row_start[r0:r0 + NA_QROWS])
        hi = max(row_start[r0:r0 + NA_QROWS]) + kh
        if (hi - lo) * GRID_W % LANES:
            if hi < rows:
                hi += 1
            else:
                lo -= 1
        nk = (hi - lo) * GRID_W
        groups.append((r0 * GRID_W, lo * GRID_W, nk, off))
        off += nk
    return tuple(groups), row_start, kh


def _na_bias_plan(rows):
    groups, row_start, kh = _na_groups(rows)
    nro = 2 * NA_ROWS - 1
    w = GRID_W
    assert 2 * w == LANES
    plan = []
    for (q0, k0, nk, off) in groups:
        for qi in range(NA_QROWS):
            qr = q0 // w + qi
            rs = row_start[qr]
            tile_of = lambda kr: kr - qr + NA_ROWS - 1 if rs <= kr < rs + kh else nro
            for p in range(nk // LANES):
                kr = k0 // w + 2 * p
                plan.append((qi * w, off + p * LANES, tile_of(kr), tile_of(kr + 1)))
    return tuple(plan), sum(g[2] for g in groups)


def _pad_rpb(rpb):
    lead = GRID_W - NA_COLS
    return jnp.pad(rpb, ((0, 0), (0, 1), (lead, LANES - rpb.shape[2] - lead)))


def _fill_na_bias(g_ref, o_ref, lo_scr, hi_scr, plan):
    w = GRID_W
    n_tiles = 2 * NA_ROWS - 1
    lane = lax.broadcasted_iota(jnp.int32, (w, LANES), 1)
    qc = lax.broadcasted_iota(jnp.int32, (w, LANES), 0)
    kc = lane % w
    col_start = jnp.clip(qc - NA_COLS // 2, 0, w - NA_COLS)
    col_ok = (kc >= col_start) & (kc < col_start + NA_COLS)
    in_lo = lane < w
    neg = jnp.full((w, LANES), NEG_BIAS, F32)
    for a in range(n_tiles):
        row = jnp.broadcast_to(g_ref[a:a + 1, :], (w, LANES))
        lo_scr[a] = jnp.where(col_ok, pltpu.roll(row, w + 1, 1, stride=1, stride_axis=0), neg)
        hi_scr[a] = jnp.where(col_ok, pltpu.roll(row, 1, 1, stride=1, stride_axis=0), neg)
    lo_scr[n_tiles] = neg
    hi_scr[n_tiles] = neg
    for (r0, c0, a_lo, a_hi) in plan:
        o_ref[r0:r0 + w, c0:c0 + LANES] = jnp.where(in_lo, lo_scr[a_lo], hi_scr[a_hi])


def _nattn_kernel(q_ref, k_ref, v_ref, ck_ref, cv_ref, g_ref, o_ref, bias_ref, lo_scr, hi_scr,
                  *, groups, plan):
    @pl.when(pl.program_id(1) == 0)
    def _():
        for e in range(HEADS_PER_BLOCK):
            _fill_na_bias(g_ref.at[e], bias_ref.at[e], lo_scr, hi_scr, plan)

    q2, k2, v2 = q_ref[0], k_ref[0], v_ref[0]
    past = ck_ref.shape[-1]
    kct = ck_ref[0, 0].reshape(LANES, past).astype(BF16)
    vct = cv_ref[0, 0].reshape(LANES, past).astype(BF16)
    q2 = q2 * ATTN_SCALE
    nq = NA_QROWS * GRID_W
    dn = (((1,), (1,)), ((), ()))
    units = [(grp, e) for grp in groups for e in range(HEADS_PER_BLOCK)]

    def scores(unit):
        (q0, k0, nk, boff), e = unit
        qe = _head_q(q2[q0:q0 + nq], e)
        s_w = (lax.dot_general(qe, k2[k0:k0 + nk], dn, preferred_element_type=F32)
               + bias_ref[e, :, boff:boff + nk])
        return s_w, jnp.dot(qe, kct, preferred_element_type=F32)

    def attend(unit, s_w, s_c):
        (_, k0, nk, _), e = unit
        mx = jnp.maximum(jnp.max(s_w, axis=-1, keepdims=True),
                         jnp.max(s_c, axis=-1, keepdims=True))
        return (jnp.dot(jnp.exp(s_w - mx).astype(BF16), _head_v(v2[k0:k0 + nk], e),
                        preferred_element_type=F32)
                + lax.dot_general(jnp.exp(s_c - mx).astype(BF16), _head_v(vct, e, axis=0), dn,
                                  preferred_element_type=F32))

    accs = []
    for unit, s in _pipelined(units, scores):
        accs.append(attend(unit, *s))
        if len(accs) == HEADS_PER_BLOCK:
            q0 = unit[0][0]
            o_ref[0, q0:q0 + nq, :] = _normalise(accs).astype(BF16)
            accs = []


def _nattn(qkv, cache_kt, cache_vt, rpb, *, layer):
    b, t, _ = qkv.shape
    past = cache_kt.shape[4]
    groups, _, _ = _na_groups(t // GRID_W)
    plan, nbias = _na_bias_plan(t // GRID_W)
    nhp = N_HEADS // HEADS_PER_BLOCK
    nq = NA_QROWS * GRID_W
    n_tiles = 2 * NA_ROWS
    cache_spec = pl.BlockSpec((1, 1, HEADS_PER_BLOCK, HEAD_DIM, past),
                              lambda hp, bi: (bi, layer, hp, 0, 0))
    return pl.pallas_call(
        functools.partial(_nattn_kernel, groups=groups, plan=plan),
        grid=(nhp, b),
        in_specs=[
            pl.BlockSpec((1, t, LANES), lambda hp, bi: (bi, 0, hp)),
            pl.BlockSpec((1, t, LANES), lambda hp, bi: (bi, 0, nhp + hp)),
            pl.BlockSpec((1, t, LANES), lambda hp, bi: (bi, 0, 2 * nhp + hp)),
            cache_spec,
            cache_spec,
            pl.BlockSpec((HEADS_PER_BLOCK, n_tiles, LANES), lambda hp, bi: (hp, 0, 0)),
        ],
        out_specs=pl.BlockSpec((1, t, LANES), lambda hp, bi: (bi, 0, hp)),
        out_shape=jax.ShapeDtypeStruct((b, t, D_MODEL), BF16),
        scratch_shapes=[pltpu.VMEM((HEADS_PER_BLOCK, nq, nbias), F32),
                        pltpu.VMEM((n_tiles, GRID_W, LANES), F32),
                        pltpu.VMEM((n_tiles, GRID_W, LANES), F32)],
        compiler_params=_params(2),
        name="nattn",
    )(qkv, qkv, qkv, cache_kt, cache_vt, _pad_rpb(rpb))


GELU_C = float(np.float32(np.sqrt(2 / np.pi)))
GELU_K = -2.0 * GELU_C * math.log2(math.e)


def _gelu(x):
    return x / (1.0 + jnp.exp2(x * (x * x * (GELU_K * 0.044715) + GELU_K)))


def _gmlp_kernel(x_ref, mod_ref, g_ref, win_f32, gv_ref, ws_ref, bs_ref, *rest, n_cast, tm, n_side):
    side_in, (o_ref,), side_out, (win_ref, vv_scr) = (
        rest[:n_side], rest[n_side:n_side + 1], rest[n_side + 1:2 * n_side + 1], rest[2 * n_side + 1:])
    _side_cast(side_in, side_out)

    def run(_):
        _gmlp_block(x_ref, mod_ref, g_ref, win_ref, gv_ref, ws_ref, bs_ref, o_ref, vv_scr, tm=tm)

    _cast_then_run(n_cast, ((win_f32, win_ref),), run)


def _gmlp_block(x_ref, mod_ref, g_ref, win_ref, gv_ref, ws_ref, bs_ref, o_ref, vv_scr, *, tm):
    e = GMLP_WIDTH
    gd = e // GMLP_GROUPS
    mod = mod_ref[0]
    h = _rms_mod(x_ref[...], g_ref[...], _mod_chunk(mod, 0), _mod_chunk(mod, 1)).astype(BF16)
    pair = 2 * gd

    def u_proj(gp):
        return jnp.dot(h, win_ref[:, gp * pair:(gp + 1) * pair], preferred_element_type=F32)

    zv = _gelu(jnp.dot(h, win_ref[:, e:2 * e], preferred_element_type=F32))
    pipe = _pipelined(list(range(e // pair)), u_proj)
    head = next(pipe)
    xc = zv - jnp.mean(zv, axis=-1, keepdims=True)
    vv = xc * lax.rsqrt(jnp.mean(xc * xc, axis=-1, keepdims=True) + EPS) * gv_ref[...]
    vv_scr[...] = vv.astype(BF16)
    for gp, u_raw in itertools.chain([head], pipe):
        u2 = _gelu(u_raw)
        for gi in range(2):
            g = 2 * gp + gi
            cols = slice(g * gd, (g + 1) * gd)
            nc = tm // CHUNK
            vcat = jnp.concatenate(
                [vv_scr[c * CHUNK:(c + 1) * CHUNK, cols] for c in range(nc)], axis=1)
            mixed = jnp.dot(ws_ref[g].astype(BF16), vcat, preferred_element_type=F32)
            for c in range(nc):
                rows = slice(c * CHUNK, (c + 1) * CHUNK)
                o_ref[rows, cols] = (u2[rows, gi * gd:(gi + 1) * gd]
                                     * (mixed[:, c * gd:(c + 1) * gd] + bs_ref[g])).astype(BF16)


def _gmlp(x, mods3, mod_row_fn, g, w_in, g_v, w_s, b_s_full, side=(), *, tm):
    n, d = x.shape
    e = GMLP_WIDTH
    nc = CAST_STEPS
    assert d % (nc * HALO) == 0
    blk, chunk = _cast_maps(nc)
    side_in, side_out, side_shapes = _side_cast_specs(side, n // tm, blk)
    return pl.pallas_call(
        functools.partial(_gmlp_kernel, n_cast=nc, tm=tm, n_side=len(side)),
        grid=(nc + n // tm,),
        in_specs=[
            pl.BlockSpec((tm, d), lambda i: (blk(i), 0)),
            pl.BlockSpec((1, 1, 6 * d), lambda i: (mod_row_fn(blk(i)), 0, 0)),
            _resident((1, d)),
            pl.BlockSpec((d // nc, 2 * e), lambda i: (chunk(i), 0)),
            _resident((1, e)),
            _resident(w_s.shape),
            _resident(b_s_full.shape),
        ] + side_in,
        out_specs=[pl.BlockSpec((tm, e), lambda i: (blk(i), 0))] + side_out,
        out_shape=[jax.ShapeDtypeStruct((n, e), BF16)] + side_shapes,
        scratch_shapes=[pltpu.VMEM((d, 2 * e), BF16), pltpu.VMEM((tm, e), BF16)],
        compiler_params=_params(),
        name="gmlp",
    )(x, mods3, g, w_in, g_v, w_s, b_s_full, *(w for w, _ in side))


def _conv_pieces(tm, seq, halo):
    if halo:
        return ((0, tm + 2 * HALO, HALO, tm),)
    return tuple((s * seq, seq, 0, seq) for s in range(tm // seq))


def _mix_ffn_kernel(x_ref, xp_ref, xn_ref, m_ref, mp_ref, mn_ref, mod_ref, gffn_ref, gfin_ref,
                    wmo_ref, wup_ref, wconv_ref, bconv_ref, wdown_ref, o_ref, act_scr,
                    *, tm, seq, halo, final):
    mod = mod_ref[0]
    gate_mix, shift, scale, gate_ffn = (_mod_chunk(mod, k) for k in (2, 3, 4, 5))
    if halo:
        xcat = jnp.concatenate([xp_ref[0], x_ref[...], xn_ref[0]], axis=0)
        mcat = jnp.concatenate([mp_ref[0], m_ref[...], mn_ref[0]], axis=0)
        main = slice(HALO, HALO + tm)
    else:
        xcat, mcat = x_ref[...], m_ref[...]
        main = slice(0, tm)
    r = xcat.shape[0]
    x1 = xcat + gate_mix * jnp.dot(mcat, wmo_ref[...], preferred_element_type=F32)
    hcat = _rms_mod(x1, gffn_ref[...], shift, scale).astype(BF16)

    f = FFN_TILE
    if halo:
        blk = pl.program_id(0) % (seq // tm)
        hcat = jnp.concatenate([
            jnp.where(blk == 0, jnp.zeros((HALO, D_MODEL), BF16), hcat[:HALO]),
            hcat[main],
            jnp.where(blk == seq // tm - 1, jnp.zeros((HALO, D_MODEL), BF16), hcat[HALO + tm:]),
        ], axis=0)
    pieces = _conv_pieces(tm, seq, halo)
    sub = lax.broadcasted_iota(jnp.int32, (PAD, 1), 0)

    def up(c):
        return [jnp.dot(hcat, wup_ref[:, col:col + f], preferred_element_type=F32)
                for col in (c * f, FFN_DIM + c * f)]

    def conv(a, col):
        cols = slice(col, col + f)
        wc = wconv_ref[0, :, cols]
        out = []
        for (a0, n_rows, own0, own) in pieces:
            piece = a[a0:a0 + n_rows]
            own_rows = slice(own0, own0 + own)
            prev = pltpu.roll(piece, 1, 0)[own_rows]
            nxt = pltpu.roll(piece, n_rows - 1, 0)[own_rows]
            if not halo:
                prev = jnp.concatenate([jnp.where(sub == 0, 0.0, prev[:PAD]), prev[PAD:]], axis=0)
                nxt = jnp.concatenate([nxt[:-PAD], jnp.where(sub == PAD - 1, 0.0, nxt[-PAD:])], axis=0)
            out.append(prev * wc[0:1] + piece[own_rows] * wc[1:2] + nxt * wc[2:3]
                       + bconv_ref[0, :, cols])
        return out[0] if len(out) == 1 else jnp.concatenate(out, axis=0)

    for c, (a_gate, a_val) in _pipelined(list(range(FFN_DIM // f)), up):
        act = jax.nn.silu(conv(a_gate, c * f)) * conv(a_val, FFN_DIM + c * f)
        act_scr[:, c * f:(c + 1) * f] = act.astype(BF16)
    acc = jnp.dot(act_scr[...], wdown_ref[...], preferred_element_type=F32)
    out = x1[main] + gate_ffn * acc
    if final:
        out = out * lax.rsqrt(jnp.mean(out * out, axis=-1, keepdims=True) + EPS) * gfin_ref[...]
    o_ref[...] = out


def _mix_ffn(x, m, mods3, mod_row_fn, g_ffn, g_fin, w_mo, w_up, w_conv, b_conv, w_down,
             *, layer, tm, seq, final):
    n, d = x.shape
    km = m.shape[1]
    halo = tm % seq != 0
    assert not halo or seq % tm == 0
    nh = n // HALO
    per = tm // HALO
    prev_map = lambda i: (jnp.maximum(i * per - 1, 0), 0, 0)
    next_map = lambda i: (jnp.minimum((i + 1) * per, nh - 1), 0, 0)
    kern = functools.partial(_mix_ffn_kernel, tm=tm, seq=seq, halo=halo, final=final)
    return pl.pallas_call(
        kern,
        grid=(n // tm,),
        in_specs=[
            pl.BlockSpec((tm, d), lambda i: (i, 0)),
            pl.BlockSpec((1, HALO, d), prev_map),
            pl.BlockSpec((1, HALO, d), next_map),
            pl.BlockSpec((tm, km), lambda i: (i, 0)),
            pl.BlockSpec((1, HALO, km), prev_map),
            pl.BlockSpec((1, HALO, km), next_map),
            pl.BlockSpec((1, 1, 6 * d), lambda i: (mod_row_fn(i), 0, 0)),
            _resident((1, d)),
            _resident((1, d)),
            _resident(w_mo.shape),
            _resident(w_up.shape),
            _resident_layer(w_conv.shape, layer),
            _resident_layer(b_conv.shape, layer),
            _resident(w_down.shape),
        ],
        out_specs=pl.BlockSpec((tm, d), lambda i: (i, 0)),
        out_shape=jax.ShapeDtypeStruct((n, d), F32),
        scratch_shapes=[pltpu.VMEM((tm, FFN_DIM), BF16)],
        compiler_params=_params(),
        name="mix_ffn",
    )(x, x.reshape(nh, HALO, d), x.reshape(nh, HALO, d),
      m, m.reshape(nh, HALO, km), m.reshape(nh, HALO, km),
      mods3, g_ffn, g_fin, w_mo, w_up, w_conv, b_conv, w_down)


def kernel(x_prompt, x_sample, cache_k, cache_v, c, c_ctx, w_ada, b_ada, norm_mix_g,
           norm_ffn_g, norm_final_g, w_qkv, w_attn_out, rpb, w_gmlp_in, g_gmlp_v,
           w_spatial, b_spatial, w_gmlp_out, w_ffn_up, w_ffn_conv, b_ffn_conv, w_ffn_down):
    batch, seq, d = x_prompt.shape
    dec_batch, dec_seq, _ = x_sample.shape
    depth = w_ada.shape[0]
    assert depth == 2 and d == D_MODEL

    cond = jnp.zeros((MOD_ROWS, d), F32).at[0].set(c_ctx).at[1:1 + dec_batch].set(c)
    mods3 = _ada(cond, w_ada, b_ada).reshape(depth * MOD_ROWS, 1, 6 * d)

    w_qkv_b = _to_bf16(w_qkv[0])
    w_gin = w_gmlp_in[0]
    w_sp = w_spatial[0]
    b_s_full = jnp.broadcast_to(b_spatial[0][:, :, None], (GMLP_GROUPS, CHUNK, CHUNK))
    conv_w = (w_ffn_conv, b_ffn_conv[:, None, :])
    g_mix = norm_mix_g.reshape(depth, 1, d)
    g_ffn = norm_ffn_g.reshape(depth, 1, d)
    g_fin = norm_final_g.reshape(1, d)
    tm = 512

    def ctx_row(layer):
        return lambda i: layer * MOD_ROWS
    x = x_prompt.reshape(batch * seq, d)
    o, state_kt, state_vt, w_ao, w_up0, w_down0 = _ctx_attn(
        x, mods3, 0, g_mix[0], w_qkv_b, [(w_attn_out, 0), (w_ffn_up, 0), (w_ffn_down, 0)],
        batch=batch, seq=seq)
    state_k, state_v = jnp.swapaxes(state_kt, 3, 4), jnp.swapaxes(state_vt, 3, 4)
    x = _mix_ffn(x, o, mods3, ctx_row(0), g_ffn[0], g_fin, w_ao, w_up0, *conv_w, w_down0,
                 layer=0, tm=tm, seq=seq, final=False)
    m, w_gout, w_up1, w_down1 = _gmlp(
        x, mods3, ctx_row(1), g_mix[1], w_gin, g_gmlp_v[0:1], w_sp, b_s_full,
        [(w_gmlp_out, 0), (w_ffn_up, 1), (w_ffn_down, 1)], tm=tm)
    y_prompt = _mix_ffn(x, m, mods3, ctx_row(1), g_ffn[1], g_fin, w_gout, w_up1, *conv_w, w_down1,
                        layer=1, tm=tm, seq=seq, final=True).reshape(batch, seq, d)

    def lat_row(layer):
        return lambda i: layer * MOD_ROWS + 1 + (i * tm) // dec_seq
    x = x_sample.reshape(dec_batch * dec_seq, d)
    qkv = _qkv(x, mods3, lat_row(0), g_mix[0], w_qkv_b, tm=tm)
    o = _nattn(qkv.reshape(dec_batch, dec_seq, 3 * d), jnp.swapaxes(cache_k, 3, 4),
               jnp.swapaxes(cache_v, 3, 4), rpb[0], layer=0)
    x = _mix_ffn(x, o.reshape(dec_batch * dec_seq, d), mods3, lat_row(0), g_ffn[0], g_fin,
                 w_ao, w_up0, *conv_w, w_down0, layer=0, tm=tm, seq=dec_seq, final=False)
    m, = _gmlp(x, mods3, lat_row(1), g_mix[1], w_gin, g_gmlp_v[0:1], w_sp, b_s_full, tm=tm)
    y_sample = _mix_ffn(x, m, mods3, lat_row(1), g_ffn[1], g_fin, w_gout, w_up1, *conv_w, w_down1,
                        layer=1, tm=tm, seq=dec_seq, final=True).reshape(dec_batch, dec_seq, d)

    return (y_prompt, y_sample, state_k, state_v)
```

```python
import functools
import itertools
import math

import numpy as np
import jax
import jax.numpy as jnp
from jax import lax
from jax.experimental import pallas as pl
from jax.experimental.pallas import tpu as pltpu

D_MODEL = 1024
N_HEADS = 16
HEAD_DIM = D_MODEL // N_HEADS
GRID_W = 64
NA_ROWS = 8
NA_COLS = 16
CHUNK = 128
GMLP_WIDTH = 2 * D_MODEL
GMLP_GROUPS = 16
FFN_DIM = 2816
CONV_W = 3
EPS = 1e-6
ATTN_SCALE = HEAD_DIM ** -0.5

LANES = 128
HEADS_PER_BLOCK = LANES // HEAD_DIM
MOD_ROWS = 8
HALO = 16
PAD = 8
FFN_TILE = 256
NA_QROWS = 4
LOOKAHEAD = 1
NEG_BIAS = -1e30
VMEM_LIMIT = 56 * 1024 * 1024

BF16 = jnp.bfloat16
F32 = jnp.float32


def _params(n_axes=1):
    return pltpu.CompilerParams(
        dimension_semantics=("arbitrary",) * n_axes, vmem_limit_bytes=VMEM_LIMIT)


def _resident(shape):
    nd = len(shape)
    return pl.BlockSpec(shape, lambda *_: (0,) * nd, pipeline_mode=pl.Buffered(1))


def _resident_layer(shape, layer):
    nd = len(shape)
    return pl.BlockSpec((1,) + tuple(shape[1:]), lambda *_: (layer,) + (0,) * (nd - 1),
                        pipeline_mode=pl.Buffered(1))


def _pipelined(items, first_stage):
    ready = [first_stage(it) for it in items[:LOOKAHEAD]]
    for i, item in enumerate(items):
        if i + LOOKAHEAD < len(items):
            ready.append(first_stage(items[i + LOOKAHEAD]))
        yield item, ready.pop(0)


def _side_cast_specs(weights, n_steps):
    ins, outs, shapes = [], [], []
    for w, layer in weights:
        _, rows, cols = w.shape
        assert rows % (n_steps * HALO) == 0
        ins.append(pl.BlockSpec((1, rows // n_steps, cols), lambda i, layer=layer: (layer, i, 0)))
        outs.append(pl.BlockSpec((rows // n_steps, cols), lambda i: (i, 0)))
        shapes.append(jax.ShapeDtypeStruct((rows, cols), BF16))
    return ins, outs, shapes


def _side_cast(src_refs, dst_refs):
    for src, dst in zip(src_refs, dst_refs):
        dst[...] = src[0].astype(BF16)


def _rms_mod(x, g, shift, scale):
    y = x * lax.rsqrt(jnp.mean(x * x, axis=-1, keepdims=True) + EPS)
    return (y * g) * (1 + scale) + shift


def _mod_chunk(mod, k):
    return mod[:, k * D_MODEL:(k + 1) * D_MODEL]


CAST_BLOCK_BYTES = 8 * 1024 * 1024


def _cast_kernel(w_ref, o_ref):
    o_ref[...] = w_ref[...].astype(o_ref.dtype)


def _to_bf16(w):
    shape = w.shape
    cols = shape[-1]
    rows = w.size // cols
    per = 16
    assert rows % per == 0
    blk = max(b for b in range(per, rows + 1, per)
              if rows % b == 0 and (b * cols * 4 <= CAST_BLOCK_BYTES or b == per))
    out = pl.pallas_call(
        _cast_kernel,
        grid=(rows // blk,),
        in_specs=[pl.BlockSpec((blk, cols), lambda i: (i, 0))],
        out_specs=pl.BlockSpec((blk, cols), lambda i: (i, 0)),
        out_shape=jax.ShapeDtypeStruct((rows, cols), BF16),
        compiler_params=_params(),
        name="to_bf16",
    )(w.reshape(rows, cols))
    return out.reshape(shape)


def _ada_kernel(cond_ref, w_ref, b_ref, o_ref):
    s = jax.nn.silu(cond_ref[...]).astype(BF16)
    o_ref[0] = jnp.dot(s, w_ref[0].astype(BF16), preferred_element_type=F32) + b_ref[0]


def _ada(cond, w_ada, b_ada):
    depth, d, n = w_ada.shape
    tn = 1536
    return pl.pallas_call(
        _ada_kernel,
        grid=(depth, n // tn),
        in_specs=[
            pl.BlockSpec((MOD_ROWS, d), lambda l, j: (0, 0)),
            pl.BlockSpec((1, d, tn), lambda l, j: (l, 0, j)),
            pl.BlockSpec((1, 1, tn), lambda l, j: (l, 0, j)),
        ],
        out_specs=pl.BlockSpec((1, MOD_ROWS, tn), lambda l, j: (l, 0, j)),
        out_shape=jax.ShapeDtypeStruct((depth, MOD_ROWS, n), F32),
        compiler_params=_params(2),
        name="ada",
    )(cond, w_ada, b_ada.reshape(depth, 1, n))


def _den_lane(e):
    return (1 - e) * HEAD_DIM


def _head_lanes(e, axis=1):
    ch = lax.broadcasted_iota(jnp.int32, (1, LANES) if axis == 1 else (LANES, 1), axis)
    own = (ch < HEAD_DIM) if e == 0 else (ch >= HEAD_DIM)
    return own, ch == _den_lane(e)


def _head_q(q2, e):
    own, _ = _head_lanes(e)
    return jnp.where(own, q2, jnp.zeros_like(q2))


def _head_v(v2, e, axis=1):
    own, is_den = _head_lanes(e, axis)
    return jnp.where(own, v2, jnp.broadcast_to(is_den.astype(v2.dtype), v2.shape))


def _normalise(accs):
    outs = [acc / acc[:, _den_lane(e):_den_lane(e) + 1] for e, acc in enumerate(accs)]
    own0, _ = _head_lanes(0)
    return jnp.where(own0, outs[0], outs[1])


def _ctx_attn_kernel(x_ref, mod_ref, g_ref, w_ref, *rest, nb, seq, n_side):
    side_in, (o_ref, sk_ref, sv_ref), side_out, (qkv_scr,) = (
        rest[:n_side], rest[n_side:n_side + 3], rest[n_side + 3:2 * n_side + 3], rest[2 * n_side + 3:])
    _side_cast(side_in, side_out)
    mod = mod_ref[0]
    h = _rms_mod(x_ref[...], g_ref[...], _mod_chunk(mod, 0), _mod_chunk(mod, 1)).astype(BF16)
    qkv_scr[...] = jnp.dot(h, w_ref[...], preferred_element_type=F32)
    for b in range(nb):
        rows = slice(b * seq, (b + 1) * seq)
        for hp in range(N_HEADS // HEADS_PER_BLOCK):
            heads = slice(hp * HEADS_PER_BLOCK, (hp + 1) * HEADS_PER_BLOCK)
            for part, s_ref in ((1, sk_ref), (2, sv_ref)):
                cols = slice(part * D_MODEL + hp * LANES, part * D_MODEL + (hp + 1) * LANES)
                s_ref[b, 0, heads] = qkv_scr[rows, cols].T.reshape(HEADS_PER_BLOCK, HEAD_DIM, seq)

    def block(unit, part):
        b, hp = unit
        return (slice(b * seq, (b + 1) * seq),
                slice(part * D_MODEL + hp * LANES, part * D_MODEL + (hp + 1) * LANES))

    def scores(unit):
        q2 = (qkv_scr[block(unit, 0)] * ATTN_SCALE).astype(BF16)
        k2 = qkv_scr[block(unit, 1)].astype(BF16)
        return [lax.dot_general(_head_q(q2, e), k2, (((1,), (1,)), ((), ())),
                                preferred_element_type=F32) for e in range(HEADS_PER_BLOCK)]

    units = [(b, hp) for b in range(nb) for hp in range(N_HEADS // HEADS_PER_BLOCK)]
    for unit, per_head in _pipelined(units, scores):
        v2 = qkv_scr[block(unit, 2)].astype(BF16)
        o2 = None
        for e, s in enumerate(per_head):
            own, _ = _head_lanes(e)
            ve = jnp.where(own, v2, jnp.zeros_like(v2))
            p = jnp.exp(s - jnp.max(s, axis=-1, keepdims=True))
            oe = (jnp.dot(p.astype(BF16), ve, preferred_element_type=F32)
                  / jnp.sum(p, axis=-1, keepdims=True))
            o2 = oe if o2 is None else o2 + oe
        o_ref[block(unit, 0)] = o2.astype(BF16)


def _ctx_attn(x, mods3, mod_row, g, w_qkv, side, *, batch, seq):
    n, d = x.shape
    nb = 2
    tm = nb * seq
    side_in, side_out, side_shapes = _side_cast_specs(side, n // tm)
    kern = functools.partial(_ctx_attn_kernel, nb=nb, seq=seq, n_side=len(side))
    state = jax.ShapeDtypeStruct((batch, 1, N_HEADS, HEAD_DIM, seq), F32)
    state_spec = pl.BlockSpec((nb, 1, N_HEADS, HEAD_DIM, seq), lambda i: (i, 0, 0, 0, 0))
    return pl.pallas_call(
        kern,
        grid=(n // tm,),
        in_specs=[
            pl.BlockSpec((tm, d), lambda i: (i, 0)),
            pl.BlockSpec((1, 1, 6 * d), lambda i: (mod_row, 0, 0)),
            _resident((1, d)),
            _resident((d, 3 * d)),
        ] + side_in,
        out_specs=[pl.BlockSpec((tm, d), lambda i: (i, 0)), state_spec, state_spec] + side_out,
        out_shape=[jax.ShapeDtypeStruct((n, d), BF16), state, state] + side_shapes,
        scratch_shapes=[pltpu.VMEM((tm, 3 * d), F32)],
        compiler_params=_params(),
        name="ctx_attn",
    )(x, mods3, g, w_qkv, *(w for w, _ in side))


def _qkv_kernel(x_ref, mod_ref, g_ref, w_ref, o_ref, *, tn):
    mod = mod_ref[0]
    h = _rms_mod(x_ref[...], g_ref[...], _mod_chunk(mod, 0), _mod_chunk(mod, 1)).astype(BF16)
    for c in range(w_ref.shape[1] // tn):
        cols = slice(c * tn, (c + 1) * tn)
        o_ref[:, cols] = jnp.dot(h, w_ref[:, cols], preferred_element_type=F32).astype(BF16)


def _qkv(x, mods3, mod_row_fn, g, w_qkv, *, tm):
    n, d = x.shape
    nout = w_qkv.shape[1]
    return pl.pallas_call(
        functools.partial(_qkv_kernel, tn=512),
        grid=(n // tm,),
        in_specs=[
            pl.BlockSpec((tm, d), lambda i: (i, 0)),
            pl.BlockSpec((1, 1, 6 * d), lambda i: (mod_row_fn(i), 0, 0)),
            _resident((1, d)),
            _resident((d, nout)),
        ],
        out_specs=pl.BlockSpec((tm, nout), lambda i: (i, 0)),
        out_shape=jax.ShapeDtypeStruct((n, nout), BF16),
        compiler_params=_params(),
        name="qkv",
    )(x, mods3, g, w_qkv)


def _na_groups(rows):
    kh = min(NA_ROWS, rows)
    row_start = [min(max(r - kh // 2, 0), rows - kh) for r in range(rows)]
    groups, off = [], 0
    for r0 in range(0, rows, NA_QROWS):
        lo = min(row_start[r0:r0 + NA_QROWS])
        hi = max(row_start[r0:r0 + NA_QROWS]) + kh
        if (hi - lo) * GRID_W % LANES:
            if hi < rows:
                hi += 1
            else:
                lo -= 1
        nk = (hi - lo) * GRID_W
        groups.append((r0 * GRID_W, lo * GRID_W, nk, off))
        off += nk
    return tuple(groups), row_start, kh


def _na_bias_plan(rows):
    groups, row_start, kh = _na_groups(rows)
    nro = 2 * NA_ROWS - 1
    w = GRID_W
    assert 2 * w == LANES
    plan = []
    for (q0, k0, nk, off) in groups:
        for qi in range(NA_QROWS):
            qr = q0 // w + qi
            rs = row_start[qr]
            tile_of = lambda kr: kr - qr + NA_ROWS - 1 if rs <= kr < rs + kh else nro
            for p in range(nk // LANES):
                kr = k0 // w + 2 * p
                plan.append((qi * w, off + p * LANES, tile_of(kr), tile_of(kr + 1)))
    return tuple(plan), sum(g[2] for g in groups)


def _pad_rpb(rpb):
    lead = GRID_W - NA_COLS
    return jnp.pad(rpb, ((0, 0), (0, 1), (lead, LANES - rpb.shape[2] - lead)))


def _fill_na_bias(g_ref, o_ref, lo_scr, hi_scr, plan):
    w = GRID_W
    n_tiles = 2 * NA_ROWS - 1
    lane = lax.broadcasted_iota(jnp.int32, (w, LANES), 1)
    qc = lax.broadcasted_iota(jnp.int32, (w, LANES), 0)
    kc = lane % w
    col_start = jnp.clip(qc - NA_COLS // 2, 0, w - NA_COLS)
    col_ok = (kc >= col_start) & (kc < col_start + NA_COLS)
    in_lo = lane < w
    neg = jnp.full((w, LANES), NEG_BIAS, F32)
    for a in range(n_tiles):
        row = jnp.broadcast_to(g_ref[a:a + 1, :], (w, LANES))
        lo_scr[a] = jnp.where(col_ok, pltpu.roll(row, w + 1, 1, stride=1, stride_axis=0), neg)
        hi_scr[a] = jnp.where(col_ok, pltpu.roll(row, 1, 1, stride=1, stride_axis=0), neg)
    lo_scr[n_tiles] = neg
    hi_scr[n_tiles] = neg
    for (r0, c0, a_lo, a_hi) in plan:
        o_ref[r0:r0 + w, c0:c0 + LANES] = jnp.where(in_lo, lo_scr[a_lo], hi_scr[a_hi])


def _nattn_kernel(q_ref, k_ref, v_ref, ck_ref, cv_ref, g_ref, o_ref, bias_ref, lo_scr, hi_scr,
                  *, groups, plan):
    @pl.when(pl.program_id(1) == 0)
    def _():
        for e in range(HEADS_PER_BLOCK):
            _fill_na_bias(g_ref.at[e], bias_ref.at[e], lo_scr, hi_scr, plan)

    q2, k2, v2 = q_ref[0], k_ref[0], v_ref[0]
    past = ck_ref.shape[-1]
    kct = ck_ref[0, 0].reshape(LANES, past).astype(BF16)
    vct = cv_ref[0, 0].reshape(LANES, past).astype(BF16)
    q2 = q2 * ATTN_SCALE
    nq = NA_QROWS * GRID_W
    dn = (((1,), (1,)), ((), ()))
    units = [(grp, e) for grp in groups for e in range(HEADS_PER_BLOCK)]

    def scores(unit):
        (q0, k0, nk, boff), e = unit
        qe = _head_q(q2[q0:q0 + nq], e)
        s_w = (lax.dot_general(qe, k2[k0:k0 + nk], dn, preferred_element_type=F32)
               + bias_ref[e, :, boff:boff + nk])
        return s_w, jnp.dot(qe, kct, preferred_element_type=F32)

    def attend(unit, s_w, s_c):
        (_, k0, nk, _), e = unit
        mx = jnp.maximum(jnp.max(s_w, axis=-1, keepdims=True),
                         jnp.max(s_c, axis=-1, keepdims=True))
        return (jnp.dot(jnp.exp(s_w - mx).astype(BF16), _head_v(v2[k0:k0 + nk], e),
                        preferred_element_type=F32)
                + lax.dot_general(jnp.exp(s_c - mx).astype(BF16), _head_v(vct, e, axis=0), dn,
                                  preferred_element_type=F32))

    accs = []
    for unit, s in _pipelined(units, scores):
        accs.append(attend(unit, *s))
        if len(accs) == HEADS_PER_BLOCK:
            q0 = unit[0][0]
            o_ref[0, q0:q0 + nq, :] = _normalise(accs).astype(BF16)
            accs = []


def _nattn(qkv, cache_kt, cache_vt, rpb, *, layer):
    b, t, _ = qkv.shape
    past = cache_kt.shape[4]
    groups, _, _ = _na_groups(t // GRID_W)
    plan, nbias = _na_bias_plan(t // GRID_W)
    nhp = N_HEADS // HEADS_PER_BLOCK
    nq = NA_QROWS * GRID_W
    n_tiles = 2 * NA_ROWS
    cache_spec = pl.BlockSpec((1, 1, HEADS_PER_BLOCK, HEAD_DIM, past),
                              lambda hp, bi: (bi, layer, hp, 0, 0))
    return pl.pallas_call(
        functools.partial(_nattn_kernel, groups=groups, plan=plan),
        grid=(nhp, b),
        in_specs=[
            pl.BlockSpec((1, t, LANES), lambda hp, bi: (bi, 0, hp)),
            pl.BlockSpec((1, t, LANES), lambda hp, bi: (bi, 0, nhp + hp)),
            pl.BlockSpec((1, t, LANES), lambda hp, bi: (bi, 0, 2 * nhp + hp)),
            cache_spec,
            cache_spec,
            pl.BlockSpec((HEADS_PER_BLOCK, n_tiles, LANES), lambda hp, bi: (hp, 0, 0)),
        ],
        out_specs=pl.BlockSpec((1, t, LANES), lambda hp, bi: (bi, 0, hp)),
        out_shape=jax.ShapeDtypeStruct((b, t, D_MODEL), BF16),
        scratch_shapes=[pltpu.VMEM((HEADS_PER_BLOCK, nq, nbias), F32),
                        pltpu.VMEM((n_tiles, GRID_W, LANES), F32),
                        pltpu.VMEM((n_tiles, GRID_W, LANES), F32)],
        compiler_params=_params(2),
        name="nattn",
    )(qkv, qkv, qkv, cache_kt, cache_vt, _pad_rpb(rpb))


GELU_C = float(np.float32(np.sqrt(2 / np.pi)))
GELU_K = -2.0 * GELU_C * math.log2(math.e)


def _gelu(x):
    return x / (1.0 + jnp.exp2(x * (x * x * (GELU_K * 0.044715) + GELU_K)))


def _gmlp_kernel(x_ref, mod_ref, g_ref, win_ref, gv_ref, ws_ref, bs_ref, *rest, tm, n_side):
    side_in, (o_ref,), side_out, (vv_scr,) = (
        rest[:n_side], rest[n_side:n_side + 1], rest[n_side + 1:2 * n_side + 1], rest[2 * n_side + 1:])
    _side_cast(side_in, side_out)
    e = GMLP_WIDTH
    gd = e // GMLP_GROUPS
    mod = mod_ref[0]
    h = _rms_mod(x_ref[...], g_ref[...], _mod_chunk(mod, 0), _mod_chunk(mod, 1)).astype(BF16)
    pair = 2 * gd

    def u_proj(gp):
        return jnp.dot(h, win_ref[:, gp * pair:(gp + 1) * pair], preferred_element_type=F32)

    zv = _gelu(jnp.dot(h, win_ref[:, e:2 * e], preferred_element_type=F32))
    pipe = _pipelined(list(range(e // pair)), u_proj)
    head = next(pipe)
    xc = zv - jnp.mean(zv, axis=-1, keepdims=True)
    vv = xc * lax.rsqrt(jnp.mean(xc * xc, axis=-1, keepdims=True) + EPS) * gv_ref[...]
    vv_scr[...] = vv.astype(BF16)
    for gp, u_raw in itertools.chain([head], pipe):
        u2 = _gelu(u_raw)
        for gi in range(2):
            g = 2 * gp + gi
            cols = slice(g * gd, (g + 1) * gd)
            nc = tm // CHUNK
            vcat = jnp.concatenate(
                [vv_scr[c * CHUNK:(c + 1) * CHUNK, cols] for c in range(nc)], axis=1)
            mixed = jnp.dot(ws_ref[g].astype(BF16), vcat, preferred_element_type=F32)
            for c in range(nc):
                rows = slice(c * CHUNK, (c + 1) * CHUNK)
                o_ref[rows, cols] = (u2[rows, gi * gd:(gi + 1) * gd]
                                     * (mixed[:, c * gd:(c + 1) * gd] + bs_ref[g])).astype(BF16)


def _gmlp(x, mods3, mod_row_fn, g, w_in, g_v, w_s, b_s_full, side=(), *, tm):
    n, d = x.shape
    e = GMLP_WIDTH
    side_in, side_out, side_shapes = _side_cast_specs(side, n // tm)
    return pl.pallas_call(
        functools.partial(_gmlp_kernel, tm=tm, n_side=len(side)),
        grid=(n // tm,),
        in_specs=[
            pl.BlockSpec((tm, d), lambda i: (i, 0)),
            pl.BlockSpec((1, 1, 6 * d), lambda i: (mod_row_fn(i), 0, 0)),
            _resident((1, d)),
            _resident((d, 2 * e)),
            _resident((1, e)),
            _resident(w_s.shape),
            _resident(b_s_full.shape),
        ] + side_in,
        out_specs=[pl.BlockSpec((tm, e), lambda i: (i, 0))] + side_out,
        out_shape=[jax.ShapeDtypeStruct((n, e), BF16)] + side_shapes,
        scratch_shapes=[pltpu.VMEM((tm, e), BF16)],
        compiler_params=_params(),
        name="gmlp",
    )(x, mods3, g, w_in, g_v, w_s, b_s_full, *(w for w, _ in side))


def _conv_pieces(tm, seq, halo):
    if halo:
        return ((0, tm + 2 * HALO, HALO, tm),)
    return tuple((s * seq, seq, 0, seq) for s in range(tm // seq))


def _mix_ffn_kernel(x_ref, xp_ref, xn_ref, m_ref, mp_ref, mn_ref, mod_ref, gffn_ref, gfin_ref,
                    wmo_ref, wup_ref, wconv_ref, bconv_ref, wdown_ref, *rest,
                    tm, seq, halo, final, n_side):
    side_in, (o_ref,), side_out, (act_scr,) = (
        rest[:n_side], rest[n_side:n_side + 1], rest[n_side + 1:2 * n_side + 1], rest[2 * n_side + 1:])
    _side_cast(side_in, side_out)
    mod = mod_ref[0]
    gate_mix, shift, scale, gate_ffn = (_mod_chunk(mod, k) for k in (2, 3, 4, 5))
    if halo:
        xcat = jnp.concatenate([xp_ref[0], x_ref[...], xn_ref[0]], axis=0)
        mcat = jnp.concatenate([mp_ref[0], m_ref[...], mn_ref[0]], axis=0)
        main = slice(HALO, HALO + tm)
    else:
        xcat, mcat = x_ref[...], m_ref[...]
        main = slice(0, tm)
    r = xcat.shape[0]
    x1 = xcat + gate_mix * jnp.dot(mcat, wmo_ref[...], preferred_element_type=F32)
    hcat = _rms_mod(x1, gffn_ref[...], shift, scale).astype(BF16)

    f = FFN_TILE
    if halo:
        blk = pl.program_id(0) % (seq // tm)
        hcat = jnp.concatenate([
            jnp.where(blk == 0, jnp.zeros((HALO, D_MODEL), BF16), hcat[:HALO]),
            hcat[main],
            jnp.where(blk == seq // tm - 1, jnp.zeros((HALO, D_MODEL), BF16), hcat[HALO + tm:]),
        ], axis=0)
    pieces = _conv_pieces(tm, seq, halo)
    sub = lax.broadcasted_iota(jnp.int32, (PAD, 1), 0)

    def up(c):
        return [jnp.dot(hcat, wup_ref[:, col:col + f], preferred_element_type=F32)
                for col in (c * f, FFN_DIM + c * f)]

    def conv(a, col):
        cols = slice(col, col + f)
        wc = wconv_ref[0, :, cols]
        out = []
        for (a0, n_rows, own0, own) in pieces:
            piece = a[a0:a0 + n_rows]
            own_rows = slice(own0, own0 + own)
            prev = pltpu.roll(piece, 1, 0)[own_rows]
            nxt = pltpu.roll(piece, n_rows - 1, 0)[own_rows]
            if not halo:
                prev = jnp.concatenate([jnp.where(sub == 0, 0.0, prev[:PAD]), prev[PAD:]], axis=0)
                nxt = jnp.concatenate([nxt[:-PAD], jnp.where(sub == PAD - 1, 0.0, nxt[-PAD:])], axis=0)
            out.append(prev * wc[0:1] + piece[own_rows] * wc[1:2] + nxt * wc[2:3]
                       + bconv_ref[0, :, cols])
        return out[0] if len(out) == 1 else jnp.concatenate(out, axis=0)

    for c, (a_gate, a_val) in _pipelined(list(range(FFN_DIM // f)), up):
        act = jax.nn.silu(conv(a_gate, c * f)) * conv(a_val, FFN_DIM + c * f)
        act_scr[:, c * f:(c + 1) * f] = act.astype(BF16)
    acc = jnp.dot(act_scr[...], wdown_ref[...], preferred_element_type=F32)
    out = x1[main] + gate_ffn * acc
    if final:
        out = out * lax.rsqrt(jnp.mean(out * out, axis=-1, keepdims=True) + EPS) * gfin_ref[...]
    o_ref[...] = out


def _mix_ffn(x, m, mods3, mod_row_fn, g_ffn, g_fin, w_mo, w_up, w_conv, b_conv, w_down,
             side=(), *, layer, tm, seq, final):
    n, d = x.shape
    km = m.shape[1]
    halo = tm % seq != 0
    assert not halo or seq % tm == 0
    nh = n // HALO
    per = tm // HALO
    prev_map = lambda i: (jnp.maximum(i * per - 1, 0), 0, 0)
    next_map = lambda i: (jnp.minimum((i + 1) * per, nh - 1), 0, 0)
    side_in, side_out, side_shapes = _side_cast_specs(side, n // tm)
    kern = functools.partial(_mix_ffn_kernel, tm=tm, seq=seq, halo=halo, final=final,
                             n_side=len(side))
    return pl.pallas_call(
        kern,
        grid=(n // tm,),
        in_specs=[
            pl.BlockSpec((tm, d), lambda i: (i, 0)),
            pl.BlockSpec((1, HALO, d), prev_map),
            pl.BlockSpec((1, HALO, d), next_map),
            pl.BlockSpec((tm, km), lambda i: (i, 0)),
            pl.BlockSpec((1, HALO, km), prev_map),
            pl.BlockSpec((1, HALO, km), next_map),
            pl.BlockSpec((1, 1, 6 * d), lambda i: (mod_row_fn(i), 0, 0)),
            _resident((1, d)),
            _resident((1, d)),
            _resident(w_mo.shape),
            _resident(w_up.shape),
            _resident_layer(w_conv.shape, layer),
            _resident_layer(b_conv.shape, layer),
            _resident(w_down.shape),
        ] + side_in,
        out_specs=[pl.BlockSpec((tm, d), lambda i: (i, 0))] + side_out,
        out_shape=[jax.ShapeDtypeStruct((n, d), F32)] + side_shapes,
        scratch_shapes=[pltpu.VMEM((tm, FFN_DIM), BF16)],
        compiler_params=_params(),
        name="mix_ffn",
    )(x, x.reshape(nh, HALO, d), x.reshape(nh, HALO, d),
      m, m.reshape(nh, HALO, km), m.reshape(nh, HALO, km),
      mods3, g_ffn, g_fin, w_mo, w_up, w_conv, b_conv, w_down, *(w for w, _ in side))


def kernel(x_prompt, x_sample, cache_k, cache_v, c, c_ctx, w_ada, b_ada, norm_mix_g,
           norm_ffn_g, norm_final_g, w_qkv, w_attn_out, rpb, w_gmlp_in, g_gmlp_v,
           w_spatial, b_spatial, w_gmlp_out, w_ffn_up, w_ffn_conv, b_ffn_conv, w_ffn_down):
    batch, seq, d = x_prompt.shape
    dec_batch, dec_seq, _ = x_sample.shape
    depth = w_ada.shape[0]
    assert depth == 2 and d == D_MODEL

    cond = jnp.zeros((MOD_ROWS, d), F32).at[0].set(c_ctx).at[1:1 + dec_batch].set(c)
    mods3 = _ada(cond, w_ada, b_ada).reshape(depth * MOD_ROWS, 1, 6 * d)

    w_qkv_b = _to_bf16(w_qkv[0])
    w_sp = w_spatial[0]
    b_s_full = jnp.broadcast_to(b_spatial[0][:, :, None], (GMLP_GROUPS, CHUNK, CHUNK))
    conv_w = (w_ffn_conv, b_ffn_conv[:, None, :])
    g_mix = norm_mix_g.reshape(depth, 1, d)
    g_ffn = norm_ffn_g.reshape(depth, 1, d)
    g_fin = norm_final_g.reshape(1, d)
    tm = 512

    def ctx_row(layer):
        return lambda i: layer * MOD_ROWS
    x = x_prompt.reshape(batch * seq, d)
    o, state_kt, state_vt, w_ao, w_up0, w_down0 = _ctx_attn(
        x, mods3, 0, g_mix[0], w_qkv_b, [(w_attn_out, 0), (w_ffn_up, 0), (w_ffn_down, 0)],
        batch=batch, seq=seq)
    state_k, state_v = jnp.swapaxes(state_kt, 3, 4), jnp.swapaxes(state_vt, 3, 4)
    x, w_gin = _mix_ffn(x, o, mods3, ctx_row(0), g_ffn[0], g_fin, w_ao, w_up0, *conv_w, w_down0,
                        [(w_gmlp_in, 0)], layer=0, tm=tm, seq=seq, final=False)
    m, w_gout, w_up1, w_down1 = _gmlp(
        x, mods3, ctx_row(1), g_mix[1], w_gin, g_gmlp_v[0:1], w_sp, b_s_full,
        [(w_gmlp_out, 0), (w_ffn_up, 1), (w_ffn_down, 1)], tm=tm)
    y_prompt, = _mix_ffn(x, m, mods3, ctx_row(1), g_ffn[1], g_fin, w_gout, w_up1, *conv_w, w_down1,
                         layer=1, tm=tm, seq=seq, final=True)

    def lat_row(layer):
        return lambda i: layer * MOD_ROWS + 1 + (i * tm) // dec_seq
    x = x_sample.reshape(dec_batch * dec_seq, d)
    qkv = _qkv(x, mods3, lat_row(0), g_mix[0], w_qkv_b, tm=tm)
    o = _nattn(qkv.reshape(dec_batch, dec_seq, 3 * d), jnp.swapaxes(cache_k, 3, 4),
               jnp.swapaxes(cache_v, 3, 4), rpb[0], layer=0)
    x, = _mix_ffn(x, o.reshape(dec_batch * dec_seq, d), mods3, lat_row(0), g_ffn[0], g_fin,
                  w_ao, w_up0, *conv_w, w_down0, layer=0, tm=tm, seq=dec_seq, final=False)
    m, = _gmlp(x, mods3, lat_row(1), g_mix[1], w_gin, g_gmlp_v[0:1], w_sp, b_s_full, tm=tm)
    y_sample, = _mix_ffn(x, m, mods3, lat_row(1), g_ffn[1], g_fin, w_gout, w_up1, *conv_w, w_down1,
                         layer=1, tm=tm, seq=dec_seq, final=True)

    return (y_prompt.reshape(batch, seq, d), y_sample.reshape(dec_batch, dec_seq, d),
            state_k, state_v)
```

```python
import functools
import itertools
import math

import numpy as np
import jax
import jax.numpy as jnp
from jax import lax
from jax.experimental import pallas as pl
from jax.experimental.pallas import tpu as pltpu

D_MODEL = 1024
N_HEADS = 16
HEAD_DIM = D_MODEL // N_HEADS
GRID_W = 64
NA_ROWS = 8
NA_COLS = 16
CHUNK = 128
GMLP_WIDTH = 2 * D_MODEL
GMLP_GROUPS = 16
FFN_DIM = 2816
CONV_W = 3
EPS = 1e-6
ATTN_SCALE = HEAD_DIM ** -0.5

LANES = 128
HEADS_PER_BLOCK = LANES // HEAD_DIM
MOD_ROWS = 8
HALO = 16
PAD = 8
FFN_TILE = 256
NA_QROWS = 4
LOOKAHEAD = 1
NEG_BIAS = -1e30
VMEM_LIMIT = 56 * 1024 * 1024

BF16 = jnp.bfloat16
F32 = jnp.float32


def _params(n_axes=1):
    return pltpu.CompilerParams(
        dimension_semantics=("arbitrary",) * n_axes, vmem_limit_bytes=VMEM_LIMIT)


def _resident(shape):
    nd = len(shape)
    return pl.BlockSpec(shape, lambda *_: (0,) * nd, pipeline_mode=pl.Buffered(1))


def _resident_layer(shape, layer):
    nd = len(shape)
    return pl.BlockSpec((1,) + tuple(shape[1:]), lambda *_: (layer,) + (0,) * (nd - 1),
                        pipeline_mode=pl.Buffered(1))


def _pipelined(items, first_stage):
    ready = [first_stage(it) for it in items[:LOOKAHEAD]]
    for i, item in enumerate(items):
        if i + LOOKAHEAD < len(items):
            ready.append(first_stage(items[i + LOOKAHEAD]))
        yield item, ready.pop(0)


def _side_cast_specs(weights, n_steps):
    ins, outs, shapes = [], [], []
    for w, layer in weights:
        _, rows, cols = w.shape
        assert rows % (n_steps * HALO) == 0
        ins.append(pl.BlockSpec((1, rows // n_steps, cols), lambda i, layer=layer: (layer, i, 0)))
        outs.append(pl.BlockSpec((rows // n_steps, cols), lambda i: (i, 0)))
        shapes.append(jax.ShapeDtypeStruct((rows, cols), BF16))
    return ins, outs, shapes


def _side_cast(src_refs, dst_refs):
    for src, dst in zip(src_refs, dst_refs):
        dst[...] = src[0].astype(BF16)


def _rms_mod(x, g, shift, scale):
    y = x * lax.rsqrt(jnp.mean(x * x, axis=-1, keepdims=True) + EPS)
    return (y * g) * (1 + scale) + shift


def _mod_chunk(mod, k):
    return mod[:, k * D_MODEL:(k + 1) * D_MODEL]


CAST_BLOCK_BYTES = 8 * 1024 * 1024


def _cast_kernel(w_ref, o_ref):
    o_ref[...] = w_ref[...].astype(o_ref.dtype)


def _to_bf16(w):
    shape = w.shape
    cols = shape[-1]
    rows = w.size // cols
    per = 16
    assert rows % per == 0
    blk = max(b for b in range(per, rows + 1, per)
              if rows % b == 0 and (b * cols * 4 <= CAST_BLOCK_BYTES or b == per))
    out = pl.pallas_call(
        _cast_kernel,
        grid=(rows // blk,),
        in_specs=[pl.BlockSpec((blk, cols), lambda i: (i, 0))],
        out_specs=pl.BlockSpec((blk, cols), lambda i: (i, 0)),
        out_shape=jax.ShapeDtypeStruct((rows, cols), BF16),
        compiler_params=_params(),
        name="to_bf16",
    )(w.reshape(rows, cols))
    return out.reshape(shape)


def _ada_kernel(cond_ref, w_ref, b_ref, o_ref):
    s = jax.nn.silu(cond_ref[...]).astype(BF16)
    o_ref[0] = jnp.dot(s, w_ref[0].astype(BF16), preferred_element_type=F32) + b_ref[0]


def _ada(cond, w_ada, b_ada):
    depth, d, n = w_ada.shape
    tn = 1536
    return pl.pallas_call(
        _ada_kernel,
        grid=(depth, n // tn),
        in_specs=[
            pl.BlockSpec((MOD_ROWS, d), lambda l, j: (0, 0)),
            pl.BlockSpec((1, d, tn), lambda l, j: (l, 0, j)),
            pl.BlockSpec((1, 1, tn), lambda l, j: (l, 0, j)),
        ],
        out_specs=pl.BlockSpec((1, MOD_ROWS, tn), lambda l, j: (l, 0, j)),
        out_shape=jax.ShapeDtypeStruct((depth, MOD_ROWS, n), F32),
        compiler_params=_params(2),
        name="ada",
    )(cond, w_ada, b_ada.reshape(depth, 1, n))


def _den_lane(e):
    return (1 - e) * HEAD_DIM


def _head_lanes(e, axis=1):
    ch = lax.broadcasted_iota(jnp.int32, (1, LANES) if axis == 1 else (LANES, 1), axis)
    own = (ch < HEAD_DIM) if e == 0 else (ch >= HEAD_DIM)
    return own, ch == _den_lane(e)


def _head_q(q2, e):
    own, _ = _head_lanes(e)
    return jnp.where(own, q2, jnp.zeros_like(q2))


def _head_v(v2, e, axis=1):
    own, is_den = _head_lanes(e, axis)
    return jnp.where(own, v2, jnp.broadcast_to(is_den.astype(v2.dtype), v2.shape))


def _normalise(accs):
    outs = [acc / acc[:, _den_lane(e):_den_lane(e) + 1] for e, acc in enumerate(accs)]
    own0, _ = _head_lanes(0)
    return jnp.where(own0, outs[0], outs[1])


def _ctx_attn_kernel(x_ref, mod_ref, g_ref, w_ref, *rest, nb, seq, n_side):
    side_in, (o_ref, sk_ref, sv_ref), side_out, (qkv_scr,) = (
        rest[:n_side], rest[n_side:n_side + 3], rest[n_side + 3:2 * n_side + 3], rest[2 * n_side + 3:])
    _side_cast(side_in, side_out)
    mod = mod_ref[0]
    h = _rms_mod(x_ref[...], g_ref[...], _mod_chunk(mod, 0), _mod_chunk(mod, 1)).astype(BF16)
    qkv_scr[...] = jnp.dot(h, w_ref[...], preferred_element_type=F32)
    for b in range(nb):
        rows = slice(b * seq, (b + 1) * seq)
        for hp in range(N_HEADS // HEADS_PER_BLOCK):
            heads = slice(hp * HEADS_PER_BLOCK, (hp + 1) * HEADS_PER_BLOCK)
            for part, s_ref in ((1, sk_ref), (2, sv_ref)):
                cols = slice(part * D_MODEL + hp * LANES, part * D_MODEL + (hp + 1) * LANES)
                s_ref[b, 0, heads] = qkv_scr[rows, cols].T.reshape(HEADS_PER_BLOCK, HEAD_DIM, seq)

    def block(unit, part):
        b, hp = unit
        return (slice(b * seq, (b + 1) * seq),
                slice(part * D_MODEL + hp * LANES, part * D_MODEL + (hp + 1) * LANES))

    def scores(unit):
        q2 = (qkv_scr[block(unit, 0)] * ATTN_SCALE).astype(BF16)
        k2 = qkv_scr[block(unit, 1)].astype(BF16)
        qs = jnp.concatenate([_head_q(q2, e) for e in range(HEADS_PER_BLOCK)], axis=0)
        return lax.dot_general(qs, k2, (((1,), (1,)), ((), ())), preferred_element_type=F32)

    units = [(b, hp) for b in range(nb) for hp in range(N_HEADS // HEADS_PER_BLOCK)]
    for unit, s in _pipelined(units, scores):
        v2 = qkv_scr[block(unit, 2)].astype(BF16)
        p = jnp.exp(s - jnp.max(s, axis=-1, keepdims=True))
        o = (jnp.dot(p.astype(BF16), v2, preferred_element_type=F32)
             / jnp.sum(p, axis=-1, keepdims=True))
        own0, _ = _head_lanes(0)
        o_ref[block(unit, 0)] = jnp.where(own0, o[:seq], o[seq:]).astype(BF16)


def _ctx_attn(x, mods3, mod_row, g, w_qkv, side, *, batch, seq):
    n, d = x.shape
    nb = 2
    tm = nb * seq
    side_in, side_out, side_shapes = _side_cast_specs(side, n // tm)
    kern = functools.partial(_ctx_attn_kernel, nb=nb, seq=seq, n_side=len(side))
    state = jax.ShapeDtypeStruct((batch, 1, N_HEADS, HEAD_DIM, seq), F32)
    state_spec = pl.BlockSpec((nb, 1, N_HEADS, HEAD_DIM, seq), lambda i: (i, 0, 0, 0, 0))
    return pl.pallas_call(
        kern,
        grid=(n // tm,),
        in_specs=[
            pl.BlockSpec((tm, d), lambda i: (i, 0)),
            pl.BlockSpec((1, 1, 6 * d), lambda i: (mod_row, 0, 0)),
            _resident((1, d)),
            _resident((d, 3 * d)),
        ] + side_in,
        out_specs=[pl.BlockSpec((tm, d), lambda i: (i, 0)), state_spec, state_spec] + side_out,
        out_shape=[jax.ShapeDtypeStruct((n, d), BF16), state, state] + side_shapes,
        scratch_shapes=[pltpu.VMEM((tm, 3 * d), F32)],
        compiler_params=_params(),
        name="ctx_attn",
    )(x, mods3, g, w_qkv, *(w for w, _ in side))


def _qkv_kernel(x_ref, mod_ref, g_ref, w_ref, o_ref, *, tn):
    mod = mod_ref[0]
    h = _rms_mod(x_ref[...], g_ref[...], _mod_chunk(mod, 0), _mod_chunk(mod, 1)).astype(BF16)
    for c in range(w_ref.shape[1] // tn):
        cols = slice(c * tn, (c + 1) * tn)
        o_ref[:, cols] = jnp.dot(h, w_ref[:, cols], preferred_element_type=F32).astype(BF16)


def _qkv(x, mods3, mod_row_fn, g, w_qkv, *, tm):
    n, d = x.shape
    nout = w_qkv.shape[1]
    return pl.pallas_call(
        functools.partial(_qkv_kernel, tn=512),
        grid=(n // tm,),
        in_specs=[
            pl.BlockSpec((tm, d), lambda i: (i, 0)),
            pl.BlockSpec((1, 1, 6 * d), lambda i: (mod_row_fn(i), 0, 0)),
            _resident((1, d)),
            _resident((d, nout)),
        ],
        out_specs=pl.BlockSpec((tm, nout), lambda i: (i, 0)),
        out_shape=jax.ShapeDtypeStruct((n, nout), BF16),
        compiler_params=_params(),
        name="qkv",
    )(x, mods3, g, w_qkv)


def _na_groups(rows):
    kh = min(NA_ROWS, rows)
    row_start = [min(max(r - kh // 2, 0), rows - kh) for r in range(rows)]
    groups, off = [], 0
    for r0 in range(0, rows, NA_QROWS):
        lo = min(row_start[r0:r0 + NA_QROWS])
        hi = max(row_start[r0:r0 + NA_QROWS]) + kh
        if (hi - lo) * GRID_W % LANES:
            if hi < rows:
                hi += 1
            else:
                lo -= 1
        nk = (hi - lo) * GRID_W
        groups.append((r0 * GRID_W, lo * GRID_W, nk, off))
        off += nk
    return tuple(groups), row_start, kh


def _na_bias_plan(rows):
    groups, row_start, kh = _na_groups(rows)
    nro = 2 * NA_ROWS - 1
    w = GRID_W
    assert 2 * w == LANES
    plan = []
    for (q0, k0, nk, off) in groups:
        for qi in range(NA_QROWS):
            qr = q0 // w + qi
            rs = row_start[qr]
            tile_of = lambda kr: kr - qr + NA_ROWS - 1 if rs <= kr < rs + kh else nro
            for p in range(nk // LANES):
                kr = k0 // w + 2 * p
                plan.append((qi * w, off + p * LANES, tile_of(kr), tile_of(kr + 1)))
    return tuple(plan), sum(g[2] for g in groups)


def _pad_rpb(rpb):
    lead = GRID_W - NA_COLS
    return jnp.pad(rpb, ((0, 0), (0, 1), (lead, LANES - rpb.shape[2] - lead)))


def _fill_na_bias(g_ref, o_ref, lo_scr, hi_scr, plan):
    w = GRID_W
    n_tiles = 2 * NA_ROWS - 1
    lane = lax.broadcasted_iota(jnp.int32, (w, LANES), 1)
    qc = lax.broadcasted_iota(jnp.int32, (w, LANES), 0)
    kc = lane % w
    col_start = jnp.clip(qc - NA_COLS // 2, 0, w - NA_COLS)
    col_ok = (kc >= col_start) & (kc < col_start + NA_COLS)
    in_lo = lane < w
    neg = jnp.full((w, LANES), NEG_BIAS, F32)
    for a in range(n_tiles):
        row = jnp.broadcast_to(g_ref[a:a + 1, :], (w, LANES))
        lo_scr[a] = jnp.where(col_ok, pltpu.roll(row, w + 1, 1, stride=1, stride_axis=0), neg)
        hi_scr[a] = jnp.where(col_ok, pltpu.roll(row, 1, 1, stride=1, stride_axis=0), neg)
    lo_scr[n_tiles] = neg
    hi_scr[n_tiles] = neg
    for (r0, c0, a_lo, a_hi) in plan:
        o_ref[r0:r0 + w, c0:c0 + LANES] = jnp.where(in_lo, lo_scr[a_lo], hi_scr[a_hi])


def _nattn_kernel(q_ref, k_ref, v_ref, ck_ref, cv_ref, g_ref, o_ref, bias_ref, lo_scr, hi_scr,
                  *, groups, plan):
    @pl.when(pl.program_id(1) == 0)
    def _():
        for e in range(HEADS_PER_BLOCK):
            _fill_na_bias(g_ref.at[e], bias_ref.at[e], lo_scr, hi_scr, plan)

    q2, k2, v2 = q_ref[0], k_ref[0], v_ref[0]
    past = ck_ref.shape[-1]
    kct = ck_ref[0, 0].reshape(LANES, past).astype(BF16)
    vct = cv_ref[0, 0].reshape(LANES, past).astype(BF16)
    q2 = q2 * ATTN_SCALE
    nq = NA_QROWS * GRID_W
    dn = (((1,), (1,)), ((), ()))
    units = [(grp, e) for grp in groups for e in range(HEADS_PER_BLOCK)]
    s_ctx = [jnp.dot(_head_q(q2, e), kct, preferred_element_type=F32)
             for e in range(HEADS_PER_BLOCK)]

    def scores(unit):
        (q0, k0, nk, boff), e = unit
        qe = _head_q(q2[q0:q0 + nq], e)
        s_w = (lax.dot_general(qe, k2[k0:k0 + nk], dn, preferred_element_type=F32)
               + bias_ref[e, :, boff:boff + nk])
        return s_w, s_ctx[e][q0:q0 + nq]

    def attend(unit, s_w, s_c):
        (_, k0, nk, _), e = unit
        mx = jnp.maximum(jnp.max(s_w, axis=-1, keepdims=True),
                         jnp.max(s_c, axis=-1, keepdims=True))
        return (jnp.dot(jnp.exp(s_w - mx).astype(BF16), _head_v(v2[k0:k0 + nk], e),
                        preferred_element_type=F32)
                + lax.dot_general(jnp.exp(s_c - mx).astype(BF16), _head_v(vct, e, axis=0), dn,
                                  preferred_element_type=F32))

    accs = []
    for unit, s in _pipelined(units, scores):
        accs.append(attend(unit, *s))
        if len(accs) == HEADS_PER_BLOCK:
            q0 = unit[0][0]
            o_ref[0, q0:q0 + nq, :] = _normalise(accs).astype(BF16)
            accs = []


def _nattn(qkv, cache_kt, cache_vt, rpb, *, layer):
    b, t, _ = qkv.shape
    past = cache_kt.shape[4]
    groups, _, _ = _na_groups(t // GRID_W)
    plan, nbias = _na_bias_plan(t // GRID_W)
    nhp = N_HEADS // HEADS_PER_BLOCK
    nq = NA_QROWS * GRID_W
    n_tiles = 2 * NA_ROWS
    cache_spec = pl.BlockSpec((1, 1, HEADS_PER_BLOCK, HEAD_DIM, past),
                              lambda hp, bi: (bi, layer, hp, 0, 0))
    return pl.pallas_call(
        functools.partial(_nattn_kernel, groups=groups, plan=plan),
        grid=(nhp, b),
        in_specs=[
            pl.BlockSpec((1, t, LANES), lambda hp, bi: (bi, 0, hp)),
            pl.BlockSpec((1, t, LANES), lambda hp, bi: (bi, 0, nhp + hp)),
            pl.BlockSpec((1, t, LANES), lambda hp, bi: (bi, 0, 2 * nhp + hp)),
            cache_spec,
            cache_spec,
            pl.BlockSpec((HEADS_PER_BLOCK, n_tiles, LANES), lambda hp, bi: (hp, 0, 0)),
        ],
        out_specs=pl.BlockSpec((1, t, LANES), lambda hp, bi: (bi, 0, hp)),
        out_shape=jax.ShapeDtypeStruct((b, t, D_MODEL), BF16),
        scratch_shapes=[pltpu.VMEM((HEADS_PER_BLOCK, nq, nbias), F32),
                        pltpu.VMEM((n_tiles, GRID_W, LANES), F32),
                        pltpu.VMEM((n_tiles, GRID_W, LANES), F32)],
        compiler_params=_params(2),
        name="nattn",
    )(qkv, qkv, qkv, cache_kt, cache_vt, _pad_rpb(rpb))


GELU_C = float(np.float32(np.sqrt(2 / np.pi)))
GELU_K = -2.0 * GELU_C * math.log2(math.e)


def _gelu(x):
    return x / (1.0 + jnp.exp2(x * (x * x * (GELU_K * 0.044715) + GELU_K)))


def _gmlp_kernel(x_ref, mod_ref, g_ref, win_ref, gv_ref, ws_ref, bs_ref, *rest, tm, n_side):
    side_in, (o_ref,), side_out, (vv_scr,) = (
        rest[:n_side], rest[n_side:n_side + 1], rest[n_side + 1:2 * n_side + 1], rest[2 * n_side + 1:])
    _side_cast(side_in, side_out)
    e = GMLP_WIDTH
    gd = e // GMLP_GROUPS
    mod = mod_ref[0]
    h = _rms_mod(x_ref[...], g_ref[...], _mod_chunk(mod, 0), _mod_chunk(mod, 1)).astype(BF16)
    pair = 2 * gd

    def u_proj(gp):
        return jnp.dot(h, win_ref[:, gp * pair:(gp + 1) * pair], preferred_element_type=F32)

    zv = _gelu(jnp.dot(h, win_ref[:, e:2 * e], preferred_element_type=F32))
    pipe = _pipelined(list(range(e // pair)), u_proj)
    head = next(pipe)
    xc = zv - jnp.mean(zv, axis=-1, keepdims=True)
    vv = xc * lax.rsqrt(jnp.mean(xc * xc, axis=-1, keepdims=True) + EPS) * gv_ref[...]
    vv_scr[...] = vv.astype(BF16)
    for gp, u_raw in itertools.chain([head], pipe):
        u2 = _gelu(u_raw)
        for gi in range(2):
            g = 2 * gp + gi
            cols = slice(g * gd, (g + 1) * gd)
            nc = tm // CHUNK
            vcat = jnp.concatenate(
                [vv_scr[c * CHUNK:(c + 1) * CHUNK, cols] for c in range(nc)], axis=1)
            mixed = jnp.dot(ws_ref[g].astype(BF16), vcat, preferred_element_type=F32)
            for c in range(nc):
                rows = slice(c * CHUNK, (c + 1) * CHUNK)
                o_ref[rows, cols] = (u2[rows, gi * gd:(gi + 1) * gd]
                                     * (mixed[:, c * gd:(c + 1) * gd] + bs_ref[g])).astype(BF16)


def _gmlp(x, mods3, mod_row_fn, g, w_in, g_v, w_s, b_s_full, side=(), *, tm):
    n, d = x.shape
    e = GMLP_WIDTH
    side_in, side_out, side_shapes = _side_cast_specs(side, n // tm)
    return pl.pallas_call(
        functools.partial(_gmlp_kernel, tm=tm, n_side=len(side)),
        grid=(n // tm,),
        in_specs=[
            pl.BlockSpec((tm, d), lambda i: (i, 0)),
            pl.BlockSpec((1, 1, 6 * d), lambda i: (mod_row_fn(i), 0, 0)),
            _resident((1, d)),
            _resident((d, 2 * e)),
            _resident((1, e)),
            _resident(w_s.shape),
            _resident(b_s_full.shape),
        ] + side_in,
        out_specs=[pl.BlockSpec((tm, e), lambda i: (i, 0))] + side_out,
        out_shape=[jax.ShapeDtypeStruct((n, e), BF16)] + side_shapes,
        scratch_shapes=[pltpu.VMEM((tm, e), BF16)],
        compiler_params=_params(),
        name="gmlp",
    )(x, mods3, g, w_in, g_v, w_s, b_s_full, *(w for w, _ in side))


def _conv_pieces(tm, seq, halo):
    if halo:
        return ((0, tm + 2 * HALO, HALO, tm),)
    return tuple((s * seq, seq, 0, seq) for s in range(tm // seq))


def _mix_ffn_kernel(x_ref, xp_ref, xn_ref, m_ref, mp_ref, mn_ref, mod_ref, gffn_ref, gfin_ref,
                    wmo_ref, wup_ref, wconv_ref, bconv_ref, wdown_ref, *rest,
                    tm, seq, halo, final, n_side):
    side_in, (o_ref,), side_out, (act_scr,) = (
        rest[:n_side], rest[n_side:n_side + 1], rest[n_side + 1:2 * n_side + 1], rest[2 * n_side + 1:])
    _side_cast(side_in, side_out)
    mod = mod_ref[0]
    gate_mix, shift, scale, gate_ffn = (_mod_chunk(mod, k) for k in (2, 3, 4, 5))
    if halo:
        xcat = jnp.concatenate([xp_ref[0], x_ref[...], xn_ref[0]], axis=0)
        mcat = jnp.concatenate([mp_ref[0], m_ref[...], mn_ref[0]], axis=0)
        main = slice(HALO, HALO + tm)
    else:
        xcat, mcat = x_ref[...], m_ref[...]
        main = slice(0, tm)
    r = xcat.shape[0]
    halves = (slice(0, r // 2), slice(r // 2, r))
    x1_parts = [xcat[rows] + gate_mix * jnp.dot(mcat[rows], wmo_ref[...], preferred_element_type=F32)
                for rows in halves]
    x1 = jnp.concatenate(x1_parts, axis=0)
    hcat = jnp.concatenate(
        [_rms_mod(part, gffn_ref[...], shift, scale).astype(BF16) for part in x1_parts], axis=0)

    f = FFN_TILE
    if halo:
        blk = pl.program_id(0) % (seq // tm)
        hcat = jnp.concatenate([
            jnp.where(blk == 0, jnp.zeros((HALO, D_MODEL), BF16), hcat[:HALO]),
            hcat[main],
            jnp.where(blk == seq // tm - 1, jnp.zeros((HALO, D_MODEL), BF16), hcat[HALO + tm:]),
        ], axis=0)
    pieces = _conv_pieces(tm, seq, halo)
    sub = lax.broadcasted_iota(jnp.int32, (PAD, 1), 0)

    def up(c):
        return [jnp.dot(hcat, wup_ref[:, col:col + f], preferred_element_type=F32)
                for col in (c * f, FFN_DIM + c * f)]

    def conv(a, col):
        cols = slice(col, col + f)
        wc = wconv_ref[0, :, cols]
        out = []
        for (a0, n_rows, own0, own) in pieces:
            piece = a[a0:a0 + n_rows]
            own_rows = slice(own0, own0 + own)
            prev = pltpu.roll(piece, 1, 0)[own_rows]
            nxt = pltpu.roll(piece, n_rows - 1, 0)[own_rows]
            if not halo:
                prev = jnp.concatenate([jnp.where(sub == 0, 0.0, prev[:PAD]), prev[PAD:]], axis=0)
                nxt = jnp.concatenate([nxt[:-PAD], jnp.where(sub == PAD - 1, 0.0, nxt[-PAD:])], axis=0)
            out.append(prev * wc[0:1] + piece[own_rows] * wc[1:2] + nxt * wc[2:3]
                       + bconv_ref[0, :, cols])
        return out[0] if len(out) == 1 else jnp.concatenate(out, axis=0)

    for c, (a_gate, a_val) in _pipelined(list(range(FFN_DIM // f)), up):
        act = jax.nn.silu(conv(a_gate, c * f)) * conv(a_val, FFN_DIM + c * f)
        act_scr[:, c * f:(c + 1) * f] = act.astype(BF16)
    acc = jnp.dot(act_scr[...], wdown_ref[...], preferred_element_type=F32)
    out = x1[main] + gate_ffn * acc
    if final:
        out = out * lax.rsqrt(jnp.mean(out * out, axis=-1, keepdims=True) + EPS) * gfin_ref[...]
    o_ref[...] = out


def _mix_ffn(x, m, mods3, mod_row_fn, g_ffn, g_fin, w_mo, w_up, w_conv, b_conv, w_down,
             side=(), *, layer, tm, seq, final):
    n, d = x.shape
    km = m.shape[1]
    halo = tm % seq != 0
    assert not halo or seq % tm == 0
    nh = n // HALO
    per = tm // HALO
    prev_map = lambda i: (jnp.maximum(i * per - 1, 0), 0, 0)
    next_map = lambda i: (jnp.minimum((i + 1) * per, nh - 1), 0, 0)
    side_in, side_out, side_shapes = _side_cast_specs(side, n // tm)
    kern = functools.partial(_mix_ffn_kernel, tm=tm, seq=seq, halo=halo, final=final,
                             n_side=len(side))
    return pl.pallas_call(
        kern,
        grid=(n // tm,),
        in_specs=[
            pl.BlockSpec((tm, d), lambda i: (i, 0)),
            pl.BlockSpec((1, HALO, d), prev_map),
            pl.BlockSpec((1, HALO, d), next_map),
            pl.BlockSpec((tm, km), lambda i: (i, 0)),
            pl.BlockSpec((1, HALO, km), prev_map),
            pl.BlockSpec((1, HALO, km), next_map),
            pl.BlockSpec((1, 1, 6 * d), lambda i: (mod_row_fn(i), 0, 0)),
            _resident((1, d)),
            _resident((1, d)),
            _resident(w_mo.shape),
            _resident(w_up.shape),
            _resident_layer(w_conv.shape, layer),
            _resident_layer(b_conv.shape, layer),
            _resident(w_down.shape),
        ] + side_in,
        out_specs=[pl.BlockSpec((tm, d), lambda i: (i, 0))] + side_out,
        out_shape=[jax.ShapeDtypeStruct((n, d), F32)] + side_shapes,
        scratch_shapes=[pltpu.VMEM((tm, FFN_DIM), BF16)],
        compiler_params=_params(),
        name="mix_ffn",
    )(x, x.reshape(nh, HALO, d), x.reshape(nh, HALO, d),
      m, m.reshape(nh, HALO, km), m.reshape(nh, HALO, km),
      mods3, g_ffn, g_fin, w_mo, w_up, w_conv, b_conv, w_down, *(w for w, _ in side))


def kernel(x_prompt, x_sample, cache_k, cache_v, c, c_ctx, w_ada, b_ada, norm_mix_g,
           norm_ffn_g, norm_final_g, w_qkv, w_attn_out, rpb, w_gmlp_in, g_gmlp_v,
           w_spatial, b_spatial, w_gmlp_out, w_ffn_up, w_ffn_conv, b_ffn_conv, w_ffn_down):
    batch, seq, d = x_prompt.shape
    dec_batch, dec_seq, _ = x_sample.shape
    depth = w_ada.shape[0]
    assert depth == 2 and d == D_MODEL

    cond = jnp.zeros((MOD_ROWS, d), F32).at[0].set(c_ctx).at[1:1 + dec_batch].set(c)
    mods3 = _ada(cond, w_ada, b_ada).reshape(depth * MOD_ROWS, 1, 6 * d)

    w_qkv_b = _to_bf16(w_qkv[0])
    w_sp = w_spatial[0]
    b_s_full = jnp.broadcast_to(b_spatial[0][:, :, None], (GMLP_GROUPS, CHUNK, CHUNK))
    conv_w = (w_ffn_conv, b_ffn_conv[:, None, :])
    g_mix = norm_mix_g.reshape(depth, 1, d)
    g_ffn = norm_ffn_g.reshape(depth, 1, d)
    g_fin = norm_final_g.reshape(1, d)
    tm = 512

    def ctx_row(layer):
        return lambda i: layer * MOD_ROWS
    x = x_prompt.reshape(batch * seq, d)
    o, state_kt, state_vt, w_ao, w_up0, w_down0 = _ctx_attn(
        x, mods3, 0, g_mix[0], w_qkv_b, [(w_attn_out, 0), (w_ffn_up, 0), (w_ffn_down, 0)],
        batch=batch, seq=seq)
    state_k, state_v = jnp.swapaxes(state_kt, 3, 4), jnp.swapaxes(state_vt, 3, 4)
    x, w_gin = _mix_ffn(x, o, mods3, ctx_row(0), g_ffn[0], g_fin, w_ao, w_up0, *conv_w, w_down0,
                        [(w_gmlp_in, 0)], layer=0, tm=tm, seq=seq, final=False)
    m, w_gout, w_up1, w_down1 = _gmlp(
        x, mods3, ctx_row(1), g_mix[1], w_gin, g_gmlp_v[0:1], w_sp, b_s_full,
        [(w_gmlp_out, 0), (w_ffn_up, 1), (w_ffn_down, 1)], tm=tm)
    y_prompt, = _mix_ffn(x, m, mods3, ctx_row(1), g_ffn[1], g_fin, w_gout, w_up1, *conv_w, w_down1,
                         layer=1, tm=tm, seq=seq, final=True)

    def lat_row(layer):
        return lambda i: layer * MOD_ROWS + 1 + (i * tm) // dec_seq
    x = x_sample.reshape(dec_batch * dec_seq, d)
    qkv = _qkv(x, mods3, lat_row(0), g_mix[0], w_qkv_b, tm=tm)
    o = _nattn(qkv.reshape(dec_batch, dec_seq, 3 * d), jnp.swapaxes(cache_k, 3, 4),
               jnp.swapaxes(cache_v, 3, 4), rpb[0], layer=0)
    x, = _mix_ffn(x, o.reshape(dec_batch * dec_seq, d), mods3, lat_row(0), g_ffn[0], g_fin,
                  w_ao, w_up0, *conv_w, w_down0, layer=0, tm=tm, seq=dec_seq, final=False)
    m, = _gmlp(x, mods3, lat_row(1), g_mix[1], w_gin, g_gmlp_v[0:1], w_sp, b_s_full, tm=tm)
    y_sample, = _mix_ffn(x, m, mods3, lat_row(1), g_ffn[1], g_fin, w_gout, w_up1, *conv_w, w_down1,
                         layer=1, tm=tm, seq=dec_seq, final=True)

    return (y_prompt.reshape(batch, seq, d), y_sample.reshape(dec_batch, dec_seq, d),
            state_k, state_v)
```

```python
import functools
import itertools
import math

import numpy as np
import jax
import jax.numpy as jnp
from jax import lax
from jax.experimental import pallas as pl
from jax.experimental.pallas import tpu as pltpu

D_MODEL = 1024
N_HEADS = 16
HEAD_DIM = D_MODEL // N_HEADS
GRID_W = 64
NA_ROWS = 8
NA_COLS = 16
CHUNK = 128
GMLP_WIDTH = 2 * D_MODEL
GMLP_GROUPS = 16
FFN_DIM = 2816
CONV_W = 3
EPS = 1e-6
ATTN_SCALE = HEAD_DIM ** -0.5
assert math.frexp(ATTN_SCALE)[0] == 0.5

LANES = 128
HEADS_PER_BLOCK = LANES // HEAD_DIM
MOD_ROWS = 8
HALO = 16
PAD = 8
FFN_TILE = 256
NA_QROWS = 4
LOOKAHEAD = 1
NEG_BIAS = -1e30
VMEM_LIMIT = 56 * 1024 * 1024

BF16 = jnp.bfloat16
F32 = jnp.float32


def _params(n_axes=1):
    return pltpu.CompilerParams(
        dimension_semantics=("arbitrary",) * n_axes, vmem_limit_bytes=VMEM_LIMIT)


def _resident(shape):
    nd = len(shape)
    return pl.BlockSpec(shape, lambda *_: (0,) * nd, pipeline_mode=pl.Buffered(1))


def _resident_layer(shape, layer):
    nd = len(shape)
    return pl.BlockSpec((1,) + tuple(shape[1:]), lambda *_: (layer,) + (0,) * (nd - 1),
                        pipeline_mode=pl.Buffered(1))


def _pipelined(items, first_stage):
    ready = [first_stage(it) for it in items[:LOOKAHEAD]]
    for i, item in enumerate(items):
        if i + LOOKAHEAD < len(items):
            ready.append(first_stage(items[i + LOOKAHEAD]))
        yield item, ready.pop(0)


def _side_cast_specs(weights, n_steps):
    ins, outs, shapes = [], [], []
    for w, layer in weights:
        _, rows, cols = w.shape
        assert rows % (n_steps * HALO) == 0
        ins.append(pl.BlockSpec((1, rows // n_steps, cols), lambda i, layer=layer: (layer, i, 0)))
        outs.append(pl.BlockSpec((rows // n_steps, cols), lambda i: (i, 0)))
        shapes.append(jax.ShapeDtypeStruct((rows, cols), BF16))
    return ins, outs, shapes


def _side_cast(src_refs, dst_refs):
    for src, dst in zip(src_refs, dst_refs):
        dst[...] = src[0].astype(BF16)


def _rms_mod(x, g, shift, scale):
    y = x * lax.rsqrt(jnp.mean(x * x, axis=-1, keepdims=True) + EPS)
    return (y * g) * (1 + scale) + shift


def _mod_chunk(mod, k):
    return mod[:, k * D_MODEL:(k + 1) * D_MODEL]


CAST_BLOCK_BYTES = 8 * 1024 * 1024


def _cast_kernel(w_ref, o_ref):
    o_ref[...] = w_ref[...].astype(o_ref.dtype)


def _to_bf16(w):
    shape = w.shape
    cols = shape[-1]
    rows = w.size // cols
    per = HALO
    assert rows % per == 0
    blk = max(b for b in range(per, rows + 1, per)
              if rows % b == 0 and (b * cols * 4 <= CAST_BLOCK_BYTES or b == per))
    out = pl.pallas_call(
        _cast_kernel,
        grid=(rows // blk,),
        in_specs=[pl.BlockSpec((blk, cols), lambda i: (i, 0))],
        out_specs=pl.BlockSpec((blk, cols), lambda i: (i, 0)),
        out_shape=jax.ShapeDtypeStruct((rows, cols), BF16),
        compiler_params=_params(),
        name="to_bf16",
    )(w.reshape(rows, cols))
    return out.reshape(shape)


def _ada_kernel(cond_ref, w_ref, b_ref, o_ref):
    s = jax.nn.silu(cond_ref[...]).astype(BF16)
    o_ref[0] = jnp.dot(s, w_ref[0].astype(BF16), preferred_element_type=F32) + b_ref[0]


def _ada(cond, w_ada, b_ada):
    depth, d, n = w_ada.shape
    tn = 1536
    return pl.pallas_call(
        _ada_kernel,
        grid=(depth, n // tn),
        in_specs=[
            pl.BlockSpec((MOD_ROWS, d), lambda l, j: (0, 0)),
            pl.BlockSpec((1, d, tn), lambda l, j: (l, 0, j)),
            pl.BlockSpec((1, 1, tn), lambda l, j: (l, 0, j)),
        ],
        out_specs=pl.BlockSpec((1, MOD_ROWS, tn), lambda l, j: (l, 0, j)),
        out_shape=jax.ShapeDtypeStruct((depth, MOD_ROWS, n), F32),
        compiler_params=_params(2),
        name="ada",
    )(cond, w_ada, b_ada.reshape(depth, 1, n))


def _den_lane(e):
    return (1 - e) * HEAD_DIM


def _head_lanes(e, axis=1):
    ch = lax.broadcasted_iota(jnp.int32, (1, LANES) if axis == 1 else (LANES, 1), axis)
    own = (ch < HEAD_DIM) if e == 0 else (ch >= HEAD_DIM)
    return own, ch == _den_lane(e)


def _head_q(q2, e):
    own, _ = _head_lanes(e)
    return jnp.where(own, q2, jnp.zeros_like(q2))


def _head_v(v2, e, axis=1):
    own, is_den = _head_lanes(e, axis)
    return jnp.where(own, v2, jnp.broadcast_to(is_den.astype(v2.dtype), v2.shape))


def _normalise(accs):
    outs = [acc / acc[:, _den_lane(e):_den_lane(e) + 1] for e, acc in enumerate(accs)]
    own0, _ = _head_lanes(0)
    return jnp.where(own0, outs[0], outs[1])


def _ctx_attn_kernel(x_ref, mod_ref, g_ref, w_ref, *rest, nb, seq, n_side):
    side_in, (o_ref, sk_ref, sv_ref), side_out, (qkv_scr,) = (
        rest[:n_side], rest[n_side:n_side + 3], rest[n_side + 3:2 * n_side + 3], rest[2 * n_side + 3:])
    _side_cast(side_in, side_out)
    mod = mod_ref[0]
    h = _rms_mod(x_ref[...], g_ref[0], _mod_chunk(mod, 0), _mod_chunk(mod, 1)).astype(BF16)
    qkv_scr[...] = jnp.dot(h, w_ref[...], preferred_element_type=F32)
    for b in range(nb):
        rows = slice(b * seq, (b + 1) * seq)
        for hp in range(N_HEADS // HEADS_PER_BLOCK):
            heads = slice(hp * HEADS_PER_BLOCK, (hp + 1) * HEADS_PER_BLOCK)
            for part, s_ref in ((1, sk_ref), (2, sv_ref)):
                cols = slice(part * D_MODEL + hp * LANES, part * D_MODEL + (hp + 1) * LANES)
                s_ref[b, 0, heads] = qkv_scr[rows, cols].T.reshape(HEADS_PER_BLOCK, HEAD_DIM, seq)

    def block(unit, part):
        b, hp = unit
        return (slice(b * seq, (b + 1) * seq),
                slice(part * D_MODEL + hp * LANES, part * D_MODEL + (hp + 1) * LANES))

    def scores(unit):
        q2 = (qkv_scr[block(unit, 0)] * ATTN_SCALE).astype(BF16)
        k2 = qkv_scr[block(unit, 1)].astype(BF16)
        qs = jnp.concatenate([_head_q(q2, e) for e in range(HEADS_PER_BLOCK)], axis=0)
        return lax.dot_general(qs, k2, (((1,), (1,)), ((), ())), preferred_element_type=F32)

    units = [(b, hp) for b in range(nb) for hp in range(N_HEADS // HEADS_PER_BLOCK)]
    for unit, s in _pipelined(units, scores):
        v2 = qkv_scr[block(unit, 2)].astype(BF16)
        p = jnp.exp(s - jnp.max(s, axis=-1, keepdims=True))
        o = (jnp.dot(p.astype(BF16), v2, preferred_element_type=F32)
             / jnp.sum(p, axis=-1, keepdims=True))
        own0, _ = _head_lanes(0)
        o_ref[block(unit, 0)] = jnp.where(own0, o[:seq], o[seq:]).astype(BF16)


def _ctx_attn(x, mods3, mod_row, g, w_qkv, side, *, layer, batch, seq):
    n, d = x.shape
    nb = 2
    tm = nb * seq
    side_in, side_out, side_shapes = _side_cast_specs(side, n // tm)
    kern = functools.partial(_ctx_attn_kernel, nb=nb, seq=seq, n_side=len(side))
    state = jax.ShapeDtypeStruct((batch, 1, N_HEADS, HEAD_DIM, seq), F32)
    state_spec = pl.BlockSpec((nb, 1, N_HEADS, HEAD_DIM, seq), lambda i: (i, 0, 0, 0, 0))
    return pl.pallas_call(
        kern,
        grid=(n // tm,),
        in_specs=[
            pl.BlockSpec((tm, d), lambda i: (i, 0)),
            pl.BlockSpec((1, 1, 6 * d), lambda i: (mod_row, 0, 0)),
            _resident_layer(g.shape, layer),
            _resident((d, 3 * d)),
        ] + side_in,
        out_specs=[pl.BlockSpec((tm, d), lambda i: (i, 0)), state_spec, state_spec] + side_out,
        out_shape=[jax.ShapeDtypeStruct((n, d), BF16), state, state] + side_shapes,
        scratch_shapes=[pltpu.VMEM((tm, 3 * d), F32)],
        compiler_params=_params(),
        name="ctx_attn",
    )(x, mods3, g, w_qkv, *(w for w, _ in side))


def _qkv_kernel(x_ref, mod_ref, g_ref, w_ref, o_ref, *, tn):
    mod = mod_ref[0]
    h = _rms_mod(x_ref[...], g_ref[0], _mod_chunk(mod, 0), _mod_chunk(mod, 1)).astype(BF16)
    for c in range(w_ref.shape[1] // tn):
        cols = slice(c * tn, (c + 1) * tn)
        o_ref[:, cols] = jnp.dot(h, w_ref[:, cols], preferred_element_type=F32).astype(BF16)


def _qkv(x, mods3, mod_row_fn, g, w_qkv, *, layer, tm):
    n, d = x.shape
    nout = w_qkv.shape[1]
    return pl.pallas_call(
        functools.partial(_qkv_kernel, tn=512),
        grid=(n // tm,),
        in_specs=[
            pl.BlockSpec((tm, d), lambda i: (i, 0)),
            pl.BlockSpec((1, 1, 6 * d), lambda i: (mod_row_fn(i), 0, 0)),
            _resident_layer(g.shape, layer),
            _resident((d, nout)),
        ],
        out_specs=pl.BlockSpec((tm, nout), lambda i: (i, 0)),
        out_shape=jax.ShapeDtypeStruct((n, nout), BF16),
        compiler_params=_params(),
        name="qkv",
    )(x, mods3, g, w_qkv)


def _na_groups(rows):
    kh = min(NA_ROWS, rows)
    row_start = [min(max(r - kh // 2, 0), rows - kh) for r in range(rows)]
    groups, off = [], 0
    for r0 in range(0, rows, NA_QROWS):
        lo = min(row_start[r0:r0 + NA_QROWS])
        hi = max(row_start[r0:r0 + NA_QROWS]) + kh
        if (hi - lo) * GRID_W % LANES:
            if hi < rows:
                hi += 1
            else:
                lo -= 1
        nk = (hi - lo) * GRID_W
        groups.append((r0 * GRID_W, lo * GRID_W, nk, off))
        off += nk
    return tuple(groups), row_start, kh


def _na_bias_plan(rows):
    groups, row_start, kh = _na_groups(rows)
    nro = 2 * NA_ROWS - 1
    w = GRID_W
    assert 2 * w == LANES
    plan = []
    for (q0, k0, nk, off) in groups:
        for qi in range(NA_QROWS):
            qr = q0 // w + qi
            rs = row_start[qr]
            tile_of = lambda kr: kr - qr + NA_ROWS - 1 if rs <= kr < rs + kh else nro
            for p in range(nk // LANES):
                kr = k0 // w + 2 * p
                plan.append((qi * w, off + p * LANES, tile_of(kr), tile_of(kr + 1)))
    return tuple(plan), sum(g[2] for g in groups)


def _pad_rpb(rpb):
    lead = GRID_W - NA_COLS
    return jnp.pad(rpb, ((0, 0), (0, 1), (lead, LANES - rpb.shape[2] - lead)))


def _fill_na_bias(g_ref, o_ref, lo_scr, hi_scr, plan):
    w = GRID_W
    n_tiles = 2 * NA_ROWS - 1
    lane = lax.broadcasted_iota(jnp.int32, (w, LANES), 1)
    qc = lax.broadcasted_iota(jnp.int32, (w, LANES), 0)
    kc = lane % w
    col_start = jnp.clip(qc - NA_COLS // 2, 0, w - NA_COLS)
    col_ok = (kc >= col_start) & (kc < col_start + NA_COLS)
    in_lo = lane < w
    neg = jnp.full((w, LANES), NEG_BIAS, F32)
    for a in range(n_tiles):
        row = jnp.broadcast_to(g_ref[a:a + 1, :], (w, LANES))
        lo_scr[a] = jnp.where(col_ok, pltpu.roll(row, w + 1, 1, stride=1, stride_axis=0), neg)
        hi_scr[a] = jnp.where(col_ok, pltpu.roll(row, 1, 1, stride=1, stride_axis=0), neg)
    lo_scr[n_tiles] = neg
    hi_scr[n_tiles] = neg
    for (r0, c0, a_lo, a_hi) in plan:
        o_ref[r0:r0 + w, c0:c0 + LANES] = jnp.where(in_lo, lo_scr[a_lo], hi_scr[a_hi])


def _nattn_kernel(q_ref, k_ref, v_ref, ck_ref, cv_ref, g_ref, o_ref, bias_ref, lo_scr, hi_scr,
                  *, groups, plan):
    @pl.when(pl.program_id(1) == 0)
    def _():
        for e in range(HEADS_PER_BLOCK):
            _fill_na_bias(g_ref.at[e], bias_ref.at[e], lo_scr, hi_scr, plan)

    q2, k2, v2 = q_ref[0], k_ref[0], v_ref[0]
    past = ck_ref.shape[-1]
    kct = ck_ref[0, 0].reshape(LANES, past).astype(BF16)
    vct = cv_ref[0, 0].reshape(LANES, past).astype(BF16)
    q2 = q2 * ATTN_SCALE
    nq = NA_QROWS * GRID_W
    dn = (((1,), (1,)), ((), ()))
    units = [(grp, e) for grp in groups for e in range(HEADS_PER_BLOCK)]
    s_ctx = [jnp.dot(_head_q(q2, e), kct, preferred_element_type=F32)
             for e in range(HEADS_PER_BLOCK)]

    def scores(unit):
        (q0, k0, nk, boff), e = unit
        qe = _head_q(q2[q0:q0 + nq], e)
        s_w = (lax.dot_general(qe, k2[k0:k0 + nk], dn, preferred_element_type=F32)
               + bias_ref[e, :, boff:boff + nk])
        return s_w, s_ctx[e][q0:q0 + nq]

    def attend(unit, s_w, s_c):
        (_, k0, nk, _), e = unit
        mx = jnp.maximum(jnp.max(s_w, axis=-1, keepdims=True),
                         jnp.max(s_c, axis=-1, keepdims=True))
        return (jnp.dot(jnp.exp(s_w - mx).astype(BF16), _head_v(v2[k0:k0 + nk], e),
                        preferred_element_type=F32)
                + lax.dot_general(jnp.exp(s_c - mx).astype(BF16), _head_v(vct, e, axis=0), dn,
                                  preferred_element_type=F32))

    accs = []
    for unit, s in _pipelined(units, scores):
        accs.append(attend(unit, *s))
        if len(accs) == HEADS_PER_BLOCK:
            q0 = unit[0][0]
            o_ref[0, q0:q0 + nq, :] = _normalise(accs).astype(BF16)
            accs = []


def _nattn(qkv, cache_kt, cache_vt, rpb, *, layer):
    b, t, _ = qkv.shape
    past = cache_kt.shape[4]
    groups, _, _ = _na_groups(t // GRID_W)
    plan, nbias = _na_bias_plan(t // GRID_W)
    nhp = N_HEADS // HEADS_PER_BLOCK
    nq = NA_QROWS * GRID_W
    n_tiles = 2 * NA_ROWS
    cache_spec = pl.BlockSpec((1, 1, HEADS_PER_BLOCK, HEAD_DIM, past),
                              lambda hp, bi: (bi, layer, hp, 0, 0))
    return pl.pallas_call(
        functools.partial(_nattn_kernel, groups=groups, plan=plan),
        grid=(nhp, b),
        in_specs=[
            pl.BlockSpec((1, t, LANES), lambda hp, bi: (bi, 0, hp)),
            pl.BlockSpec((1, t, LANES), lambda hp, bi: (bi, 0, nhp + hp)),
            pl.BlockSpec((1, t, LANES), lambda hp, bi: (bi, 0, 2 * nhp + hp)),
            cache_spec,
            cache_spec,
            pl.BlockSpec((HEADS_PER_BLOCK, n_tiles, LANES), lambda hp, bi: (hp, 0, 0)),
        ],
        out_specs=pl.BlockSpec((1, t, LANES), lambda hp, bi: (bi, 0, hp)),
        out_shape=jax.ShapeDtypeStruct((b, t, D_MODEL), BF16),
        scratch_shapes=[pltpu.VMEM((HEADS_PER_BLOCK, nq, nbias), F32),
                        pltpu.VMEM((n_tiles, GRID_W, LANES), F32),
                        pltpu.VMEM((n_tiles, GRID_W, LANES), F32)],
        compiler_params=_params(2),
        name="nattn",
    )(qkv, qkv, qkv, cache_kt, cache_vt, _pad_rpb(rpb))


GELU_C = float(np.float32(np.sqrt(2 / np.pi)))
GELU_K = -2.0 * GELU_C * math.log2(math.e)


def _gelu(x):
    return x / (1.0 + jnp.exp2(x * (x * x * (GELU_K * 0.044715) + GELU_K)))


def _gmlp_kernel(x_ref, mod_ref, g_ref, win_ref, gv_ref, ws_ref, bs_ref, *rest, tm, n_side):
    side_in, (o_ref,), side_out, (vv_scr,) = (
        rest[:n_side], rest[n_side:n_side + 1], rest[n_side + 1:2 * n_side + 1], rest[2 * n_side + 1:])
    _side_cast(side_in, side_out)
    e = GMLP_WIDTH
    gd = e // GMLP_GROUPS
    mod = mod_ref[0]
    h = _rms_mod(x_ref[...], g_ref[0], _mod_chunk(mod, 0), _mod_chunk(mod, 1)).astype(BF16)
    pair = 2 * gd

    def u_proj(gp):
        return jnp.dot(h, win_ref[:, gp * pair:(gp + 1) * pair], preferred_element_type=F32)

    zv = _gelu(jnp.dot(h, win_ref[:, e:2 * e], preferred_element_type=F32))
    pipe = _pipelined(list(range(e // pair)), u_proj)
    head = next(pipe)
    xc = zv - jnp.mean(zv, axis=-1, keepdims=True)
    vv = xc * lax.rsqrt(jnp.mean(xc * xc, axis=-1, keepdims=True) + EPS) * gv_ref[...]
    vv_scr[...] = vv.astype(BF16)
    for gp, u_raw in itertools.chain([head], pipe):
        u2 = _gelu(u_raw)
        for gi in range(2):
            g = 2 * gp + gi
            cols = slice(g * gd, (g + 1) * gd)
            nc = tm // CHUNK
            vcat = jnp.concatenate(
                [vv_scr[c * CHUNK:(c + 1) * CHUNK, cols] for c in range(nc)], axis=1)
            mixed = jnp.dot(ws_ref[g].astype(BF16), vcat, preferred_element_type=F32)
            for c in range(nc):
                rows = slice(c * CHUNK, (c + 1) * CHUNK)
                o_ref[rows, cols] = (u2[rows, gi * gd:(gi + 1) * gd]
                                     * (mixed[:, c * gd:(c + 1) * gd] + bs_ref[g])).astype(BF16)


def _gmlp(x, mods3, mod_row_fn, g, w_in, g_v, w_s, b_s_full, side=(), *, layer, tm):
    n, d = x.shape
    e = GMLP_WIDTH
    side_in, side_out, side_shapes = _side_cast_specs(side, n // tm)
    return pl.pallas_call(
        functools.partial(_gmlp_kernel, tm=tm, n_side=len(side)),
        grid=(n // tm,),
        in_specs=[
            pl.BlockSpec((tm, d), lambda i: (i, 0)),
            pl.BlockSpec((1, 1, 6 * d), lambda i: (mod_row_fn(i), 0, 0)),
            _resident_layer(g.shape, layer),
            _resident((d, 2 * e)),
            _resident((1, e)),
            _resident(w_s.shape),
            _resident(b_s_full.shape),
        ] + side_in,
        out_specs=[pl.BlockSpec((tm, e), lambda i: (i, 0))] + side_out,
        out_shape=[jax.ShapeDtypeStruct((n, e), BF16)] + side_shapes,
        scratch_shapes=[pltpu.VMEM((tm, e), BF16)],
        compiler_params=_params(),
        name="gmlp",
    )(x, mods3, g, w_in, g_v, w_s, b_s_full, *(w for w, _ in side))


def _conv_pieces(tm, seq, halo):
    if halo:
        return ((0, tm + 2 * HALO, HALO, tm),)
    return tuple((s * seq, seq, 0, seq) for s in range(tm // seq))


def _mix_ffn_kernel(x_ref, xp_ref, xn_ref, m_ref, mp_ref, mn_ref, mod_ref, gffn_ref, gfin_ref,
                    wmo_ref, wup_ref, wconv_ref, bconv_ref, wdown_ref, *rest,
                    layer, tm, seq, halo, final, n_side):
    side_in, (o_ref,), side_out, (act_scr,) = (
        rest[:n_side], rest[n_side:n_side + 1], rest[n_side + 1:2 * n_side + 1], rest[2 * n_side + 1:])
    _side_cast(side_in, side_out)
    mod = mod_ref[0]
    gate_mix, shift, scale, gate_ffn = (_mod_chunk(mod, k) for k in (2, 3, 4, 5))
    if halo:
        xcat = jnp.concatenate([xp_ref[0], x_ref[...], xn_ref[0]], axis=0)
        mcat = jnp.concatenate([mp_ref[0], m_ref[...], mn_ref[0]], axis=0)
        main = slice(HALO, HALO + tm)
    else:
        xcat, mcat = x_ref[...], m_ref[...]
        main = slice(0, tm)
    r = xcat.shape[0]
    halves = (slice(0, r // 2), slice(r // 2, r))
    x1_parts = [xcat[rows] + gate_mix * jnp.dot(mcat[rows], wmo_ref[...], preferred_element_type=F32)
                for rows in halves]
    x1 = jnp.concatenate(x1_parts, axis=0)
    hcat = jnp.concatenate(
        [_rms_mod(part, gffn_ref[0], shift, scale).astype(BF16) for part in x1_parts], axis=0)

    f = FFN_TILE
    if halo:
        blk = pl.program_id(0) % (seq // tm)
        hcat = jnp.concatenate([
            jnp.where(blk == 0, jnp.zeros((HALO, D_MODEL), BF16), hcat[:HALO]),
            hcat[main],
            jnp.where(blk == seq // tm - 1, jnp.zeros((HALO, D_MODEL), BF16), hcat[HALO + tm:]),
        ], axis=0)
    pieces = _conv_pieces(tm, seq, halo)
    sub = lax.broadcasted_iota(jnp.int32, (PAD, 1), 0)

    def up(c):
        return [jnp.dot(hcat, wup_ref[:, col:col + f], preferred_element_type=F32)
                for col in (c * f, FFN_DIM + c * f)]

    def conv(a, col):
        cols = slice(col, col + f)
        wc = wconv_ref[0, :, cols]
        out = []
        for (a0, n_rows, own0, own) in pieces:
            piece = a[a0:a0 + n_rows]
            own_rows = slice(own0, own0 + own)
            prev = pltpu.roll(piece, 1, 0)[own_rows]
            nxt = pltpu.roll(piece, n_rows - 1, 0)[own_rows]
            if not halo:
                prev = jnp.concatenate([jnp.where(sub == 0, 0.0, prev[:PAD]), prev[PAD:]], axis=0)
                nxt = jnp.concatenate([nxt[:-PAD], jnp.where(sub == PAD - 1, 0.0, nxt[-PAD:])], axis=0)
            out.append(prev * wc[0:1] + piece[own_rows] * wc[1:2] + nxt * wc[2:3]
                       + bconv_ref[layer:layer + 1, cols])
        return out[0] if len(out) == 1 else jnp.concatenate(out, axis=0)

    for c, (a_gate, a_val) in _pipelined(list(range(FFN_DIM // f)), up):
        act = jax.nn.silu(conv(a_gate, c * f)) * conv(a_val, FFN_DIM + c * f)
        act_scr[:, c * f:(c + 1) * f] = act.astype(BF16)
    acc = jnp.dot(act_scr[...], wdown_ref[...], preferred_element_type=F32)
    out = x1[main] + gate_ffn * acc
    if final:
        out = out * lax.rsqrt(jnp.mean(out * out, axis=-1, keepdims=True) + EPS) * gfin_ref[...]
    o_ref[...] = out


def _mix_ffn(x, m, mods3, mod_row_fn, g_ffn, g_fin, w_mo, w_up, w_conv, b_conv, w_down,
             side=(), *, layer, tm, seq, final):
    n, d = x.shape
    km = m.shape[1]
    halo = tm % seq != 0
    assert not halo or seq % tm == 0
    nh = n // HALO
    per = tm // HALO
    prev_map = lambda i: (jnp.maximum(i * per - 1, 0), 0, 0)
    next_map = lambda i: (jnp.minimum((i + 1) * per, nh - 1), 0, 0)
    side_in, side_out, side_shapes = _side_cast_specs(side, n // tm)
    kern = functools.partial(_mix_ffn_kernel, layer=layer, tm=tm, seq=seq, halo=halo, final=final,
                             n_side=len(side))
    return pl.pallas_call(
        kern,
        grid=(n // tm,),
        in_specs=[
            pl.BlockSpec((tm, d), lambda i: (i, 0)),
            pl.BlockSpec((1, HALO, d), prev_map),
            pl.BlockSpec((1, HALO, d), next_map),
            pl.BlockSpec((tm, km), lambda i: (i, 0)),
            pl.BlockSpec((1, HALO, km), prev_map),
            pl.BlockSpec((1, HALO, km), next_map),
            pl.BlockSpec((1, 1, 6 * d), lambda i: (mod_row_fn(i), 0, 0)),
            _resident_layer(g_ffn.shape, layer),
            _resident((1, d)),
            _resident(w_mo.shape),
            _resident(w_up.shape),
            _resident_layer(w_conv.shape, layer),
            _resident(b_conv.shape),
            _resident(w_down.shape),
        ] + side_in,
        out_specs=[pl.BlockSpec((tm, d), lambda i: (i, 0))] + side_out,
        out_shape=[jax.ShapeDtypeStruct((n, d), F32)] + side_shapes,
        scratch_shapes=[pltpu.VMEM((tm, FFN_DIM), BF16)],
        compiler_params=_params(),
        name="mix_ffn",
    )(x, x.reshape(nh, HALO, d), x.reshape(nh, HALO, d),
      m, m.reshape(nh, HALO, km), m.reshape(nh, HALO, km),
      mods3, g_ffn, g_fin, w_mo, w_up, w_conv, b_conv, w_down, *(w for w, _ in side))


def kernel(x_prompt, x_sample, cache_k, cache_v, c, c_ctx, w_ada, b_ada, norm_mix_g,
           norm_ffn_g, norm_final_g, w_qkv, w_attn_out, rpb, w_gmlp_in, g_gmlp_v,
           w_spatial, b_spatial, w_gmlp_out, w_ffn_up, w_ffn_conv, b_ffn_conv, w_ffn_down):
    batch, seq, d = x_prompt.shape
    dec_batch, dec_seq, _ = x_sample.shape
    depth = w_ada.shape[0]
    assert depth == 2 and d == D_MODEL

    cond = jnp.zeros((MOD_ROWS, d), F32).at[0].set(c_ctx).at[1:1 + dec_batch].set(c)
    mods3 = _ada(cond, w_ada, b_ada).reshape(depth * MOD_ROWS, 1, 6 * d)

    w_qkv_b = _to_bf16(w_qkv[0])
    w_sp = w_spatial[0]
    b_s_full = jnp.broadcast_to(b_spatial[0][:, :, None], (GMLP_GROUPS, CHUNK, CHUNK))
    conv_w = (w_ffn_conv, b_ffn_conv)
    g_mix = norm_mix_g.reshape(depth, 1, d)
    g_ffn = norm_ffn_g.reshape(depth, 1, d)
    g_fin = norm_final_g.reshape(1, d)
    tm = 512

    def ctx_row(layer):
        return lambda i: layer * MOD_ROWS
    x = x_prompt.reshape(batch * seq, d)
    o, state_kt, state_vt, w_ao, w_up0, w_down0 = _ctx_attn(
        x, mods3, 0, g_mix, w_qkv_b, [(w_attn_out, 0), (w_ffn_up, 0), (w_ffn_down, 0)],
        layer=0, batch=batch, seq=seq)
    state_k, state_v = jnp.swapaxes(state_kt, 3, 4), jnp.swapaxes(state_vt, 3, 4)
    x, w_gin = _mix_ffn(x, o, mods3, ctx_row(0), g_ffn, g_fin, w_ao, w_up0, *conv_w, w_down0,
                        [(w_gmlp_in, 0)], layer=0, tm=tm, seq=seq, final=False)
    m, w_gout, w_up1, w_down1 = _gmlp(
        x, mods3, ctx_row(1), g_mix, w_gin, g_gmlp_v, w_sp, b_s_full,
        [(w_gmlp_out, 0), (w_ffn_up, 1), (w_ffn_down, 1)], layer=1, tm=tm)
    y_prompt, = _mix_ffn(x, m, mods3, ctx_row(1), g_ffn, g_fin, w_gout, w_up1, *conv_w, w_down1,
                         layer=1, tm=tm, seq=seq, final=True)

    def lat_row(layer):
        return lambda i: layer * MOD_ROWS + 1 + (i * tm) // dec_seq
    x = x_sample.reshape(dec_batch * dec_seq, d)
    qkv = _qkv(x, mods3, lat_row(0), g_mix, w_qkv_b, layer=0, tm=tm)
    o = _nattn(qkv.reshape(dec_batch, dec_seq, 3 * d), jnp.swapaxes(cache_k, 3, 4),
               jnp.swapaxes(cache_v, 3, 4), rpb[0], layer=0)
    x, = _mix_ffn(x, o.reshape(dec_batch * dec_seq, d), mods3, lat_row(0), g_ffn, g_fin,
                  w_ao, w_up0, *conv_w, w_down0, layer=0, tm=tm, seq=dec_seq, final=False)
    m, = _gmlp(x, mods3, lat_row(1), g_mix, w_gin, g_gmlp_v, w_sp, b_s_full, layer=1, tm=tm)
    y_sample, = _mix_ffn(x, m, mods3, lat_row(1), g_ffn, g_fin, w_gout, w_up1, *conv_w, w_down1,
                         layer=1, tm=tm, seq=dec_seq, final=True)

    return (y_prompt.reshape(batch, seq, d), y_sample.reshape(dec_batch, dec_seq, d),
            state_k, state_v)
```

```python
import functools
import itertools
import math

import numpy as np
import jax
import jax.numpy as jnp
from jax import lax
from jax.experimental import pallas as pl
from jax.experimental.pallas import tpu as pltpu

D_MODEL = 1024
N_HEADS = 16
HEAD_DIM = D_MODEL // N_HEADS
GRID_W = 64
NA_ROWS = 8
NA_COLS = 16
CHUNK = 128
GMLP_WIDTH = 2 * D_MODEL
GMLP_GROUPS = 16
FFN_DIM = 2816
CONV_W = 3
EPS = 1e-6
ATTN_SCALE = HEAD_DIM ** -0.5
assert math.frexp(ATTN_SCALE)[0] == 0.5

LANES = 128
HEADS_PER_BLOCK = LANES // HEAD_DIM
MOD_ROWS = 8
HALO = 16
PAD = 8
FFN_TILE = 256
NA_QROWS = 4
LOOKAHEAD = 1
NEG_BIAS = -1e30
VMEM_LIMIT = 56 * 1024 * 1024

BF16 = jnp.bfloat16
F32 = jnp.float32


def _params(n_axes=1):
    return pltpu.CompilerParams(
        dimension_semantics=("arbitrary",) * n_axes, vmem_limit_bytes=VMEM_LIMIT)


def _resident(shape):
    nd = len(shape)
    return pl.BlockSpec(shape, lambda *_: (0,) * nd, pipeline_mode=pl.Buffered(1))


def _resident_layer(shape, layer):
    nd = len(shape)
    return pl.BlockSpec((1,) + tuple(shape[1:]), lambda *_: (layer,) + (0,) * (nd - 1),
                        pipeline_mode=pl.Buffered(1))


def _pipelined(items, first_stage):
    ready = [first_stage(it) for it in items[:LOOKAHEAD]]
    for i, item in enumerate(items):
        if i + LOOKAHEAD < len(items):
            ready.append(first_stage(items[i + LOOKAHEAD]))
        yield item, ready.pop(0)


def _side_cast_specs(weights, n_steps):
    ins, outs, shapes = [], [], []
    for w, layer in weights:
        _, rows, cols = w.shape
        assert rows % (n_steps * HALO) == 0
        ins.append(pl.BlockSpec((1, rows // n_steps, cols), lambda i, layer=layer: (layer, i, 0)))
        outs.append(pl.BlockSpec((rows // n_steps, cols), lambda i: (i, 0)))
        shapes.append(jax.ShapeDtypeStruct((rows, cols), BF16))
    return ins, outs, shapes


def _side_cast(src_refs, dst_refs):
    for src, dst in zip(src_refs, dst_refs):
        dst[...] = src[0].astype(BF16)


def _rms_mod(x, g, shift, scale):
    y = x * lax.rsqrt(jnp.mean(x * x, axis=-1, keepdims=True) + EPS)
    return (y * g) * (1 + scale) + shift


def _mod_chunk(mod, k):
    return mod[:, k * D_MODEL:(k + 1) * D_MODEL]


CAST_BLOCK_BYTES = 8 * 1024 * 1024


def _cast_kernel(w_ref, o_ref):
    o_ref[...] = w_ref[...].astype(o_ref.dtype)


def _to_bf16(w):
    shape = w.shape
    cols = shape[-1]
    rows = w.size // cols
    per = HALO
    assert rows % per == 0
    blk = max(b for b in range(per, rows + 1, per)
              if rows % b == 0 and (b * cols * 4 <= CAST_BLOCK_BYTES or b == per))
    out = pl.pallas_call(
        _cast_kernel,
        grid=(rows // blk,),
        in_specs=[pl.BlockSpec((blk, cols), lambda i: (i, 0))],
        out_specs=pl.BlockSpec((blk, cols), lambda i: (i, 0)),
        out_shape=jax.ShapeDtypeStruct((rows, cols), BF16),
        compiler_params=_params(),
        name="to_bf16",
    )(w.reshape(rows, cols))
    return out.reshape(shape)


def _ada_kernel(cond_ref, w_ref, b_ref, o_ref):
    s = jax.nn.silu(cond_ref[...]).astype(BF16)
    bias = b_ref[pl.ds(pl.program_id(0), 1), :]
    o_ref[0] = jnp.dot(s, w_ref[0].astype(BF16), preferred_element_type=F32) + bias


def _ada(cond, w_ada, b_ada):
    depth, d, n = w_ada.shape
    tn = n // 2
    return pl.pallas_call(
        _ada_kernel,
        grid=(depth, n // tn),
        in_specs=[
            pl.BlockSpec((MOD_ROWS, d), lambda l, j: (0, 0)),
            pl.BlockSpec((1, d, tn), lambda l, j: (l, 0, j)),
            pl.BlockSpec((depth, tn), lambda l, j: (0, j)),
        ],
        out_specs=pl.BlockSpec((1, MOD_ROWS, tn), lambda l, j: (l, 0, j)),
        out_shape=jax.ShapeDtypeStruct((depth, MOD_ROWS, n), F32),
        compiler_params=_params(2),
        name="ada",
    )(cond, w_ada, b_ada)


def _den_lane(e):
    return (1 - e) * HEAD_DIM


def _head_lanes(e, axis=1):
    ch = lax.broadcasted_iota(jnp.int32, (1, LANES) if axis == 1 else (LANES, 1), axis)
    own = (ch < HEAD_DIM) if e == 0 else (ch >= HEAD_DIM)
    return own, ch == _den_lane(e)


def _head_q(q2, e):
    own, _ = _head_lanes(e)
    return jnp.where(own, q2, jnp.zeros_like(q2))


def _head_v(v2, e, axis=1):
    own, is_den = _head_lanes(e, axis)
    return jnp.where(own, v2, jnp.broadcast_to(is_den.astype(v2.dtype), v2.shape))


def _normalise(accs):
    outs = [acc / acc[:, _den_lane(e):_den_lane(e) + 1] for e, acc in enumerate(accs)]
    own0, _ = _head_lanes(0)
    return jnp.where(own0, outs[0], outs[1])


def _ctx_attn_kernel(x_ref, mod_ref, g_ref, w_ref, *rest, layer, nb, seq, n_side):
    side_in, (o_ref, sk_ref, sv_ref), side_out, (qkv_scr,) = (
        rest[:n_side], rest[n_side:n_side + 3], rest[n_side + 3:2 * n_side + 3], rest[2 * n_side + 3:])
    _side_cast(side_in, side_out)
    mod = mod_ref[0]
    h = _rms_mod(x_ref[...], g_ref[layer:layer + 1, :], _mod_chunk(mod, 0), _mod_chunk(mod, 1)).astype(BF16)
    qkv_scr[...] = jnp.dot(h, w_ref[...], preferred_element_type=F32)
    for b in range(nb):
        rows = slice(b * seq, (b + 1) * seq)
        for hp in range(N_HEADS // HEADS_PER_BLOCK):
            heads = slice(hp * HEADS_PER_BLOCK, (hp + 1) * HEADS_PER_BLOCK)
            for part, s_ref in ((1, sk_ref), (2, sv_ref)):
                cols = slice(part * D_MODEL + hp * LANES, part * D_MODEL + (hp + 1) * LANES)
                s_ref[b, 0, heads] = qkv_scr[rows, cols].T.reshape(HEADS_PER_BLOCK, HEAD_DIM, seq)

    def block(unit, part):
        b, hp = unit
        return (slice(b * seq, (b + 1) * seq),
                slice(part * D_MODEL + hp * LANES, part * D_MODEL + (hp + 1) * LANES))

    def scores(unit):
        q2 = (qkv_scr[block(unit, 0)] * ATTN_SCALE).astype(BF16)
        k2 = qkv_scr[block(unit, 1)].astype(BF16)
        qs = jnp.concatenate([_head_q(q2, e) for e in range(HEADS_PER_BLOCK)], axis=0)
        return lax.dot_general(qs, k2, (((1,), (1,)), ((), ())), preferred_element_type=F32)

    units = [(b, hp) for b in range(nb) for hp in range(N_HEADS // HEADS_PER_BLOCK)]
    for unit, s in _pipelined(units, scores):
        v2 = qkv_scr[block(unit, 2)].astype(BF16)
        p = jnp.exp(s - jnp.max(s, axis=-1, keepdims=True))
        o = (jnp.dot(p.astype(BF16), v2, preferred_element_type=F32)
             / jnp.sum(p, axis=-1, keepdims=True))
        own0, _ = _head_lanes(0)
        o_ref[block(unit, 0)] = jnp.where(own0, o[:seq], o[seq:]).astype(BF16)


def _ctx_attn(x, mods3, mod_row, g, w_qkv, side, *, layer, batch, seq):
    n, d = x.shape
    nb = 2
    tm = nb * seq
    side_in, side_out, side_shapes = _side_cast_specs(side, n // tm)
    kern = functools.partial(_ctx_attn_kernel, layer=layer, nb=nb, seq=seq, n_side=len(side))
    state = jax.ShapeDtypeStruct((batch, 1, N_HEADS, HEAD_DIM, seq), F32)
    state_spec = pl.BlockSpec((nb, 1, N_HEADS, HEAD_DIM, seq), lambda i: (i, 0, 0, 0, 0))
    return pl.pallas_call(
        kern,
        grid=(n // tm,),
        in_specs=[
            pl.BlockSpec((tm, d), lambda i: (i, 0)),
            pl.BlockSpec((1, 1, 6 * d), lambda i: (mod_row, 0, 0)),
            _resident(g.shape),
            _resident((d, 3 * d)),
        ] + side_in,
        out_specs=[pl.BlockSpec((tm, d), lambda i: (i, 0)), state_spec, state_spec] + side_out,
        out_shape=[jax.ShapeDtypeStruct((n, d), BF16), state, state] + side_shapes,
        scratch_shapes=[pltpu.VMEM((tm, 3 * d), F32)],
        compiler_params=_params(),
        name="ctx_attn",
    )(x, mods3, g, w_qkv, *(w for w, _ in side))


def _qkv_kernel(x_ref, mod_ref, g_ref, w_ref, o_ref, *, layer, tn):
    mod = mod_ref[0]
    h = _rms_mod(x_ref[...], g_ref[layer:layer + 1, :], _mod_chunk(mod, 0), _mod_chunk(mod, 1)).astype(BF16)
    for c in range(w_ref.shape[1] // tn):
        cols = slice(c * tn, (c + 1) * tn)
        o_ref[:, cols] = jnp.dot(h, w_ref[:, cols], preferred_element_type=F32).astype(BF16)


def _qkv(x, mods3, mod_row_fn, g, w_qkv, *, layer, tm):
    n, d = x.shape
    nout = w_qkv.shape[1]
    return pl.pallas_call(
        functools.partial(_qkv_kernel, layer=layer, tn=512),
        grid=(n // tm,),
        in_specs=[
            pl.BlockSpec((tm, d), lambda i: (i, 0)),
            pl.BlockSpec((1, 1, 6 * d), lambda i: (mod_row_fn(i), 0, 0)),
            _resident(g.shape),
            _resident((d, nout)),
        ],
        out_specs=pl.BlockSpec((tm, nout), lambda i: (i, 0)),
        out_shape=jax.ShapeDtypeStruct((n, nout), BF16),
        compiler_params=_params(),
        name="qkv",
    )(x, mods3, g, w_qkv)


def _na_groups(rows):
    kh = min(NA_ROWS, rows)
    row_start = [min(max(r - kh // 2, 0), rows - kh) for r in range(rows)]
    groups, off = [], 0
    for r0 in range(0, rows, NA_QROWS):
        lo = min(row_start[r0:r0 + NA_QROWS])
        hi = max(row_start[r0:r0 + NA_QROWS]) + kh
        if (hi - lo) * GRID_W % LANES:
            if hi < rows:
                hi += 1
            else:
                lo -= 1
        nk = (hi - lo) * GRID_W
        groups.append((r0 * GRID_W, lo * GRID_W, nk, off))
        off += nk
    return tuple(groups), row_start, kh


def _na_bias_plan(rows):
    groups, row_start, kh = _na_groups(rows)
    nro = 2 * NA_ROWS - 1
    w = GRID_W
    assert 2 * w == LANES
    plan = []
    for (q0, k0, nk, off) in groups:
        for qi in range(NA_QROWS):
            qr = q0 // w + qi
            rs = row_start[qr]
            tile_of = lambda kr: kr - qr + NA_ROWS - 1 if rs <= kr < rs + kh else nro
            for p in range(nk // LANES):
                kr = k0 // w + 2 * p
                plan.append((qi * w, off + p * LANES, tile_of(kr), tile_of(kr + 1)))
    return tuple(plan), sum(g[2] for g in groups)


def _pad_rpb(rpb):
    lead = GRID_W - NA_COLS
    return jnp.pad(rpb, ((0, 0), (0, 1), (lead, LANES - rpb.shape[2] - lead)))


def _fill_na_bias(g_ref, o_ref, lo_scr, hi_scr, plan):
    w = GRID_W
    n_tiles = 2 * NA_ROWS - 1
    lane = lax.broadcasted_iota(jnp.int32, (w, LANES), 1)
    qc = lax.broadcasted_iota(jnp.int32, (w, LANES), 0)
    kc = lane % w
    col_start = jnp.clip(qc - NA_COLS // 2, 0, w - NA_COLS)
    col_ok = (kc >= col_start) & (kc < col_start + NA_COLS)
    in_lo = lane < w
    neg = jnp.full((w, LANES), NEG_BIAS, F32)
    for a in range(n_tiles):
        row = jnp.broadcast_to(g_ref[a:a + 1, :], (w, LANES))
        lo_scr[a] = jnp.where(col_ok, pltpu.roll(row, w + 1, 1, stride=1, stride_axis=0), neg)
        hi_scr[a] = jnp.where(col_ok, pltpu.roll(row, 1, 1, stride=1, stride_axis=0), neg)
    lo_scr[n_tiles] = neg
    hi_scr[n_tiles] = neg
    for (r0, c0, a_lo, a_hi) in plan:
        o_ref[r0:r0 + w, c0:c0 + LANES] = jnp.where(in_lo, lo_scr[a_lo], hi_scr[a_hi])


def _nattn_kernel(q_ref, k_ref, v_ref, ck_ref, cv_ref, g_ref, o_ref, bias_ref, lo_scr, hi_scr,
                  *, groups, plan):
    @pl.when(pl.program_id(1) == 0)
    def _():
        for e in range(HEADS_PER_BLOCK):
            _fill_na_bias(g_ref.at[e], bias_ref.at[e], lo_scr, hi_scr, plan)

    q2, k2, v2 = q_ref[0], k_ref[0], v_ref[0]
    past = ck_ref.shape[-1]
    kct = ck_ref[0, 0].reshape(LANES, past).astype(BF16)
    vct = cv_ref[0, 0].reshape(LANES, past).astype(BF16)
    q2 = q2 * ATTN_SCALE
    nq = NA_QROWS * GRID_W
    dn = (((1,), (1,)), ((), ()))
    units = [(grp, e) for grp in groups for e in range(HEADS_PER_BLOCK)]
    s_ctx = [jnp.dot(_head_q(q2, e), kct, preferred_element_type=F32)
             for e in range(HEADS_PER_BLOCK)]

    def scores(unit):
        (q0, k0, nk, boff), e = unit
        qe = _head_q(q2[q0:q0 + nq], e)
        s_w = (lax.dot_general(qe, k2[k0:k0 + nk], dn, preferred_element_type=F32)
               + bias_ref[e, :, boff:boff + nk])
        return s_w, s_ctx[e][q0:q0 + nq]

    def attend(unit, s_w, s_c):
        (_, k0, nk, _), e = unit
        mx = jnp.maximum(jnp.max(s_w, axis=-1, keepdims=True),
                         jnp.max(s_c, axis=-1, keepdims=True))
        return (jnp.dot(jnp.exp(s_w - mx).astype(BF16), _head_v(v2[k0:k0 + nk], e),
                        preferred_element_type=F32)
                + lax.dot_general(jnp.exp(s_c - mx).astype(BF16), _head_v(vct, e, axis=0), dn,
                                  preferred_element_type=F32))

    accs = []
    for unit, s in _pipelined(units, scores):
        accs.append(attend(unit, *s))
        if len(accs) == HEADS_PER_BLOCK:
            q0 = unit[0][0]
            o_ref[0, q0:q0 + nq, :] = _normalise(accs).astype(BF16)
            accs = []


def _nattn(qkv, cache_kt, cache_vt, rpb, *, layer):
    b, t, _ = qkv.shape
    past = cache_kt.shape[4]
    groups, _, _ = _na_groups(t // GRID_W)
    plan, nbias = _na_bias_plan(t // GRID_W)
    nhp = N_HEADS // HEADS_PER_BLOCK
    nq = NA_QROWS * GRID_W
    n_tiles = 2 * NA_ROWS
    cache_spec = pl.BlockSpec((1, 1, HEADS_PER_BLOCK, HEAD_DIM, past),
                              lambda hp, bi: (bi, layer, hp, 0, 0))
    return pl.pallas_call(
        functools.partial(_nattn_kernel, groups=groups, plan=plan),
        grid=(nhp, b),
        in_specs=[
            pl.BlockSpec((1, t, LANES), lambda hp, bi: (bi, 0, hp)),
            pl.BlockSpec((1, t, LANES), lambda hp, bi: (bi, 0, nhp + hp)),
            pl.BlockSpec((1, t, LANES), lambda hp, bi: (bi, 0, 2 * nhp + hp)),
            cache_spec,
            cache_spec,
            pl.BlockSpec((HEADS_PER_BLOCK, n_tiles, LANES), lambda hp, bi: (hp, 0, 0)),
        ],
        out_specs=pl.BlockSpec((1, t, LANES), lambda hp, bi: (bi, 0, hp)),
        out_shape=jax.ShapeDtypeStruct((b, t, D_MODEL), BF16),
        scratch_shapes=[pltpu.VMEM((HEADS_PER_BLOCK, nq, nbias), F32),
                        pltpu.VMEM((n_tiles, GRID_W, LANES), F32),
                        pltpu.VMEM((n_tiles, GRID_W, LANES), F32)],
        compiler_params=_params(2),
        name="nattn",
    )(qkv, qkv, qkv, cache_kt, cache_vt, _pad_rpb(rpb))


GELU_C = float(np.float32(np.sqrt(2 / np.pi)))
GELU_K = -2.0 * GELU_C * math.log2(math.e)


def _gelu(x):
    return x / (1.0 + jnp.exp2(x * (x * x * (GELU_K * 0.044715) + GELU_K)))


def _gmlp_kernel(x_ref, mod_ref, g_ref, win_ref, gv_ref, ws_ref, bs_ref, *rest, layer, tm, n_side):
    side_in, (o_ref,), side_out, (vv_scr,) = (
        rest[:n_side], rest[n_side:n_side + 1], rest[n_side + 1:2 * n_side + 1], rest[2 * n_side + 1:])
    _side_cast(side_in, side_out)
    e = GMLP_WIDTH
    gd = e // GMLP_GROUPS
    mod = mod_ref[0]
    h = _rms_mod(x_ref[...], g_ref[layer:layer + 1, :], _mod_chunk(mod, 0), _mod_chunk(mod, 1)).astype(BF16)
    pair = 2 * gd

    def u_proj(gp):
        return jnp.dot(h, win_ref[:, gp * pair:(gp + 1) * pair], preferred_element_type=F32)

    zv = _gelu(jnp.dot(h, win_ref[:, e:2 * e], preferred_element_type=F32))
    pipe = _pipelined(list(range(e // pair)), u_proj)
    head = next(pipe)
    xc = zv - jnp.mean(zv, axis=-1, keepdims=True)
    vv = xc * lax.rsqrt(jnp.mean(xc * xc, axis=-1, keepdims=True) + EPS) * gv_ref[...]
    vv_scr[...] = vv.astype(BF16)
    for gp, u_raw in itertools.chain([head], pipe):
        u2 = _gelu(u_raw)
        for gi in range(2):
            g = 2 * gp + gi
            cols = slice(g * gd, (g + 1) * gd)
            nc = tm // CHUNK
            vcat = jnp.concatenate(
                [vv_scr[c * CHUNK:(c + 1) * CHUNK, cols] for c in range(nc)], axis=1)
            mixed = jnp.dot(ws_ref[g].astype(BF16), vcat, preferred_element_type=F32)
            for c in range(nc):
                rows = slice(c * CHUNK, (c + 1) * CHUNK)
                o_ref[rows, cols] = (u2[rows, gi * gd:(gi + 1) * gd]
                                     * (mixed[:, c * gd:(c + 1) * gd] + bs_ref[g])).astype(BF16)


def _gmlp(x, mods3, mod_row_fn, g, w_in, g_v, w_s, b_s_full, side=(), *, layer, tm):
    n, d = x.shape
    e = GMLP_WIDTH
    side_in, side_out, side_shapes = _side_cast_specs(side, n // tm)
    return pl.pallas_call(
        functools.partial(_gmlp_kernel, layer=layer, tm=tm, n_side=len(side)),
        grid=(n // tm,),
        in_specs=[
            pl.BlockSpec((tm, d), lambda i: (i, 0)),
            pl.BlockSpec((1, 1, 6 * d), lambda i: (mod_row_fn(i), 0, 0)),
            _resident(g.shape),
            _resident((d, 2 * e)),
            _resident((1, e)),
            _resident(w_s.shape),
            _resident(b_s_full.shape),
        ] + side_in,
        out_specs=[pl.BlockSpec((tm, e), lambda i: (i, 0))] + side_out,
        out_shape=[jax.ShapeDtypeStruct((n, e), BF16)] + side_shapes,
        scratch_shapes=[pltpu.VMEM((tm, e), BF16)],
        compiler_params=_params(),
        name="gmlp",
    )(x, mods3, g, w_in, g_v, w_s, b_s_full, *(w for w, _ in side))


def _conv_pieces(tm, seq, halo):
    if halo:
        return ((0, tm + 2 * HALO, HALO, tm),)
    return tuple((s * seq, seq, 0, seq) for s in range(tm // seq))


def _mix_ffn_kernel(x_ref, xp_ref, xn_ref, m_ref, mp_ref, mn_ref, mod_ref, gffn_ref, gfin_ref,
                    wmo_ref, wup_ref, wconv_ref, bconv_ref, wdown_ref, *rest,
                    layer, tm, seq, halo, final, n_side):
    side_in, (o_ref,), side_out, (act_scr,) = (
        rest[:n_side], rest[n_side:n_side + 1], rest[n_side + 1:2 * n_side + 1], rest[2 * n_side + 1:])
    _side_cast(side_in, side_out)
    mod = mod_ref[0]
    gate_mix, shift, scale, gate_ffn = (_mod_chunk(mod, k) for k in (2, 3, 4, 5))
    if halo:
        xcat = jnp.concatenate([xp_ref[0], x_ref[...], xn_ref[0]], axis=0)
        mcat = jnp.concatenate([mp_ref[0], m_ref[...], mn_ref[0]], axis=0)
        main = slice(HALO, HALO + tm)
    else:
        xcat, mcat = x_ref[...], m_ref[...]
        main = slice(0, tm)
    r = xcat.shape[0]
    halves = (slice(0, r // 2), slice(r // 2, r))
    x1_parts = [xcat[rows] + gate_mix * jnp.dot(mcat[rows], wmo_ref[...], preferred_element_type=F32)
                for rows in halves]
    x1 = jnp.concatenate(x1_parts, axis=0)
    hcat = jnp.concatenate(
        [_rms_mod(part, gffn_ref[layer:layer + 1, :], shift, scale).astype(BF16)
         for part in x1_parts], axis=0)

    f = FFN_TILE
    if halo:
        blk = pl.program_id(0) % (seq // tm)
        hcat = jnp.concatenate([
            jnp.where(blk == 0, jnp.zeros((HALO, D_MODEL), BF16), hcat[:HALO]),
            hcat[main],
            jnp.where(blk == seq // tm - 1, jnp.zeros((HALO, D_MODEL), BF16), hcat[HALO + tm:]),
        ], axis=0)
    pieces = _conv_pieces(tm, seq, halo)
    sub = lax.broadcasted_iota(jnp.int32, (PAD, 1), 0)

    def up(c):
        return [jnp.dot(hcat, wup_ref[:, col:col + f], preferred_element_type=F32)
                for col in (c * f, FFN_DIM + c * f)]

    def conv(a, col):
        cols = slice(col, col + f)
        wc = wconv_ref[0, :, cols]
        out = []
        for (a0, n_rows, own0, own) in pieces:
            piece = a[a0:a0 + n_rows]
            own_rows = slice(own0, own0 + own)
            prev = pltpu.roll(piece, 1, 0)[own_rows]
            nxt = pltpu.roll(piece, n_rows - 1, 0)[own_rows]
            if not halo:
                prev = jnp.concatenate([jnp.where(sub == 0, 0.0, prev[:PAD]), prev[PAD:]], axis=0)
                nxt = jnp.concatenate([nxt[:-PAD], jnp.where(sub == PAD - 1, 0.0, nxt[-PAD:])], axis=0)
            out.append(prev * wc[0:1] + piece[own_rows] * wc[1:2] + nxt * wc[2:3]
                       + bconv_ref[layer:layer + 1, cols])
        return out[0] if len(out) == 1 else jnp.concatenate(out, axis=0)

    for c, (a_gate, a_val) in _pipelined(list(range(FFN_DIM // f)), up):
        act = jax.nn.silu(conv(a_gate, c * f)) * conv(a_val, FFN_DIM + c * f)
        act_scr[:, c * f:(c + 1) * f] = act.astype(BF16)
    acc = jnp.dot(act_scr[...], wdown_ref[...], preferred_element_type=F32)
    out = x1[main] + gate_ffn * acc
    if final:
        out = out * lax.rsqrt(jnp.mean(out * out, axis=-1, keepdims=True) + EPS) * gfin_ref[...]
    o_ref[...] = out


def _mix_ffn(x, m, mods3, mod_row_fn, g_ffn, g_fin, w_mo, w_up, w_conv, b_conv, w_down,
             side=(), *, layer, tm, seq, final):
    n, d = x.shape
    km = m.shape[1]
    halo = tm % seq != 0
    assert not halo or seq % tm == 0
    nh = n // HALO
    per = tm // HALO
    prev_map = lambda i: (jnp.maximum(i * per - 1, 0), 0, 0)
    next_map = lambda i: (jnp.minimum((i + 1) * per, nh - 1), 0, 0)
    side_in, side_out, side_shapes = _side_cast_specs(side, n // tm)
    kern = functools.partial(_mix_ffn_kernel, layer=layer, tm=tm, seq=seq, halo=halo, final=final,
                             n_side=len(side))
    return pl.pallas_call(
        kern,
        grid=(n // tm,),
        in_specs=[
            pl.BlockSpec((tm, d), lambda i: (i, 0)),
            pl.BlockSpec((1, HALO, d), prev_map),
            pl.BlockSpec((1, HALO, d), next_map),
            pl.BlockSpec((tm, km), lambda i: (i, 0)),
            pl.BlockSpec((1, HALO, km), prev_map),
            pl.BlockSpec((1, HALO, km), next_map),
            pl.BlockSpec((1, 1, 6 * d), lambda i: (mod_row_fn(i), 0, 0)),
            _resident(g_ffn.shape),
            _resident((1, d)),
            _resident(w_mo.shape),
            _resident(w_up.shape),
            _resident_layer(w_conv.shape, layer),
            _resident(b_conv.shape),
            _resident(w_down.shape),
        ] + side_in,
        out_specs=[pl.BlockSpec((tm, d), lambda i: (i, 0))] + side_out,
        out_shape=[jax.ShapeDtypeStruct((n, d), F32)] + side_shapes,
        scratch_shapes=[pltpu.VMEM((tm, FFN_DIM), BF16)],
        compiler_params=_params(),
        name="mix_ffn",
    )(x, x.reshape(nh, HALO, d), x.reshape(nh, HALO, d),
      m, m.reshape(nh, HALO, km), m.reshape(nh, HALO, km),
      mods3, g_ffn, g_fin, w_mo, w_up, w_conv, b_conv, w_down, *(w for w, _ in side))


def kernel(x_prompt, x_sample, cache_k, cache_v, c, c_ctx, w_ada, b_ada, norm_mix_g,
           norm_ffn_g, norm_final_g, w_qkv, w_attn_out, rpb, w_gmlp_in, g_gmlp_v,
           w_spatial, b_spatial, w_gmlp_out, w_ffn_up, w_ffn_conv, b_ffn_conv, w_ffn_down):
    batch, seq, d = x_prompt.shape
    dec_batch, dec_seq, _ = x_sample.shape
    depth = w_ada.shape[0]
    assert depth == 2 and d == D_MODEL

    cond = jnp.zeros((MOD_ROWS, d), F32).at[0].set(c_ctx).at[1:1 + dec_batch].set(c)
    mods3 = _ada(cond, w_ada, b_ada).reshape(depth * MOD_ROWS, 1, 6 * d)

    w_qkv_b = _to_bf16(w_qkv[0])
    w_sp = w_spatial[0]
    b_s_full = jnp.broadcast_to(b_spatial[0][:, :, None], (GMLP_GROUPS, CHUNK, CHUNK))
    conv_w = (w_ffn_conv, b_ffn_conv)
    g_mix, g_ffn = norm_mix_g, norm_ffn_g
    g_fin = norm_final_g.reshape(1, d)
    tm = 512

    def ctx_row(layer):
        return lambda i: layer * MOD_ROWS
    x = x_prompt.reshape(batch * seq, d)
    o, state_kt, state_vt, w_ao, w_up0, w_down0 = _ctx_attn(
        x, mods3, 0, g_mix, w_qkv_b, [(w_attn_out, 0), (w_ffn_up, 0), (w_ffn_down, 0)],
        layer=0, batch=batch, seq=seq)
    state_k, state_v = jnp.swapaxes(state_kt, 3, 4), jnp.swapaxes(state_vt, 3, 4)
    x, w_gin = _mix_ffn(x, o, mods3, ctx_row(0), g_ffn, g_fin, w_ao, w_up0, *conv_w, w_down0,
                        [(w_gmlp_in, 0)], layer=0, tm=tm, seq=seq, final=False)
    m, w_gout, w_up1, w_down1 = _gmlp(
        x, mods3, ctx_row(1), g_mix, w_gin, g_gmlp_v, w_sp, b_s_full,
        [(w_gmlp_out, 0), (w_ffn_up, 1), (w_ffn_down, 1)], layer=1, tm=tm)
    y_prompt, = _mix_ffn(x, m, mods3, ctx_row(1), g_ffn, g_fin, w_gout, w_up1, *conv_w, w_down1,
                         layer=1, tm=tm, seq=seq, final=True)

    def lat_row(layer):
        return lambda i: layer * MOD_ROWS + 1 + (i * tm) // dec_seq
    x = x_sample.reshape(dec_batch * dec_seq, d)
    qkv = _qkv(x, mods3, lat_row(0), g_mix, w_qkv_b, layer=0, tm=tm)
    o = _nattn(qkv.reshape(dec_batch, dec_seq, 3 * d), jnp.swapaxes(cache_k, 3, 4),
               jnp.swapaxes(cache_v, 3, 4), rpb[0], layer=0)
    x, = _mix_ffn(x, o.reshape(dec_batch * dec_seq, d), mods3, lat_row(0), g_ffn, g_fin,
                  w_ao, w_up0, *conv_w, w_down0, layer=0, tm=tm, seq=dec_seq, final=False)
    m, = _gmlp(x, mods3, lat_row(1), g_mix, w_gin, g_gmlp_v, w_sp, b_s_full, layer=1, tm=tm)
    y_sample, = _mix_ffn(x, m, mods3, lat_row(1), g_ffn, g_fin, w_gout, w_up1, *conv_w, w_down1,
                         layer=1, tm=tm, seq=dec_seq, final=True)

    return (y_prompt.reshape(batch, seq, d), y_sample.reshape(dec_batch, dec_seq, d),
            state_k, state_v)
```

```python
import functools
import itertools
import math

import numpy as np
import jax
import jax.numpy as jnp
from jax import lax
from jax.experimental import pallas as pl
from jax.experimental.pallas import tpu as pltpu

D_MODEL = 1024
N_HEADS = 16
HEAD_DIM = D_MODEL // N_HEADS
GRID_W = 64
NA_ROWS = 8
NA_COLS = 16
CHUNK = 128
GMLP_WIDTH = 2 * D_MODEL
GMLP_GROUPS = 16
FFN_DIM = 2816
CONV_W = 3
EPS = 1e-6
ATTN_SCALE = HEAD_DIM ** -0.5
assert math.frexp(ATTN_SCALE)[0] == 0.5

LANES = 128
HEADS_PER_BLOCK = LANES // HEAD_DIM
MOD_ROWS = 8
HALO = 16
PAD = 8
FFN_TILE = 256
NA_QROWS = 4
LOOKAHEAD = 1
NEG_BIAS = -1e30
VMEM_LIMIT = 56 * 1024 * 1024

BF16 = jnp.bfloat16
F32 = jnp.float32


def _params(n_axes=1):
    return pltpu.CompilerParams(
        dimension_semantics=("arbitrary",) * n_axes, vmem_limit_bytes=VMEM_LIMIT)


def _resident(shape):
    nd = len(shape)
    return pl.BlockSpec(shape, lambda *_: (0,) * nd, pipeline_mode=pl.Buffered(1))


def _resident_layer(shape, layer):
    nd = len(shape)
    return pl.BlockSpec((1,) + tuple(shape[1:]), lambda *_: (layer,) + (0,) * (nd - 1),
                        pipeline_mode=pl.Buffered(1))


def _pipelined(items, first_stage):
    ready = [first_stage(it) for it in items[:LOOKAHEAD]]
    for i, item in enumerate(items):
        if i + LOOKAHEAD < len(items):
            ready.append(first_stage(items[i + LOOKAHEAD]))
        yield item, ready.pop(0)


def _side_cast_specs(weights, n_steps):
    ins, outs, shapes = [], [], []
    for w, layer in weights:
        _, rows, cols = w.shape
        assert rows % (n_steps * HALO) == 0
        ins.append(pl.BlockSpec((1, rows // n_steps, cols), lambda i, layer=layer: (layer, i, 0)))
        outs.append(pl.BlockSpec((rows // n_steps, cols), lambda i: (i, 0)))
        shapes.append(jax.ShapeDtypeStruct((rows, cols), BF16))
    return ins, outs, shapes


def _side_cast(src_refs, dst_refs):
    for src, dst in zip(src_refs, dst_refs):
        dst[...] = src[0].astype(BF16)


def _rms_mod(x, g, shift, scale):
    y = x * lax.rsqrt(jnp.mean(x * x, axis=-1, keepdims=True) + EPS)
    return (y * g) * (1 + scale) + shift


def _mod_chunk(mod, k):
    return mod[:, k * D_MODEL:(k + 1) * D_MODEL]


CAST_BLOCK_BYTES = 8 * 1024 * 1024


def _cast_kernel(w_ref, o_ref):
    o_ref[...] = w_ref[...].astype(o_ref.dtype)


def _to_bf16(w):
    shape = w.shape
    cols = shape[-1]
    rows = w.size // cols
    per = HALO
    assert rows % per == 0
    blk = max(b for b in range(per, rows + 1, per)
              if rows % b == 0 and (b * cols * 4 <= CAST_BLOCK_BYTES or b == per))
    out = pl.pallas_call(
        _cast_kernel,
        grid=(rows // blk,),
        in_specs=[pl.BlockSpec((blk, cols), lambda i: (i, 0))],
        out_specs=pl.BlockSpec((blk, cols), lambda i: (i, 0)),
        out_shape=jax.ShapeDtypeStruct((rows, cols), BF16),
        compiler_params=_params(),
        name="to_bf16",
    )(w.reshape(rows, cols))
    return out.reshape(shape)


def _ada_kernel(cond_ref, w_ref, b_ref, o_ref):
    s = jax.nn.silu(cond_ref[...]).astype(BF16)
    bias = b_ref[pl.ds(pl.program_id(0), 1), :]
    mod = jnp.dot(s, w_ref[0].astype(BF16), preferred_element_type=F32) + bias
    for r in range(MOD_ROWS):
        o_ref[r] = mod[r:r + 1, :]


def _ada(cond, w_ada, b_ada):
    depth, d, n = w_ada.shape
    tn = n // 4
    return pl.pallas_call(
        _ada_kernel,
        grid=(depth, n // tn),
        in_specs=[
            pl.BlockSpec((MOD_ROWS, d), lambda l, j: (0, 0)),
            pl.BlockSpec((1, d, tn), lambda l, j: (l, 0, j)),
            pl.BlockSpec((depth, tn), lambda l, j: (0, j)),
        ],
        out_specs=pl.BlockSpec((MOD_ROWS, 1, tn), lambda l, j: (l, 0, j)),
        out_shape=jax.ShapeDtypeStruct((depth * MOD_ROWS, 1, n), F32),
        compiler_params=_params(2),
        name="ada",
    )(cond, w_ada, b_ada)


def _den_lane(e):
    return (1 - e) * HEAD_DIM


def _head_lanes(e, axis=1):
    ch = lax.broadcasted_iota(jnp.int32, (1, LANES) if axis == 1 else (LANES, 1), axis)
    own = (ch < HEAD_DIM) if e == 0 else (ch >= HEAD_DIM)
    return own, ch == _den_lane(e)


def _head_q(q2, e):
    own, _ = _head_lanes(e)
    return jnp.where(own, q2, jnp.zeros_like(q2))


def _head_v(v2, e, axis=1):
    own, is_den = _head_lanes(e, axis)
    return jnp.where(own, v2, jnp.broadcast_to(is_den.astype(v2.dtype), v2.shape))


def _normalise(accs):
    outs = [acc / acc[:, _den_lane(e):_den_lane(e) + 1] for e, acc in enumerate(accs)]
    own0, _ = _head_lanes(0)
    return jnp.where(own0, outs[0], outs[1])


def _ctx_attn_kernel(x_ref, mod_ref, g_ref, w_ref, *rest, layer, nb, seq, n_side):
    side_in, (o_ref, sk_ref, sv_ref), side_out, (qkv_scr,) = (
        rest[:n_side], rest[n_side:n_side + 3], rest[n_side + 3:2 * n_side + 3], rest[2 * n_side + 3:])
    _side_cast(side_in, side_out)
    mod = mod_ref[0]
    h = _rms_mod(x_ref[...], g_ref[layer:layer + 1, :], _mod_chunk(mod, 0), _mod_chunk(mod, 1)).astype(BF16)
    qkv_scr[...] = jnp.dot(h, w_ref[...], preferred_element_type=F32)
    for b in range(nb):
        rows = slice(b * seq, (b + 1) * seq)
        for hp in range(N_HEADS // HEADS_PER_BLOCK):
            heads = slice(hp * HEADS_PER_BLOCK, (hp + 1) * HEADS_PER_BLOCK)
            for part, s_ref in ((1, sk_ref), (2, sv_ref)):
                cols = slice(part * D_MODEL + hp * LANES, part * D_MODEL + (hp + 1) * LANES)
                s_ref[b, 0, heads] = qkv_scr[rows, cols].T.reshape(HEADS_PER_BLOCK, HEAD_DIM, seq)

    def block(unit, part):
        b, hp = unit
        return (slice(b * seq, (b + 1) * seq),
                slice(part * D_MODEL + hp * LANES, part * D_MODEL + (hp + 1) * LANES))

    def scores(unit):
        q2 = (qkv_scr[block(unit, 0)] * ATTN_SCALE).astype(BF16)
        k2 = qkv_scr[block(unit, 1)].astype(BF16)
        qs = jnp.concatenate([_head_q(q2, e) for e in range(HEADS_PER_BLOCK)], axis=0)
        return lax.dot_general(qs, k2, (((1,), (1,)), ((), ())), preferred_element_type=F32)

    units = [(b, hp) for b in range(nb) for hp in range(N_HEADS // HEADS_PER_BLOCK)]
    for unit, s in _pipelined(units, scores):
        v2 = qkv_scr[block(unit, 2)].astype(BF16)
        p = jnp.exp(s - jnp.max(s, axis=-1, keepdims=True))
        o = (jnp.dot(p.astype(BF16), v2, preferred_element_type=F32)
             / jnp.sum(p, axis=-1, keepdims=True))
        own0, _ = _head_lanes(0)
        o_ref[block(unit, 0)] = jnp.where(own0, o[:seq], o[seq:]).astype(BF16)


def _ctx_attn(x, mods3, mod_row, g, w_qkv, side, *, layer, batch, seq):
    n, d = x.shape
    nb = 2
    tm = nb * seq
    side_in, side_out, side_shapes = _side_cast_specs(side, n // tm)
    kern = functools.partial(_ctx_attn_kernel, layer=layer, nb=nb, seq=seq, n_side=len(side))
    state = jax.ShapeDtypeStruct((batch, 1, N_HEADS, HEAD_DIM, seq), F32)
    state_spec = pl.BlockSpec((nb, 1, N_HEADS, HEAD_DIM, seq), lambda i: (i, 0, 0, 0, 0))
    return pl.pallas_call(
        kern,
        grid=(n // tm,),
        in_specs=[
            pl.BlockSpec((tm, d), lambda i: (i, 0)),
            pl.BlockSpec((1, 1, 6 * d), lambda i: (mod_row, 0, 0)),
            _resident(g.shape),
            _resident((d, 3 * d)),
        ] + side_in,
        out_specs=[pl.BlockSpec((tm, d), lambda i: (i, 0)), state_spec, state_spec] + side_out,
        out_shape=[jax.ShapeDtypeStruct((n, d), BF16), state, state] + side_shapes,
        scratch_shapes=[pltpu.VMEM((tm, 3 * d), F32)],
        compiler_params=_params(),
        name="ctx_attn",
    )(x, mods3, g, w_qkv, *(w for w, _ in side))


def _qkv_kernel(x_ref, mod_ref, g_ref, w_ref, o_ref, *, layer, tn):
    mod = mod_ref[0]
    h = _rms_mod(x_ref[...], g_ref[layer:layer + 1, :], _mod_chunk(mod, 0), _mod_chunk(mod, 1)).astype(BF16)
    for c in range(w_ref.shape[1] // tn):
        cols = slice(c * tn, (c + 1) * tn)
        o_ref[:, cols] = jnp.dot(h, w_ref[:, cols], preferred_element_type=F32).astype(BF16)


def _qkv(x, mods3, mod_row_fn, g, w_qkv, *, layer, tm):
    n, d = x.shape
    nout = w_qkv.shape[1]
    return pl.pallas_call(
        functools.partial(_qkv_kernel, layer=layer, tn=512),
        grid=(n // tm,),
        in_specs=[
            pl.BlockSpec((tm, d), lambda i: (i, 0)),
            pl.BlockSpec((1, 1, 6 * d), lambda i: (mod_row_fn(i), 0, 0)),
            _resident(g.shape),
            _resident((d, nout)),
        ],
        out_specs=pl.BlockSpec((tm, nout), lambda i: (i, 0)),
        out_shape=jax.ShapeDtypeStruct((n, nout), BF16),
        compiler_params=_params(),
        name="qkv",
    )(x, mods3, g, w_qkv)


def _na_groups(rows):
    kh = min(NA_ROWS, rows)
    row_start = [min(max(r - kh // 2, 0), rows - kh) for r in range(rows)]
    groups, off = [], 0
    for r0 in range(0, rows, NA_QROWS):
        lo = min(row_start[r0:r0 + NA_QROWS])
        hi = max(row_start[r0:r0 + NA_QROWS]) + kh
        if (hi - lo) * GRID_W % LANES:
            if hi < rows:
                hi += 1
            else:
                lo -= 1
        nk = (hi - lo) * GRID_W
        groups.append((r0 * GRID_W, lo * GRID_W, nk, off))
        off += nk
    return tuple(groups), row_start, kh


def _na_bias_plan(rows):
    groups, row_start, kh = _na_groups(rows)
    nro = 2 * NA_ROWS - 1
    w = GRID_W
    assert 2 * w == LANES
    plan = []
    for (q0, k0, nk, off) in groups:
        for qi in range(NA_QROWS):
            qr = q0 // w + qi
            rs = row_start[qr]
            tile_of = lambda kr: kr - qr + NA_ROWS - 1 if rs <= kr < rs + kh else nro
            for p in range(nk // LANES):
                kr = k0 // w + 2 * p
                plan.append((qi * w, off + p * LANES, tile_of(kr), tile_of(kr + 1)))
    return tuple(plan), sum(g[2] for g in groups)


def _pad_rpb(rpb):
    lead = GRID_W - NA_COLS
    return jnp.pad(rpb, ((0, 0), (0, 1), (lead, LANES - rpb.shape[2] - lead)))


def _fill_na_bias(g_ref, o_ref, lo_scr, hi_scr, plan):
    w = GRID_W
    n_tiles = 2 * NA_ROWS - 1
    lane = lax.broadcasted_iota(jnp.int32, (w, LANES), 1)
    qc = lax.broadcasted_iota(jnp.int32, (w, LANES), 0)
    kc = lane % w
    col_start = jnp.clip(qc - NA_COLS // 2, 0, w - NA_COLS)
    col_ok = (kc >= col_start) & (kc < col_start + NA_COLS)
    in_lo = lane < w
    neg = jnp.full((w, LANES), NEG_BIAS, F32)
    for a in range(n_tiles):
        row = jnp.broadcast_to(g_ref[a:a + 1, :], (w, LANES))
        lo_scr[a] = jnp.where(col_ok, pltpu.roll(row, w + 1, 1, stride=1, stride_axis=0), neg)
        hi_scr[a] = jnp.where(col_ok, pltpu.roll(row, 1, 1, stride=1, stride_axis=0), neg)
    lo_scr[n_tiles] = neg
    hi_scr[n_tiles] = neg
    for (r0, c0, a_lo, a_hi) in plan:
        o_ref[r0:r0 + w, c0:c0 + LANES] = jnp.where(in_lo, lo_scr[a_lo], hi_scr[a_hi])


def _nattn_kernel(q_ref, k_ref, v_ref, ck_ref, cv_ref, g_ref, o_ref, bias_ref, lo_scr, hi_scr,
                  *, groups, plan):
    @pl.when(pl.program_id(1) == 0)
    def _():
        for e in range(HEADS_PER_BLOCK):
            _fill_na_bias(g_ref.at[e], bias_ref.at[e], lo_scr, hi_scr, plan)

    q2, k2, v2 = q_ref[0], k_ref[0], v_ref[0]
    past = ck_ref.shape[-1]
    kct = ck_ref[0, 0].reshape(LANES, past).astype(BF16)
    vct = cv_ref[0, 0].reshape(LANES, past).astype(BF16)
    q2 = q2 * ATTN_SCALE
    nq = NA_QROWS * GRID_W
    dn = (((1,), (1,)), ((), ()))
    units = [(grp, e) for grp in groups for e in range(HEADS_PER_BLOCK)]
    s_ctx = [jnp.dot(_head_q(q2, e), kct, preferred_element_type=F32)
             for e in range(HEADS_PER_BLOCK)]

    def scores(unit):
        (q0, k0, nk, boff), e = unit
        qe = _head_q(q2[q0:q0 + nq], e)
        s_w = (lax.dot_general(qe, k2[k0:k0 + nk], dn, preferred_element_type=F32)
               + bias_ref[e, :, boff:boff + nk])
        return s_w, s_ctx[e][q0:q0 + nq]

    def attend(unit, s_w, s_c):
        (_, k0, nk, _), e = unit
        mx = jnp.maximum(jnp.max(s_w, axis=-1, keepdims=True),
                         jnp.max(s_c, axis=-1, keepdims=True))
        return (jnp.dot(jnp.exp(s_w - mx).astype(BF16), _head_v(v2[k0:k0 + nk], e),
                        preferred_element_type=F32)
                + lax.dot_general(jnp.exp(s_c - mx).astype(BF16), _head_v(vct, e, axis=0), dn,
                                  preferred_element_type=F32))

    accs = []
    for unit, s in _pipelined(units, scores):
        accs.append(attend(unit, *s))
        if len(accs) == HEADS_PER_BLOCK:
            q0 = unit[0][0]
            o_ref[0, q0:q0 + nq, :] = _normalise(accs).astype(BF16)
            accs = []


def _nattn(qkv, cache_kt, cache_vt, rpb, *, layer):
    b, t, _ = qkv.shape
    past = cache_kt.shape[4]
    groups, _, _ = _na_groups(t // GRID_W)
    plan, nbias = _na_bias_plan(t // GRID_W)
    nhp = N_HEADS // HEADS_PER_BLOCK
    nq = NA_QROWS * GRID_W
    n_tiles = 2 * NA_ROWS
    cache_spec = pl.BlockSpec((1, 1, HEADS_PER_BLOCK, HEAD_DIM, past),
                              lambda hp, bi: (bi, layer, hp, 0, 0))
    return pl.pallas_call(
        functools.partial(_nattn_kernel, groups=groups, plan=plan),
        grid=(nhp, b),
        in_specs=[
            pl.BlockSpec((1, t, LANES), lambda hp, bi: (bi, 0, hp)),
            pl.BlockSpec((1, t, LANES), lambda hp, bi: (bi, 0, nhp + hp)),
            pl.BlockSpec((1, t, LANES), lambda hp, bi: (bi, 0, 2 * nhp + hp)),
            cache_spec,
            cache_spec,
            pl.BlockSpec((HEADS_PER_BLOCK, n_tiles, LANES), lambda hp, bi: (hp, 0, 0)),
        ],
        out_specs=pl.BlockSpec((1, t, LANES), lambda hp, bi: (bi, 0, hp)),
        out_shape=jax.ShapeDtypeStruct((b, t, D_MODEL), BF16),
        scratch_shapes=[pltpu.VMEM((HEADS_PER_BLOCK, nq, nbias), F32),
                        pltpu.VMEM((n_tiles, GRID_W, LANES), F32),
                        pltpu.VMEM((n_tiles, GRID_W, LANES), F32)],
        compiler_params=_params(2),
        name="nattn",
    )(qkv, qkv, qkv, cache_kt, cache_vt, _pad_rpb(rpb))


GELU_C = float(np.float32(np.sqrt(2 / np.pi)))
GELU_K = -2.0 * GELU_C * math.log2(math.e)


def _gelu(x):
    return x / (1.0 + jnp.exp2(x * (x * x * (GELU_K * 0.044715) + GELU_K)))


def _gmlp_kernel(x_ref, mod_ref, g_ref, win_ref, gv_ref, ws_ref, bs_ref, *rest, layer, tm, n_side):
    side_in, (o_ref,), side_out, (vv_scr,) = (
        rest[:n_side], rest[n_side:n_side + 1], rest[n_side + 1:2 * n_side + 1], rest[2 * n_side + 1:])
    _side_cast(side_in, side_out)
    e = GMLP_WIDTH
    gd = e // GMLP_GROUPS
    mod = mod_ref[0]
    h = _rms_mod(x_ref[...], g_ref[layer:layer + 1, :], _mod_chunk(mod, 0), _mod_chunk(mod, 1)).astype(BF16)
    pair = 2 * gd

    def u_proj(gp):
        return jnp.dot(h, win_ref[:, gp * pair:(gp + 1) * pair], preferred_element_type=F32)

    zv = _gelu(jnp.dot(h, win_ref[:, e:2 * e], preferred_element_type=F32))
    pipe = _pipelined(list(range(e // pair)), u_proj)
    head = next(pipe)
    xc = zv - jnp.mean(zv, axis=-1, keepdims=True)
    vv = xc * lax.rsqrt(jnp.mean(xc * xc, axis=-1, keepdims=True) + EPS) * gv_ref[...]
    vv_scr[...] = vv.astype(BF16)
    for gp, u_raw in itertools.chain([head], pipe):
        u2 = _gelu(u_raw)
        for gi in range(2):
            g = 2 * gp + gi
            cols = slice(g * gd, (g + 1) * gd)
            nc = tm // CHUNK
            vcat = jnp.concatenate(
                [vv_scr[c * CHUNK:(c + 1) * CHUNK, cols] for c in range(nc)], axis=1)
            mixed = jnp.dot(ws_ref[g].astype(BF16), vcat, preferred_element_type=F32)
            for c in range(nc):
                rows = slice(c * CHUNK, (c + 1) * CHUNK)
                o_ref[rows, cols] = (u2[rows, gi * gd:(gi + 1) * gd]
                                     * (mixed[:, c * gd:(c + 1) * gd] + bs_ref[g])).astype(BF16)


def _gmlp(x, mods3, mod_row_fn, g, w_in, g_v, w_s, b_s_full, side=(), *, layer, tm):
    n, d = x.shape
    e = GMLP_WIDTH
    side_in, side_out, side_shapes = _side_cast_specs(side, n // tm)
    return pl.pallas_call(
        functools.partial(_gmlp_kernel, layer=layer, tm=tm, n_side=len(side)),
        grid=(n // tm,),
        in_specs=[
            pl.BlockSpec((tm, d), lambda i: (i, 0)),
            pl.BlockSpec((1, 1, 6 * d), lambda i: (mod_row_fn(i), 0, 0)),
            _resident(g.shape),
            _resident((d, 2 * e)),
            _resident((1, e)),
            _resident(w_s.shape),
            _resident(b_s_full.shape),
        ] + side_in,
        out_specs=[pl.BlockSpec((tm, e), lambda i: (i, 0))] + side_out,
        out_shape=[jax.ShapeDtypeStruct((n, e), BF16)] + side_shapes,
        scratch_shapes=[pltpu.VMEM((tm, e), BF16)],
        compiler_params=_params(),
        name="gmlp",
    )(x, mods3, g, w_in, g_v, w_s, b_s_full, *(w for w, _ in side))


def _conv_pieces(tm, seq, halo):
    if halo:
        return ((0, tm + 2 * HALO, HALO, tm),)
    return tuple((s * seq, seq, 0, seq) for s in range(tm // seq))


def _mix_ffn_kernel(x_ref, xp_ref, xn_ref, m_ref, mp_ref, mn_ref, mod_ref, gffn_ref, gfin_ref,
                    wmo_ref, wup_ref, wconv_ref, bconv_ref, wdown_ref, *rest,
                    layer, tm, seq, halo, final, n_side):
    side_in, (o_ref,), side_out, (act_scr,) = (
        rest[:n_side], rest[n_side:n_side + 1], rest[n_side + 1:2 * n_side + 1], rest[2 * n_side + 1:])
    _side_cast(side_in, side_out)
    mod = mod_ref[0]
    gate_mix, shift, scale, gate_ffn = (_mod_chunk(mod, k) for k in (2, 3, 4, 5))
    if halo:
        xcat = jnp.concatenate([xp_ref[0], x_ref[...], xn_ref[0]], axis=0)
        mcat = jnp.concatenate([mp_ref[0], m_ref[...], mn_ref[0]], axis=0)
        main = slice(HALO, HALO + tm)
    else:
        xcat, mcat = x_ref[...], m_ref[...]
        main = slice(0, tm)
    r = xcat.shape[0]
    halves = (slice(0, r // 2), slice(r // 2, r))
    x1_parts = [xcat[rows] + gate_mix * jnp.dot(mcat[rows], wmo_ref[...], preferred_element_type=F32)
                for rows in halves]
    x1 = jnp.concatenate(x1_parts, axis=0)
    hcat = jnp.concatenate(
        [_rms_mod(part, gffn_ref[layer:layer + 1, :], shift, scale).astype(BF16)
         for part in x1_parts], axis=0)

    f = FFN_TILE
    if halo:
        blk = pl.program_id(0) % (seq // tm)
        hcat = jnp.concatenate([
            jnp.where(blk == 0, jnp.zeros((HALO, D_MODEL), BF16), hcat[:HALO]),
            hcat[main],
            jnp.where(blk == seq // tm - 1, jnp.zeros((HALO, D_MODEL), BF16), hcat[HALO + tm:]),
        ], axis=0)
    pieces = _conv_pieces(tm, seq, halo)
    sub = lax.broadcasted_iota(jnp.int32, (PAD, 1), 0)

    def up(c):
        return [jnp.dot(hcat, wup_ref[:, col:col + f], preferred_element_type=F32)
                for col in (c * f, FFN_DIM + c * f)]

    def conv(a, col):
        cols = slice(col, col + f)
        wc = wconv_ref[0, :, cols]
        out = []
        for (a0, n_rows, own0, own) in pieces:
            piece = a[a0:a0 + n_rows]
            own_rows = slice(own0, own0 + own)
            prev = pltpu.roll(piece, 1, 0)[own_rows]
            nxt = pltpu.roll(piece, n_rows - 1, 0)[own_rows]
            if not halo:
                prev = jnp.concatenate([jnp.where(sub == 0, 0.0, prev[:PAD]), prev[PAD:]], axis=0)
                nxt = jnp.concatenate([nxt[:-PAD], jnp.where(sub == PAD - 1, 0.0, nxt[-PAD:])], axis=0)
            out.append(prev * wc[0:1] + piece[own_rows] * wc[1:2] + nxt * wc[2:3]
                       + bconv_ref[layer:layer + 1, cols])
        return out[0] if len(out) == 1 else jnp.concatenate(out, axis=0)

    for c, (a_gate, a_val) in _pipelined(list(range(FFN_DIM // f)), up):
        act = jax.nn.silu(conv(a_gate, c * f)) * conv(a_val, FFN_DIM + c * f)
        act_scr[:, c * f:(c + 1) * f] = act.astype(BF16)
    acc = jnp.dot(act_scr[...], wdown_ref[...], preferred_element_type=F32)
    out = x1[main] + gate_ffn * acc
    if final:
        out = out * lax.rsqrt(jnp.mean(out * out, axis=-1, keepdims=True) + EPS) * gfin_ref[...]
    o_ref[...] = out


def _mix_ffn(x, m, mods3, mod_row_fn, g_ffn, g_fin, w_mo, w_up, w_conv, b_conv, w_down,
             side=(), *, layer, tm, seq, final):
    n, d = x.shape
    km = m.shape[1]
    halo = tm % seq != 0
    assert not halo or seq % tm == 0
    nh = n // HALO
    per = tm // HALO
    prev_map = lambda i: (jnp.maximum(i * per - 1, 0), 0, 0)
    next_map = lambda i: (jnp.minimum((i + 1) * per, nh - 1), 0, 0)
    side_in, side_out, side_shapes = _side_cast_specs(side, n // tm)
    kern = functools.partial(_mix_ffn_kernel, layer=layer, tm=tm, seq=seq, halo=halo, final=final,
                             n_side=len(side))
    return pl.pallas_call(
        kern,
        grid=(n // tm,),
        in_specs=[
            pl.BlockSpec((tm, d), lambda i: (i, 0)),
            pl.BlockSpec((1, HALO, d), prev_map),
            pl.BlockSpec((1, HALO, d), next_map),
            pl.BlockSpec((tm, km), lambda i: (i, 0)),
            pl.BlockSpec((1, HALO, km), prev_map),
            pl.BlockSpec((1, HALO, km), next_map),
            pl.BlockSpec((1, 1, 6 * d), lambda i: (mod_row_fn(i), 0, 0)),
            _resident(g_ffn.shape),
            _resident((1, d)),
            _resident(w_mo.shape),
            _resident(w_up.shape),
            _resident_layer(w_conv.shape, layer),
            _resident(b_conv.shape),
            _resident(w_down.shape),
        ] + side_in,
        out_specs=[pl.BlockSpec((tm, d), lambda i: (i, 0))] + side_out,
        out_shape=[jax.ShapeDtypeStruct((n, d), F32)] + side_shapes,
        scratch_shapes=[pltpu.VMEM((tm, FFN_DIM), BF16)],
        compiler_params=_params(),
        name="mix_ffn",
    )(x, x.reshape(nh, HALO, d), x.reshape(nh, HALO, d),
      m, m.reshape(nh, HALO, km), m.reshape(nh, HALO, km),
      mods3, g_ffn, g_fin, w_mo, w_up, w_conv, b_conv, w_down, *(w for w, _ in side))


def kernel(x_prompt, x_sample, cache_k, cache_v, c, c_ctx, w_ada, b_ada, norm_mix_g,
           norm_ffn_g, norm_final_g, w_qkv, w_attn_out, rpb, w_gmlp_in, g_gmlp_v,
           w_spatial, b_spatial, w_gmlp_out, w_ffn_up, w_ffn_conv, b_ffn_conv, w_ffn_down):
    batch, seq, d = x_prompt.shape
    dec_batch, dec_seq, _ = x_sample.shape
    depth = w_ada.shape[0]
    assert depth == 2 and d == D_MODEL

    cond = jnp.zeros((MOD_ROWS, d), F32).at[0].set(c_ctx).at[1:1 + dec_batch].set(c)
    mods3 = _ada(cond, w_ada, b_ada)

    w_qkv_b = _to_bf16(w_qkv[0])
    w_sp = w_spatial[0]
    b_s_full = jnp.broadcast_to(b_spatial[0][:, :, None], (GMLP_GROUPS, CHUNK, CHUNK))
    conv_w = (w_ffn_conv, b_ffn_conv)
    g_mix, g_ffn = norm_mix_g, norm_ffn_g
    g_fin = norm_final_g.reshape(1, d)
    tm = 512

    def ctx_row(layer):
        return lambda i: layer * MOD_ROWS
    x = x_prompt.reshape(batch * seq, d)
    o, state_kt, state_vt, w_ao, w_up0, w_down0 = _ctx_attn(
        x, mods3, 0, g_mix, w_qkv_b, [(w_attn_out, 0), (w_ffn_up, 0), (w_ffn_down, 0)],
        layer=0, batch=batch, seq=seq)
    state_k, state_v = jnp.swapaxes(state_kt, 3, 4), jnp.swapaxes(state_vt, 3, 4)
    x, w_gin = _mix_ffn(x, o, mods3, ctx_row(0), g_ffn, g_fin, w_ao, w_up0, *conv_w, w_down0,
                        [(w_gmlp_in, 0)], layer=0, tm=tm, seq=seq, final=False)
    m, w_gout, w_up1, w_down1 = _gmlp(
        x, mods3, ctx_row(1), g_mix, w_gin, g_gmlp_v, w_sp, b_s_full,
        [(w_gmlp_out, 0), (w_ffn_up, 1), (w_ffn_down, 1)], layer=1, tm=tm)
    y_prompt, = _mix_ffn(x, m, mods3, ctx_row(1), g_ffn, g_fin, w_gout, w_up1, *conv_w, w_down1,
                         layer=1, tm=tm, seq=seq, final=True)

    def lat_row(layer):
        return lambda i: layer * MOD_ROWS + 1 + (i * tm) // dec_seq
    x = x_sample.reshape(dec_batch * dec_seq, d)
    qkv = _qkv(x, mods3, lat_row(0), g_mix, w_qkv_b, layer=0, tm=tm)
    o = _nattn(qkv.reshape(dec_batch, dec_seq, 3 * d), jnp.swapaxes(cache_k, 3, 4),
               jnp.swapaxes(cache_v, 3, 4), rpb[0], layer=0)
    x, = _mix_ffn(x, o.reshape(dec_batch * dec_seq, d), mods3, lat_row(0), g_ffn, g_fin,
                  w_ao, w_up0, *conv_w, w_down0, layer=0, tm=tm, seq=dec_seq, final=False)
    m, = _gmlp(x, mods3, lat_row(1), g_mix, w_gin, g_gmlp_v, w_sp, b_s_full, layer=1, tm=tm)
    y_sample, = _mix_ffn(x, m, mods3, lat_row(1), g_ffn, g_fin, w_gout, w_up1, *conv_w, w_down1,
                         layer=1, tm=tm, seq=dec_seq, final=True)

    return (y_prompt.reshape(batch, seq, d), y_sample.reshape(dec_batch, dec_seq, d),
            state_k, state_v)
```

```python
import functools
import itertools
import math

import numpy as np
import jax
import jax.numpy as jnp
from jax import lax
from jax.experimental import pallas as pl
from jax.experimental.pallas import tpu as pltpu

D_MODEL = 1024
N_HEADS = 16
HEAD_DIM = D_MODEL // N_HEADS
GRID_W = 64
NA_ROWS = 8
NA_COLS = 16
CHUNK = 128
GMLP_WIDTH = 2 * D_MODEL
GMLP_GROUPS = 16
FFN_DIM = 2816
CONV_W = 3
EPS = 1e-6
ATTN_SCALE = HEAD_DIM ** -0.5
assert math.frexp(ATTN_SCALE)[0] == 0.5

LANES = 128
HEADS_PER_BLOCK = LANES // HEAD_DIM
MOD_ROWS = 8
HALO = 16
PAD = 8
FFN_TILE = 256
NA_QROWS = 4
NA_BATCH = 4
LOOKAHEAD = 1
NEG_BIAS = -1e30
VMEM_LIMIT = 56 * 1024 * 1024

BF16 = jnp.bfloat16
F32 = jnp.float32


def _params(n_axes=1):
    return pltpu.CompilerParams(
        dimension_semantics=("arbitrary",) * n_axes, vmem_limit_bytes=VMEM_LIMIT)


def _resident(shape):
    nd = len(shape)
    return pl.BlockSpec(shape, lambda *_: (0,) * nd, pipeline_mode=pl.Buffered(1))


def _resident_layer(shape, layer):
    nd = len(shape)
    return pl.BlockSpec((1,) + tuple(shape[1:]), lambda *_: (layer,) + (0,) * (nd - 1),
                        pipeline_mode=pl.Buffered(1))


def _pipelined(items, first_stage):
    ready = [first_stage(it) for it in items[:LOOKAHEAD]]
    for i, item in enumerate(items):
        if i + LOOKAHEAD < len(items):
            ready.append(first_stage(items[i + LOOKAHEAD]))
        yield item, ready.pop(0)


def _side_cast_specs(weights, n_steps):
    ins, outs, shapes = [], [], []
    for w, layer in weights:
        _, rows, cols = w.shape
        assert rows % (n_steps * HALO) == 0
        ins.append(pl.BlockSpec((1, rows // n_steps, cols), lambda i, layer=layer: (layer, i, 0)))
        outs.append(pl.BlockSpec((rows // n_steps, cols), lambda i: (i, 0)))
        shapes.append(jax.ShapeDtypeStruct((rows, cols), BF16))
    return ins, outs, shapes


def _side_cast(src_refs, dst_refs):
    for src, dst in zip(src_refs, dst_refs):
        dst[...] = src[0].astype(BF16)


def _rms_mod(x, g, shift, scale):
    y = x * lax.rsqrt(jnp.mean(x * x, axis=-1, keepdims=True) + EPS)
    return (y * g) * (1 + scale) + shift


def _mod_chunk(mod, k):
    return mod[:, k * D_MODEL:(k + 1) * D_MODEL]


CAST_BLOCK_BYTES = 8 * 1024 * 1024


def _cast_kernel(w_ref, o_ref):
    o_ref[...] = w_ref[...].astype(o_ref.dtype)


def _to_bf16(w):
    shape = w.shape
    cols = shape[-1]
    rows = w.size // cols
    per = HALO
    assert rows % per == 0
    blk = max(b for b in range(per, rows + 1, per)
              if rows % b == 0 and (b * cols * 4 <= CAST_BLOCK_BYTES or b == per))
    out = pl.pallas_call(
        _cast_kernel,
        grid=(rows // blk,),
        in_specs=[pl.BlockSpec((blk, cols), lambda i: (i, 0))],
        out_specs=pl.BlockSpec((blk, cols), lambda i: (i, 0)),
        out_shape=jax.ShapeDtypeStruct((rows, cols), BF16),
        compiler_params=_params(),
        name="to_bf16",
    )(w.reshape(rows, cols))
    return out.reshape(shape)


def _ada_kernel(cond_ref, w_ref, b_ref, o_ref):
    s = jax.nn.silu(cond_ref[...]).astype(BF16)
    bias = b_ref[pl.ds(pl.program_id(0), 1), :]
    mod = jnp.dot(s, w_ref[0].astype(BF16), preferred_element_type=F32) + bias
    for r in range(MOD_ROWS):
        o_ref[r] = mod[r:r + 1, :]


def _ada(cond, w_ada, b_ada):
    depth, d, n = w_ada.shape
    tn = n // 4
    return pl.pallas_call(
        _ada_kernel,
        grid=(depth, n // tn),
        in_specs=[
            pl.BlockSpec((MOD_ROWS, d), lambda l, j: (0, 0)),
            pl.BlockSpec((1, d, tn), lambda l, j: (l, 0, j)),
            pl.BlockSpec((depth, tn), lambda l, j: (0, j)),
        ],
        out_specs=pl.BlockSpec((MOD_ROWS, 1, tn), lambda l, j: (l, 0, j)),
        out_shape=jax.ShapeDtypeStruct((depth * MOD_ROWS, 1, n), F32),
        compiler_params=_params(2),
        name="ada",
    )(cond, w_ada, b_ada)


def _den_lane(e):
    return (1 - e) * HEAD_DIM


def _head_lanes(e, axis=1):
    ch = lax.broadcasted_iota(jnp.int32, (1, LANES) if axis == 1 else (LANES, 1), axis)
    own = (ch < HEAD_DIM) if e == 0 else (ch >= HEAD_DIM)
    return own, ch == _den_lane(e)


def _head_q(q2, e):
    own, _ = _head_lanes(e)
    return jnp.where(own, q2, jnp.zeros_like(q2))


def _head_v(v2, e, axis=1):
    own, is_den = _head_lanes(e, axis)
    return jnp.where(own, v2, jnp.broadcast_to(is_den.astype(v2.dtype), v2.shape))


def _normalise(accs):
    outs = [acc / acc[:, _den_lane(e):_den_lane(e) + 1] for e, acc in enumerate(accs)]
    own0, _ = _head_lanes(0)
    return jnp.where(own0, outs[0], outs[1])


def _ctx_attn_kernel(x_ref, mod_ref, g_ref, w_ref, *rest, layer, nb, seq, n_side):
    side_in, (o_ref, sk_ref, sv_ref), side_out, (qkv_scr,) = (
        rest[:n_side], rest[n_side:n_side + 3], rest[n_side + 3:2 * n_side + 3], rest[2 * n_side + 3:])
    _side_cast(side_in, side_out)
    mod = mod_ref[0]
    h = _rms_mod(x_ref[...], g_ref[layer:layer + 1, :], _mod_chunk(mod, 0), _mod_chunk(mod, 1)).astype(BF16)
    qkv_scr[...] = jnp.dot(h, w_ref[...], preferred_element_type=F32)
    for b in range(nb):
        rows = slice(b * seq, (b + 1) * seq)
        for hp in range(N_HEADS // HEADS_PER_BLOCK):
            heads = slice(hp * HEADS_PER_BLOCK, (hp + 1) * HEADS_PER_BLOCK)
            for part, s_ref in ((1, sk_ref), (2, sv_ref)):
                cols = slice(part * D_MODEL + hp * LANES, part * D_MODEL + (hp + 1) * LANES)
                s_ref[b, 0, heads] = qkv_scr[rows, cols].T.reshape(HEADS_PER_BLOCK, HEAD_DIM, seq)

    def block(unit, part):
        b, hp = unit
        return (slice(b * seq, (b + 1) * seq),
                slice(part * D_MODEL + hp * LANES, part * D_MODEL + (hp + 1) * LANES))

    def scores(unit):
        q2 = (qkv_scr[block(unit, 0)] * ATTN_SCALE).astype(BF16)
        k2 = qkv_scr[block(unit, 1)].astype(BF16)
        qs = jnp.concatenate([_head_q(q2, e) for e in range(HEADS_PER_BLOCK)], axis=0)
        return lax.dot_general(qs, k2, (((1,), (1,)), ((), ())), preferred_element_type=F32)

    units = [(b, hp) for b in range(nb) for hp in range(N_HEADS // HEADS_PER_BLOCK)]
    for unit, s in _pipelined(units, scores):
        v2 = qkv_scr[block(unit, 2)].astype(BF16)
        p = jnp.exp(s - jnp.max(s, axis=-1, keepdims=True))
        o = (jnp.dot(p.astype(BF16), v2, preferred_element_type=F32)
             / jnp.sum(p, axis=-1, keepdims=True))
        own0, _ = _head_lanes(0)
        o_ref[block(unit, 0)] = jnp.where(own0, o[:seq], o[seq:]).astype(BF16)


def _ctx_attn(x, mods3, mod_row, g, w_qkv, side, *, layer, batch, seq):
    n, d = x.shape
    nb = 2
    tm = nb * seq
    side_in, side_out, side_shapes = _side_cast_specs(side, n // tm)
    kern = functools.partial(_ctx_attn_kernel, layer=layer, nb=nb, seq=seq, n_side=len(side))
    state = jax.ShapeDtypeStruct((batch, 1, N_HEADS, HEAD_DIM, seq), F32)
    state_spec = pl.BlockSpec((nb, 1, N_HEADS, HEAD_DIM, seq), lambda i: (i, 0, 0, 0, 0))
    return pl.pallas_call(
        kern,
        grid=(n // tm,),
        in_specs=[
            pl.BlockSpec((tm, d), lambda i: (i, 0)),
            pl.BlockSpec((1, 1, 6 * d), lambda i: (mod_row, 0, 0)),
            _resident(g.shape),
            _resident((d, 3 * d)),
        ] + side_in,
        out_specs=[pl.BlockSpec((tm, d), lambda i: (i, 0)), state_spec, state_spec] + side_out,
        out_shape=[jax.ShapeDtypeStruct((n, d), BF16), state, state] + side_shapes,
        scratch_shapes=[pltpu.VMEM((tm, 3 * d), F32)],
        compiler_params=_params(),
        name="ctx_attn",
    )(x, mods3, g, w_qkv, *(w for w, _ in side))


def _qkv_kernel(x_ref, mod_ref, g_ref, w_ref, o_ref, *, layer, tn):
    mod = mod_ref[0]
    h = _rms_mod(x_ref[...], g_ref[layer:layer + 1, :], _mod_chunk(mod, 0), _mod_chunk(mod, 1)).astype(BF16)
    for c in range(w_ref.shape[1] // tn):
        cols = slice(c * tn, (c + 1) * tn)
        o_ref[:, cols] = jnp.dot(h, w_ref[:, cols], preferred_element_type=F32).astype(BF16)


def _qkv(x, mods3, mod_row_fn, g, w_qkv, *, layer, tm):
    n, d = x.shape
    nout = w_qkv.shape[1]
    return pl.pallas_call(
        functools.partial(_qkv_kernel, layer=layer, tn=512),
        grid=(n // tm,),
        in_specs=[
            pl.BlockSpec((tm, d), lambda i: (i, 0)),
            pl.BlockSpec((1, 1, 6 * d), lambda i: (mod_row_fn(i), 0, 0)),
            _resident(g.shape),
            _resident((d, nout)),
        ],
        out_specs=pl.BlockSpec((tm, nout), lambda i: (i, 0)),
        out_shape=jax.ShapeDtypeStruct((n, nout), BF16),
        compiler_params=_params(),
        name="qkv",
    )(x, mods3, g, w_qkv)


def _na_groups(rows):
    kh = min(NA_ROWS, rows)
    row_start = [min(max(r - kh // 2, 0), rows - kh) for r in range(rows)]
    groups, off = [], 0
    for r0 in range(0, rows, NA_QROWS):
        lo = min(row_start[r0:r0 + NA_QROWS])
        hi = max(row_start[r0:r0 + NA_QROWS]) + kh
        if (hi - lo) * GRID_W % LANES:
            if hi < rows:
                hi += 1
            else:
                lo -= 1
        nk = (hi - lo) * GRID_W
        groups.append((r0 * GRID_W, lo * GRID_W, nk, off))
        off += nk
    return tuple(groups), row_start, kh


def _na_bias_plan(rows):
    groups, row_start, kh = _na_groups(rows)
    nro = 2 * NA_ROWS - 1
    w = GRID_W
    assert 2 * w == LANES
    plan = []
    for (q0, k0, nk, off) in groups:
        for qi in range(NA_QROWS):
            qr = q0 // w + qi
            rs = row_start[qr]
            tile_of = lambda kr: kr - qr + NA_ROWS - 1 if rs <= kr < rs + kh else nro
            for p in range(nk // LANES):
                kr = k0 // w + 2 * p
                plan.append((qi * w, off + p * LANES, tile_of(kr), tile_of(kr + 1)))
    return tuple(plan), sum(g[2] for g in groups)


def _pad_rpb(rpb):
    lead = GRID_W - NA_COLS
    return jnp.pad(rpb, ((0, 0), (0, 1), (lead, LANES - rpb.shape[2] - lead)))


def _fill_na_bias(g_ref, o_ref, lo_scr, hi_scr, plan):
    w = GRID_W
    n_tiles = 2 * NA_ROWS - 1
    lane = lax.broadcasted_iota(jnp.int32, (w, LANES), 1)
    qc = lax.broadcasted_iota(jnp.int32, (w, LANES), 0)
    kc = lane % w
    col_start = jnp.clip(qc - NA_COLS // 2, 0, w - NA_COLS)
    col_ok = (kc >= col_start) & (kc < col_start + NA_COLS)
    in_lo = lane < w
    neg = jnp.full((w, LANES), NEG_BIAS, F32)
    for a in range(n_tiles):
        row = jnp.broadcast_to(g_ref[a:a + 1, :], (w, LANES))
        lo_scr[a] = jnp.where(col_ok, pltpu.roll(row, w + 1, 1, stride=1, stride_axis=0), neg)
        hi_scr[a] = jnp.where(col_ok, pltpu.roll(row, 1, 1, stride=1, stride_axis=0), neg)
    lo_scr[n_tiles] = neg
    hi_scr[n_tiles] = neg
    for (r0, c0, a_lo, a_hi) in plan:
        o_ref[r0:r0 + w, c0:c0 + LANES] = jnp.where(in_lo, lo_scr[a_lo], hi_scr[a_hi])


def _nattn_kernel(q_ref, k_ref, v_ref, ck_ref, cv_ref, g_ref, o_ref, bias_ref, lo_scr, hi_scr,
                  *, groups, plan):
    @pl.when(pl.program_id(1) == 0)
    def _():
        for e in range(HEADS_PER_BLOCK):
            _fill_na_bias(g_ref.at[e], bias_ref.at[e], lo_scr, hi_scr, plan)

    nbat = q_ref.shape[0]
    past = ck_ref.shape[-1]
    nq = NA_QROWS * GRID_W
    dn = (((1,), (1,)), ((), ()))
    q2 = [q_ref[b] * ATTN_SCALE for b in range(nbat)]
    k2 = [k_ref[b] for b in range(nbat)]
    v2 = [v_ref[b] for b in range(nbat)]
    kct = [ck_ref[b, 0].reshape(LANES, past).astype(BF16) for b in range(nbat)]
    vct = [cv_ref[b, 0].reshape(LANES, past).astype(BF16) for b in range(nbat)]
    units = [(b, grp, e) for b in range(nbat) for grp in groups for e in range(HEADS_PER_BLOCK)]
    s_ctx = {}

    def scores(unit):
        b, (q0, k0, nk, boff), e = unit
        if (b, e) not in s_ctx:
            s_ctx[b, e] = jnp.dot(_head_q(q2[b], e), kct[b], preferred_element_type=F32)
        qe = _head_q(q2[b][q0:q0 + nq], e)
        s_w = (lax.dot_general(qe, k2[b][k0:k0 + nk], dn, preferred_element_type=F32)
               + bias_ref[e, :, boff:boff + nk])
        return s_w, s_ctx[b, e][q0:q0 + nq]

    def attend(unit, s_w, s_c):
        b, (_, k0, nk, _), e = unit
        mx = jnp.maximum(jnp.max(s_w, axis=-1, keepdims=True),
                         jnp.max(s_c, axis=-1, keepdims=True))
        return (jnp.dot(jnp.exp(s_w - mx).astype(BF16), _head_v(v2[b][k0:k0 + nk], e),
                        preferred_element_type=F32)
                + lax.dot_general(jnp.exp(s_c - mx).astype(BF16), _head_v(vct[b], e, axis=0), dn,
                                  preferred_element_type=F32))

    accs = []
    for unit, s in _pipelined(units, scores):
        accs.append(attend(unit, *s))
        if len(accs) == HEADS_PER_BLOCK:
            b, (q0, _, _, _), _ = unit
            o_ref[b, q0:q0 + nq, :] = _normalise(accs).astype(BF16)
            accs = []


def _nattn(qkv, cache_kt, cache_vt, rpb, *, layer):
    b, t, _ = qkv.shape
    past = cache_kt.shape[4]
    groups, _, _ = _na_groups(t // GRID_W)
    plan, nbias = _na_bias_plan(t // GRID_W)
    nhp = N_HEADS // HEADS_PER_BLOCK
    nq = NA_QROWS * GRID_W
    n_tiles = 2 * NA_ROWS
    nbat = NA_BATCH if b % NA_BATCH == 0 else 1
    cache_spec = pl.BlockSpec((nbat, 1, HEADS_PER_BLOCK, HEAD_DIM, past),
                              lambda hp, bi: (bi, layer, hp, 0, 0))
    return pl.pallas_call(
        functools.partial(_nattn_kernel, groups=groups, plan=plan),
        grid=(nhp, b // nbat),
        in_specs=[
            pl.BlockSpec((nbat, t, LANES), lambda hp, bi: (bi, 0, hp)),
            pl.BlockSpec((nbat, t, LANES), lambda hp, bi: (bi, 0, nhp + hp)),
            pl.BlockSpec((nbat, t, LANES), lambda hp, bi: (bi, 0, 2 * nhp + hp)),
            cache_spec,
            cache_spec,
            pl.BlockSpec((HEADS_PER_BLOCK, n_tiles, LANES), lambda hp, bi: (hp, 0, 0)),
        ],
        out_specs=pl.BlockSpec((nbat, t, LANES), lambda hp, bi: (bi, 0, hp)),
        out_shape=jax.ShapeDtypeStruct((b, t, D_MODEL), BF16),
        scratch_shapes=[pltpu.VMEM((HEADS_PER_BLOCK, nq, nbias), F32),
                        pltpu.VMEM((n_tiles, GRID_W, LANES), F32),
                        pltpu.VMEM((n_tiles, GRID_W, LANES), F32)],
        compiler_params=_params(2),
        name="nattn",
    )(qkv, qkv, qkv, cache_kt, cache_vt, _pad_rpb(rpb))


GELU_C = float(np.float32(np.sqrt(2 / np.pi)))
GELU_K = -2.0 * GELU_C * math.log2(math.e)


def _gelu(x):
    return x / (1.0 + jnp.exp2(x * (x * x * (GELU_K * 0.044715) + GELU_K)))


def _gmlp_kernel(x_ref, mod_ref, g_ref, win_ref, gv_ref, ws_ref, bs_ref, *rest, layer, tm, n_side):
    side_in, (o_ref,), side_out, (vv_scr,) = (
        rest[:n_side], rest[n_side:n_side + 1], rest[n_side + 1:2 * n_side + 1], rest[2 * n_side + 1:])
    _side_cast(side_in, side_out)
    e = GMLP_WIDTH
    gd = e // GMLP_GROUPS
    mod = mod_ref[0]
    h = _rms_mod(x_ref[...], g_ref[layer:layer + 1, :], _mod_chunk(mod, 0), _mod_chunk(mod, 1)).astype(BF16)
    pair = 2 * gd

    def u_proj(gp):
        return jnp.dot(h, win_ref[:, gp * pair:(gp + 1) * pair], preferred_element_type=F32)

    zv = _gelu(jnp.dot(h, win_ref[:, e:2 * e], preferred_element_type=F32))
    pipe = _pipelined(list(range(e // pair)), u_proj)
    head = next(pipe)
    xc = zv - jnp.mean(zv, axis=-1, keepdims=True)
    vv = xc * lax.rsqrt(jnp.mean(xc * xc, axis=-1, keepdims=True) + EPS) * gv_ref[...]
    vv_scr[...] = vv.astype(BF16)
    for gp, u_raw in itertools.chain([head], pipe):
        u2 = _gelu(u_raw)
        for gi in range(2):
            g = 2 * gp + gi
            cols = slice(g * gd, (g + 1) * gd)
            nc = tm // CHUNK
            vcat = jnp.concatenate(
                [vv_scr[c * CHUNK:(c + 1) * CHUNK, cols] for c in range(nc)], axis=1)
            mixed = jnp.dot(ws_ref[g].astype(BF16), vcat, preferred_element_type=F32)
            for c in range(nc):
                rows = slice(c * CHUNK, (c + 1) * CHUNK)
                o_ref[rows, cols] = (u2[rows, gi * gd:(gi + 1) * gd]
                                     * (mixed[:, c * gd:(c + 1) * gd] + bs_ref[g])).astype(BF16)


def _gmlp(x, mods3, mod_row_fn, g, w_in, g_v, w_s, b_s_full, side=(), *, layer, tm):
    n, d = x.shape
    e = GMLP_WIDTH
    side_in, side_out, side_shapes = _side_cast_specs(side, n // tm)
    return pl.pallas_call(
        functools.partial(_gmlp_kernel, layer=layer, tm=tm, n_side=len(side)),
        grid=(n // tm,),
        in_specs=[
            pl.BlockSpec((tm, d), lambda i: (i, 0)),
            pl.BlockSpec((1, 1, 6 * d), lambda i: (mod_row_fn(i), 0, 0)),
            _resident(g.shape),
            _resident((d, 2 * e)),
            _resident((1, e)),
            _resident(w_s.shape),
            _resident(b_s_full.shape),
        ] + side_in,
        out_specs=[pl.BlockSpec((tm, e), lambda i: (i, 0))] + side_out,
        out_shape=[jax.ShapeDtypeStruct((n, e), BF16)] + side_shapes,
        scratch_shapes=[pltpu.VMEM((tm, e), BF16)],
        compiler_params=_params(),
        name="gmlp",
    )(x, mods3, g, w_in, g_v, w_s, b_s_full, *(w for w, _ in side))


def _conv_pieces(tm, seq, halo):
    if halo:
        return ((0, tm + 2 * HALO, HALO, tm),)
    return tuple((s * seq, seq, 0, seq) for s in range(tm // seq))


def _mix_ffn_kernel(x_ref, xp_ref, xn_ref, m_ref, mp_ref, mn_ref, mod_ref, gffn_ref, gfin_ref,
                    wmo_ref, wup_ref, wconv_ref, bconv_ref, wdown_ref, *rest,
                    layer, tm, seq, halo, final, n_side):
    side_in, (o_ref,), side_out, (act_scr,) = (
        rest[:n_side], rest[n_side:n_side + 1], rest[n_side + 1:2 * n_side + 1], rest[2 * n_side + 1:])
    _side_cast(side_in, side_out)
    mod = mod_ref[0]
    gate_mix, shift, scale, gate_ffn = (_mod_chunk(mod, k) for k in (2, 3, 4, 5))
    if halo:
        xcat = jnp.concatenate([xp_ref[0], x_ref[...], xn_ref[0]], axis=0)
        mcat = jnp.concatenate([mp_ref[0], m_ref[...], mn_ref[0]], axis=0)
        main = slice(HALO, HALO + tm)
    else:
        xcat, mcat = x_ref[...], m_ref[...]
        main = slice(0, tm)
    r = xcat.shape[0]
    halves = (slice(0, r // 2), slice(r // 2, r))
    x1_parts = [xcat[rows] + gate_mix * jnp.dot(mcat[rows], wmo_ref[...], preferred_element_type=F32)
                for rows in halves]
    x1 = jnp.concatenate(x1_parts, axis=0)
    hcat = jnp.concatenate(
        [_rms_mod(part, gffn_ref[layer:layer + 1, :], shift, scale).astype(BF16)
         for part in x1_parts], axis=0)

    f = FFN_TILE
    if halo:
        blk = pl.program_id(0) % (seq // tm)
        hcat = jnp.concatenate([
            jnp.where(blk == 0, jnp.zeros((HALO, D_MODEL), BF16), hcat[:HALO]),
            hcat[main],
            jnp.where(blk == seq // tm - 1, jnp.zeros((HALO, D_MODEL), BF16), hcat[HALO + tm:]),
        ], axis=0)
    pieces = _conv_pieces(tm, seq, halo)
    sub = lax.broadcasted_iota(jnp.int32, (PAD, 1), 0)

    def up(c):
        return [jnp.dot(hcat, wup_ref[:, col:col + f], preferred_element_type=F32)
                for col in (c * f, FFN_DIM + c * f)]

    def conv(a, col):
        cols = slice(col, col + f)
        wc = wconv_ref[0, :, cols]
        out = []
        for (a0, n_rows, own0, own) in pieces:
            piece = a[a0:a0 + n_rows]
            own_rows = slice(own0, own0 + own)
            prev = pltpu.roll(piece, 1, 0)[own_rows]
            nxt = pltpu.roll(piece, n_rows - 1, 0)[own_rows]
            if not halo:
                prev = jnp.concatenate([jnp.where(sub == 0, 0.0, prev[:PAD]), prev[PAD:]], axis=0)
                nxt = jnp.concatenate([nxt[:-PAD], jnp.where(sub == PAD - 1, 0.0, nxt[-PAD:])], axis=0)
            out.append(prev * wc[0:1] + piece[own_rows] * wc[1:2] + nxt * wc[2:3]
                       + bconv_ref[layer:layer + 1, cols])
        return out[0] if len(out) == 1 else jnp.concatenate(out, axis=0)

    for c, (a_gate, a_val) in _pipelined(list(range(FFN_DIM // f)), up):
        act = jax.nn.silu(conv(a_gate, c * f)) * conv(a_val, FFN_DIM + c * f)
        act_scr[:, c * f:(c + 1) * f] = act.astype(BF16)
    acc = jnp.dot(act_scr[...], wdown_ref[...], preferred_element_type=F32)
    out = x1[main] + gate_ffn * acc
    if final:
        out = out * lax.rsqrt(jnp.mean(out * out, axis=-1, keepdims=True) + EPS) * gfin_ref[...]
    o_ref[...] = out


def _mix_ffn(x, m, mods3, mod_row_fn, g_ffn, g_fin, w_mo, w_up, w_conv, b_conv, w_down,
             side=(), *, layer, tm, seq, final):
    n, d = x.shape
    km = m.shape[1]
    assert w_conv.shape[1] == CONV_W == 3
    halo = tm % seq != 0
    assert not halo or seq % tm == 0
    nh = n // HALO
    per = tm // HALO
    prev_map = lambda i: (jnp.maximum(i * per - 1, 0), 0, 0)
    next_map = lambda i: (jnp.minimum((i + 1) * per, nh - 1), 0, 0)
    side_in, side_out, side_shapes = _side_cast_specs(side, n // tm)
    kern = functools.partial(_mix_ffn_kernel, layer=layer, tm=tm, seq=seq, halo=halo, final=final,
                             n_side=len(side))
    return pl.pallas_call(
        kern,
        grid=(n // tm,),
        in_specs=[
            pl.BlockSpec((tm, d), lambda i: (i, 0)),
            pl.BlockSpec((1, HALO, d), prev_map),
            pl.BlockSpec((1, HALO, d), next_map),
            pl.BlockSpec((tm, km), lambda i: (i, 0)),
            pl.BlockSpec((1, HALO, km), prev_map),
            pl.BlockSpec((1, HALO, km), next_map),
            pl.BlockSpec((1, 1, 6 * d), lambda i: (mod_row_fn(i), 0, 0)),
            _resident(g_ffn.shape),
            _resident((1, d)),
            _resident(w_mo.shape),
            _resident(w_up.shape),
            _resident_layer(w_conv.shape, layer),
            _resident(b_conv.shape),
            _resident(w_down.shape),
        ] + side_in,
        out_specs=[pl.BlockSpec((tm, d), lambda i: (i, 0))] + side_out,
        out_shape=[jax.ShapeDtypeStruct((n, d), F32)] + side_shapes,
        scratch_shapes=[pltpu.VMEM((tm, FFN_DIM), BF16)],
        compiler_params=_params(),
        name="mix_ffn",
    )(x, x.reshape(nh, HALO, d), x.reshape(nh, HALO, d),
      m, m.reshape(nh, HALO, km), m.reshape(nh, HALO, km),
      mods3, g_ffn, g_fin, w_mo, w_up, w_conv, b_conv, w_down, *(w for w, _ in side))


def kernel(x_prompt, x_sample, cache_k, cache_v, c, c_ctx, w_ada, b_ada, norm_mix_g,
           norm_ffn_g, norm_final_g, w_qkv, w_attn_out, rpb, w_gmlp_in, g_gmlp_v,
           w_spatial, b_spatial, w_gmlp_out, w_ffn_up, w_ffn_conv, b_ffn_conv, w_ffn_down):
    batch, seq, d = x_prompt.shape
    dec_batch, dec_seq, _ = x_sample.shape
    depth = w_ada.shape[0]
    assert depth == 2 and d == D_MODEL

    cond = jnp.zeros((MOD_ROWS, d), F32).at[0].set(c_ctx).at[1:1 + dec_batch].set(c)
    mods3 = _ada(cond, w_ada, b_ada)

    w_qkv_b = _to_bf16(w_qkv[0])
    w_sp = w_spatial[0]
    b_s_full = jnp.broadcast_to(b_spatial[0][:, :, None], (GMLP_GROUPS, CHUNK, CHUNK))
    conv_w = (w_ffn_conv, b_ffn_conv)
    g_mix, g_ffn = norm_mix_g, norm_ffn_g
    g_fin = norm_final_g.reshape(1, d)
    tm = 512

    def ctx_row(layer):
        return lambda i: layer * MOD_ROWS
    x = x_prompt.reshape(batch * seq, d)
    o, state_kt, state_vt, w_ao, w_up0, w_down0 = _ctx_attn(
        x, mods3, 0, g_mix, w_qkv_b, [(w_attn_out, 0), (w_ffn_up, 0), (w_ffn_down, 0)],
        layer=0, batch=batch, seq=seq)
    state_k, state_v = jnp.swapaxes(state_kt, 3, 4), jnp.swapaxes(state_vt, 3, 4)
    x, w_gin = _mix_ffn(x, o, mods3, ctx_row(0), g_ffn, g_fin, w_ao, w_up0, *conv_w, w_down0,
                        [(w_gmlp_in, 0)], layer=0, tm=tm, seq=seq, final=False)
    m, w_gout, w_up1, w_down1 = _gmlp(
        x, mods3, ctx_row(1), g_mix, w_gin, g_gmlp_v, w_sp, b_s_full,
        [(w_gmlp_out, 0), (w_ffn_up, 1), (w_ffn_down, 1)], layer=1, tm=tm)
    y_prompt, = _mix_ffn(x, m, mods3, ctx_row(1), g_ffn, g_fin, w_gout, w_up1, *conv_w, w_down1,
                         layer=1, tm=tm, seq=seq, final=True)

    def lat_row(layer):
        return lambda i: layer * MOD_ROWS + 1 + (i * tm) // dec_seq
    x = x_sample.reshape(dec_batch * dec_seq, d)
    qkv = _qkv(x, mods3, lat_row(0), g_mix, w_qkv_b, layer=0, tm=tm)
    o = _nattn(qkv.reshape(dec_batch, dec_seq, 3 * d), jnp.swapaxes(cache_k, 3, 4),
               jnp.swapaxes(cache_v, 3, 4), rpb[0], layer=0)
    x, = _mix_ffn(x, o.reshape(dec_batch * dec_seq, d), mods3, lat_row(0), g_ffn, g_fin,
                  w_ao, w_up0, *conv_w, w_down0, layer=0, tm=tm, seq=dec_seq, final=False)
    m, = _gmlp(x, mods3, lat_row(1), g_mix, w_gin, g_gmlp_v, w_sp, b_s_full, layer=1, tm=tm)
    y_sample, = _mix_ffn(x, m, mods3, lat_row(1), g_ffn, g_fin, w_gout, w_up1, *conv_w, w_down1,
                         layer=1, tm=tm, seq=dec_seq, final=True)

    return (y_prompt.reshape(batch, seq, d), y_sample.reshape(dec_batch, dec_seq, d),
            state_k, state_v)
```

```python
import functools
import itertools
import math

import numpy as np
import jax
import jax.numpy as jnp
from jax import lax
from jax.experimental import pallas as pl
from jax.experimental.pallas import tpu as pltpu

D_MODEL = 1024
N_HEADS = 16
HEAD_DIM = D_MODEL // N_HEADS
GRID_W = 64
NA_ROWS = 8
NA_COLS = 16
CHUNK = 128
GMLP_WIDTH = 2 * D_MODEL
GMLP_GROUPS = 16
FFN_DIM = 2816
CONV_W = 3
EPS = 1e-6
ATTN_SCALE = HEAD_DIM ** -0.5
assert math.frexp(ATTN_SCALE)[0] == 0.5

LANES = 128
HEADS_PER_BLOCK = LANES // HEAD_DIM
MOD_ROWS = 8
HALO = 16
PAD = 8
FFN_TILE = 256
NA_QROWS = 4
NA_BATCH = 4
LOOKAHEAD = 1
NEG_BIAS = -1e30
VMEM_LIMIT = 56 * 1024 * 1024

BF16 = jnp.bfloat16
F32 = jnp.float32


def _params(n_axes=1):
    return pltpu.CompilerParams(
        dimension_semantics=("arbitrary",) * n_axes, vmem_limit_bytes=VMEM_LIMIT)


def _resident(shape):
    nd = len(shape)
    return pl.BlockSpec(shape, lambda *_: (0,) * nd, pipeline_mode=pl.Buffered(1))


def _resident_layer(shape, layer):
    nd = len(shape)
    return pl.BlockSpec((1,) + tuple(shape[1:]), lambda *_: (layer,) + (0,) * (nd - 1),
                        pipeline_mode=pl.Buffered(1))


def _pipelined(items, first_stage):
    ready = [first_stage(it) for it in items[:LOOKAHEAD]]
    for i, item in enumerate(items):
        if i + LOOKAHEAD < len(items):
            ready.append(first_stage(items[i + LOOKAHEAD]))
        yield item, ready.pop(0)


def _side_cast_specs(weights, n_steps):
    ins, outs, shapes = [], [], []
    for w, layer in weights:
        _, rows, cols = w.shape
        assert rows % (n_steps * HALO) == 0
        ins.append(pl.BlockSpec((1, rows // n_steps, cols), lambda i, layer=layer: (layer, i, 0)))
        outs.append(pl.BlockSpec((rows // n_steps, cols), lambda i: (i, 0)))
        shapes.append(jax.ShapeDtypeStruct((rows, cols), BF16))
    return ins, outs, shapes


def _side_cast(src_refs, dst_refs):
    for src, dst in zip(src_refs, dst_refs):
        dst[...] = src[0].astype(BF16)


def _rms_mod(x, g, shift, scale):
    y = x * lax.rsqrt(jnp.mean(x * x, axis=-1, keepdims=True) + EPS)
    return (y * g) * (1 + scale) + shift


def _mod_chunk(mod, k):
    return mod[:, k * D_MODEL:(k + 1) * D_MODEL]


CAST_BLOCK_BYTES = 8 * 1024 * 1024


def _cast_kernel(w_ref, o_ref):
    o_ref[...] = w_ref[...].astype(o_ref.dtype)


def _to_bf16(w):
    shape = w.shape
    cols = shape[-1]
    rows = w.size // cols
    per = HALO
    assert rows % per == 0
    blk = max(b for b in range(per, rows + 1, per)
              if rows % b == 0 and (b * cols * 4 <= CAST_BLOCK_BYTES or b == per))
    out = pl.pallas_call(
        _cast_kernel,
        grid=(rows // blk,),
        in_specs=[pl.BlockSpec((blk, cols), lambda i: (i, 0))],
        out_specs=pl.BlockSpec((blk, cols), lambda i: (i, 0)),
        out_shape=jax.ShapeDtypeStruct((rows, cols), BF16),
        compiler_params=_params(),
        name="to_bf16",
    )(w.reshape(rows, cols))
    return out.reshape(shape)


def _ada_kernel(cond_ref, w_ref, b_ref, o_ref):
    s = jax.nn.silu(cond_ref[...]).astype(BF16)
    bias = b_ref[pl.ds(pl.program_id(0), 1), :]
    mod = jnp.dot(s, w_ref[0].astype(BF16), preferred_element_type=F32) + bias
    for r in range(MOD_ROWS):
        o_ref[r] = mod[r:r + 1, :]


def _ada(cond, w_ada, b_ada):
    depth, d, n = w_ada.shape
    tn = n // 4
    return pl.pallas_call(
        _ada_kernel,
        grid=(depth, n // tn),
        in_specs=[
            pl.BlockSpec((MOD_ROWS, d), lambda l, j: (0, 0)),
            pl.BlockSpec((1, d, tn), lambda l, j: (l, 0, j)),
            pl.BlockSpec((depth, tn), lambda l, j: (0, j)),
        ],
        out_specs=pl.BlockSpec((MOD_ROWS, 1, tn), lambda l, j: (l, 0, j)),
        out_shape=jax.ShapeDtypeStruct((depth * MOD_ROWS, 1, n), F32),
        compiler_params=_params(2),
        name="ada",
    )(cond, w_ada, b_ada)


def _den_lane(e):
    return (1 - e) * HEAD_DIM


def _head_lanes(e, axis=1):
    ch = lax.broadcasted_iota(jnp.int32, (1, LANES) if axis == 1 else (LANES, 1), axis)
    own = (ch < HEAD_DIM) if e == 0 else (ch >= HEAD_DIM)
    return own, ch == _den_lane(e)


def _head_q(q2, e):
    own, _ = _head_lanes(e)
    return jnp.where(own, q2, jnp.zeros_like(q2))


def _head_v(v2, e, axis=1):
    own, is_den = _head_lanes(e, axis)
    return jnp.where(own, v2, jnp.broadcast_to(is_den.astype(v2.dtype), v2.shape))


def _normalise(accs):
    outs = [acc / acc[:, _den_lane(e):_den_lane(e) + 1] for e, acc in enumerate(accs)]
    own0, _ = _head_lanes(0)
    return jnp.where(own0, outs[0], outs[1])


def _ctx_attn_kernel(x_ref, mod_ref, g_ref, w_ref, *rest, layer, nb, seq, n_side):
    side_in, (o_ref, sk_ref, sv_ref), side_out, (qkv_scr,) = (
        rest[:n_side], rest[n_side:n_side + 3], rest[n_side + 3:2 * n_side + 3], rest[2 * n_side + 3:])
    _side_cast(side_in, side_out)
    mod = mod_ref[0]
    h = _rms_mod(x_ref[...], g_ref[layer:layer + 1, :], _mod_chunk(mod, 0), _mod_chunk(mod, 1)).astype(BF16)
    qkv_scr[...] = jnp.dot(h, w_ref[...], preferred_element_type=F32)
    for b in range(nb):
        rows = slice(b * seq, (b + 1) * seq)
        for hp in range(N_HEADS // HEADS_PER_BLOCK):
            heads = slice(hp * HEADS_PER_BLOCK, (hp + 1) * HEADS_PER_BLOCK)
            for part, s_ref in ((1, sk_ref), (2, sv_ref)):
                cols = slice(part * D_MODEL + hp * LANES, part * D_MODEL + (hp + 1) * LANES)
                s_ref[b, 0, heads] = qkv_scr[rows, cols].T.reshape(HEADS_PER_BLOCK, HEAD_DIM, seq)

    def block(unit, part):
        b, hp = unit
        return (slice(b * seq, (b + 1) * seq),
                slice(part * D_MODEL + hp * LANES, part * D_MODEL + (hp + 1) * LANES))

    def scores(unit):
        q2 = (qkv_scr[block(unit, 0)] * ATTN_SCALE).astype(BF16)
        k2 = qkv_scr[block(unit, 1)].astype(BF16)
        qs = jnp.concatenate([_head_q(q2, e) for e in range(HEADS_PER_BLOCK)], axis=0)
        return lax.dot_general(qs, k2, (((1,), (1,)), ((), ())), preferred_element_type=F32)

    units = [(b, hp) for b in range(nb) for hp in range(N_HEADS // HEADS_PER_BLOCK)]
    for unit, s in _pipelined(units, scores):
        v2 = qkv_scr[block(unit, 2)].astype(BF16)
        p = jnp.exp(s - jnp.max(s, axis=-1, keepdims=True))
        o = (jnp.dot(p.astype(BF16), v2, preferred_element_type=F32)
             / jnp.sum(p, axis=-1, keepdims=True))
        own0, _ = _head_lanes(0)
        o_ref[block(unit, 0)] = jnp.where(own0, o[:seq], o[seq:]).astype(BF16)


def _ctx_attn(x, mods3, mod_row, g, w_qkv, side, *, layer, batch, seq):
    n, d = x.shape
    nb = 2
    tm = nb * seq
    side_in, side_out, side_shapes = _side_cast_specs(side, n // tm)
    kern = functools.partial(_ctx_attn_kernel, layer=layer, nb=nb, seq=seq, n_side=len(side))
    state = jax.ShapeDtypeStruct((batch, 1, N_HEADS, HEAD_DIM, seq), F32)
    state_spec = pl.BlockSpec((nb, 1, N_HEADS, HEAD_DIM, seq), lambda i: (i, 0, 0, 0, 0))
    return pl.pallas_call(
        kern,
        grid=(n // tm,),
        in_specs=[
            pl.BlockSpec((tm, d), lambda i: (i, 0)),
            pl.BlockSpec((1, 1, 6 * d), lambda i: (mod_row, 0, 0)),
            _resident(g.shape),
            _resident((d, 3 * d)),
        ] + side_in,
        out_specs=[pl.BlockSpec((tm, d), lambda i: (i, 0)), state_spec, state_spec] + side_out,
        out_shape=[jax.ShapeDtypeStruct((n, d), BF16), state, state] + side_shapes,
        scratch_shapes=[pltpu.VMEM((tm, 3 * d), F32)],
        compiler_params=_params(),
        name="ctx_attn",
    )(x, mods3, g, w_qkv, *(w for w, _ in side))


def _qkv_kernel(x_ref, mod_ref, g_ref, w_ref, o_ref, *, layer, tn):
    mod = mod_ref[0]
    h = _rms_mod(x_ref[...], g_ref[layer:layer + 1, :], _mod_chunk(mod, 0), _mod_chunk(mod, 1)).astype(BF16)
    for c in range(w_ref.shape[1] // tn):
        cols = slice(c * tn, (c + 1) * tn)
        o_ref[:, cols] = jnp.dot(h, w_ref[:, cols], preferred_element_type=F32).astype(BF16)


def _qkv(x, mods3, mod_row_fn, g, w_qkv, *, layer, tm):
    n, d = x.shape
    nout = w_qkv.shape[1]
    return pl.pallas_call(
        functools.partial(_qkv_kernel, layer=layer, tn=512),
        grid=(n // tm,),
        in_specs=[
            pl.BlockSpec((tm, d), lambda i: (i, 0)),
            pl.BlockSpec((1, 1, 6 * d), lambda i: (mod_row_fn(i), 0, 0)),
            _resident(g.shape),
            _resident((d, nout)),
        ],
        out_specs=pl.BlockSpec((tm, nout), lambda i: (i, 0)),
        out_shape=jax.ShapeDtypeStruct((n, nout), BF16),
        compiler_params=_params(),
        name="qkv",
    )(x, mods3, g, w_qkv)


def _na_groups(rows):
    kh = min(NA_ROWS, rows)
    row_start = [min(max(r - kh // 2, 0), rows - kh) for r in range(rows)]
    groups, off = [], 0
    for r0 in range(0, rows, NA_QROWS):
        lo = min(row_start[r0:r0 + NA_QROWS])
        hi = max(row_start[r0:r0 + NA_QROWS]) + kh
        if (hi - lo) * GRID_W % LANES:
            if hi < rows:
                hi += 1
            else:
                lo -= 1
        nk = (hi - lo) * GRID_W
        groups.append((r0 * GRID_W, lo * GRID_W, nk, off))
        off += nk
    return tuple(groups), row_start, kh


def _na_bias_plan(rows):
    groups, row_start, kh = _na_groups(rows)
    nro = 2 * NA_ROWS - 1
    w = GRID_W
    assert 2 * w == LANES
    plan = []
    for (q0, k0, nk, off) in groups:
        for qi in range(NA_QROWS):
            qr = q0 // w + qi
            rs = row_start[qr]
            tile_of = lambda kr: kr - qr + NA_ROWS - 1 if rs <= kr < rs + kh else nro
            for p in range(nk // LANES):
                kr = k0 // w + 2 * p
                plan.append((qi * w, off + p * LANES, tile_of(kr), tile_of(kr + 1)))
    return tuple(plan), sum(g[2] for g in groups)


def _pad_rpb(rpb):
    lead = GRID_W - NA_COLS
    return jnp.pad(rpb, ((0, 0), (0, 1), (lead, LANES - rpb.shape[2] - lead)))


def _fill_na_bias(g_ref, o_ref, lo_scr, hi_scr, plan):
    w = GRID_W
    n_tiles = 2 * NA_ROWS - 1
    lane = lax.broadcasted_iota(jnp.int32, (w, LANES), 1)
    qc = lax.broadcasted_iota(jnp.int32, (w, LANES), 0)
    kc = lane % w
    col_start = jnp.clip(qc - NA_COLS // 2, 0, w - NA_COLS)
    col_ok = (kc >= col_start) & (kc < col_start + NA_COLS)
    in_lo = lane < w
    neg = jnp.full((w, LANES), NEG_BIAS, F32)
    for a in range(n_tiles):
        row = jnp.broadcast_to(g_ref[a:a + 1, :], (w, LANES))
        lo_scr[a] = jnp.where(col_ok, pltpu.roll(row, w + 1, 1, stride=1, stride_axis=0), neg)
        hi_scr[a] = jnp.where(col_ok, pltpu.roll(row, 1, 1, stride=1, stride_axis=0), neg)
    lo_scr[n_tiles] = neg
    hi_scr[n_tiles] = neg
    for (r0, c0, a_lo, a_hi) in plan:
        o_ref[r0:r0 + w, c0:c0 + LANES] = jnp.where(in_lo, lo_scr[a_lo], hi_scr[a_hi])


def _nattn_kernel(q_ref, k_ref, v_ref, ck_ref, cv_ref, g_ref, o_ref, bias_ref, lo_scr, hi_scr,
                  *, groups, plan):
    @pl.when(pl.program_id(1) == 0)
    def _():
        for e in range(HEADS_PER_BLOCK):
            _fill_na_bias(g_ref.at[e], bias_ref.at[e], lo_scr, hi_scr, plan)

    nbat = q_ref.shape[0]
    past = ck_ref.shape[-1]
    nq = NA_QROWS * GRID_W
    dn = (((1,), (1,)), ((), ()))
    q2 = [q_ref[b] * ATTN_SCALE for b in range(nbat)]
    k2 = [k_ref[b] for b in range(nbat)]
    v2 = [v_ref[b] for b in range(nbat)]
    kct = [ck_ref[b, 0].reshape(LANES, past).astype(BF16) for b in range(nbat)]
    vct = [cv_ref[b, 0].reshape(LANES, past).astype(BF16) for b in range(nbat)]
    units = [(b, grp, e) for b in range(nbat) for grp in groups for e in range(HEADS_PER_BLOCK)]
    s_ctx = {}

    def scores(unit):
        b, (q0, k0, nk, boff), e = unit
        if (b, e) not in s_ctx:
            s_ctx[b, e] = jnp.dot(_head_q(q2[b], e), kct[b], preferred_element_type=F32)
        qe = _head_q(q2[b][q0:q0 + nq], e)
        s_w = (lax.dot_general(qe, k2[b][k0:k0 + nk], dn, preferred_element_type=F32)
               + bias_ref[e, :, boff:boff + nk])
        return s_w, s_ctx[b, e][q0:q0 + nq]

    def attend(unit, s_w, s_c):
        b, (_, k0, nk, _), e = unit
        mx = jnp.maximum(jnp.max(s_w, axis=-1, keepdims=True),
                         jnp.max(s_c, axis=-1, keepdims=True))
        return (jnp.dot(jnp.exp(s_w - mx).astype(BF16), _head_v(v2[b][k0:k0 + nk], e),
                        preferred_element_type=F32)
                + lax.dot_general(jnp.exp(s_c - mx).astype(BF16), _head_v(vct[b], e, axis=0), dn,
                                  preferred_element_type=F32))

    accs = []
    for unit, s in _pipelined(units, scores):
        accs.append(attend(unit, *s))
        if len(accs) == HEADS_PER_BLOCK:
            b, (q0, _, _, _), _ = unit
            o_ref[b, q0:q0 + nq, :] = _normalise(accs).astype(BF16)
            accs = []


def _nattn(qkv, cache_kt, cache_vt, rpb, *, layer):
    b, t, _ = qkv.shape
    past = cache_kt.shape[4]
    groups, _, _ = _na_groups(t // GRID_W)
    plan, nbias = _na_bias_plan(t // GRID_W)
    nhp = N_HEADS // HEADS_PER_BLOCK
    nq = NA_QROWS * GRID_W
    n_tiles = 2 * NA_ROWS
    nbat = NA_BATCH if b % NA_BATCH == 0 else 1
    cache_spec = pl.BlockSpec((nbat, 1, HEADS_PER_BLOCK, HEAD_DIM, past),
                              lambda hp, bi: (bi, layer, hp, 0, 0))
    return pl.pallas_call(
        functools.partial(_nattn_kernel, groups=groups, plan=plan),
        grid=(nhp, b // nbat),
        in_specs=[
            pl.BlockSpec((nbat, t, LANES), lambda hp, bi: (bi, 0, hp)),
            pl.BlockSpec((nbat, t, LANES), lambda hp, bi: (bi, 0, nhp + hp)),
            pl.BlockSpec((nbat, t, LANES), lambda hp, bi: (bi, 0, 2 * nhp + hp)),
            cache_spec,
            cache_spec,
            pl.BlockSpec((HEADS_PER_BLOCK, n_tiles, LANES), lambda hp, bi: (hp, 0, 0)),
        ],
        out_specs=pl.BlockSpec((nbat, t, LANES), lambda hp, bi: (bi, 0, hp)),
        out_shape=jax.ShapeDtypeStruct((b, t, D_MODEL), BF16),
        scratch_shapes=[pltpu.VMEM((HEADS_PER_BLOCK, nq, nbias), F32),
                        pltpu.VMEM((n_tiles, GRID_W, LANES), F32),
                        pltpu.VMEM((n_tiles, GRID_W, LANES), F32)],
        compiler_params=_params(2),
        name="nattn",
    )(qkv, qkv, qkv, cache_kt, cache_vt, _pad_rpb(rpb))


GELU_C = float(np.float32(np.sqrt(2 / np.pi)))
GELU_K = -2.0 * GELU_C * math.log2(math.e)


def _gelu(x):
    return x / (1.0 + jnp.exp2(x * (x * x * (GELU_K * 0.044715) + GELU_K)))


def _gmlp_kernel(x_ref, mod_ref, g_ref, win_ref, gv_ref, ws_ref, bs_ref, *rest, layer, tm, n_side):
    side_in, (o_ref,), side_out, (vv_scr,) = (
        rest[:n_side], rest[n_side:n_side + 1], rest[n_side + 1:2 * n_side + 1], rest[2 * n_side + 1:])
    _side_cast(side_in, side_out)
    e = GMLP_WIDTH
    gd = e // GMLP_GROUPS
    mod = mod_ref[0]
    h = _rms_mod(x_ref[...], g_ref[layer:layer + 1, :], _mod_chunk(mod, 0), _mod_chunk(mod, 1)).astype(BF16)
    pair = 2 * gd

    def u_proj(gp):
        return jnp.dot(h, win_ref[:, gp * pair:(gp + 1) * pair], preferred_element_type=F32)

    zv = _gelu(jnp.dot(h, win_ref[:, e:2 * e], preferred_element_type=F32))
    pipe = _pipelined(list(range(e // pair)), u_proj)
    head = next(pipe)
    xc = zv - jnp.mean(zv, axis=-1, keepdims=True)
    vv = xc * lax.rsqrt(jnp.mean(xc * xc, axis=-1, keepdims=True) + EPS) * gv_ref[...]
    vv_scr[...] = vv.astype(BF16)
    for gp, u_raw in itertools.chain([head], pipe):
        u2 = _gelu(u_raw)
        for gi in range(2):
            g = 2 * gp + gi
            cols = slice(g * gd, (g + 1) * gd)
            nc = tm // CHUNK
            vcat = jnp.concatenate(
                [vv_scr[c * CHUNK:(c + 1) * CHUNK, cols] for c in range(nc)], axis=1)
            mixed = jnp.dot(ws_ref[g].astype(BF16), vcat, preferred_element_type=F32)
            for c in range(nc):
                rows = slice(c * CHUNK, (c + 1) * CHUNK)
                o_ref[rows, cols] = (u2[rows, gi * gd:(gi + 1) * gd]
                                     * (mixed[:, c * gd:(c + 1) * gd] + bs_ref[g])).astype(BF16)


def _gmlp(x, mods3, mod_row_fn, g, w_in, g_v, w_s, b_s_full, side=(), *, layer, tm):
    n, d = x.shape
    e = GMLP_WIDTH
    side_in, side_out, side_shapes = _side_cast_specs(side, n // tm)
    return pl.pallas_call(
        functools.partial(_gmlp_kernel, layer=layer, tm=tm, n_side=len(side)),
        grid=(n // tm,),
        in_specs=[
            pl.BlockSpec((tm, d), lambda i: (i, 0)),
            pl.BlockSpec((1, 1, 6 * d), lambda i: (mod_row_fn(i), 0, 0)),
            _resident(g.shape),
            _resident((d, 2 * e)),
            _resident((1, e)),
            _resident(w_s.shape),
            _resident(b_s_full.shape),
        ] + side_in,
        out_specs=[pl.BlockSpec((tm, e), lambda i: (i, 0))] + side_out,
        out_shape=[jax.ShapeDtypeStruct((n, e), BF16)] + side_shapes,
        scratch_shapes=[pltpu.VMEM((tm, e), BF16)],
        compiler_params=_params(),
        name="gmlp",
    )(x, mods3, g, w_in, g_v, w_s, b_s_full, *(w for w, _ in side))


def _conv_pieces(tm, seq, halo):
    if halo:
        return ((0, tm + 2 * HALO, HALO, tm),)
    return tuple((s * seq, seq, 0, seq) for s in range(tm // seq))


def _mix_ffn_kernel(x_ref, xp_ref, xn_ref, m_ref, mp_ref, mn_ref, mod_ref, gffn_ref, gfin_ref,
                    wmo_ref, wup_ref, wconv_ref, bconv_ref, wdown_hbm, *rest,
                    layer, tm, seq, halo, final, n_side):
    side_in, (o_ref,), side_out, (act_scr, wdown_ref, sem) = (
        rest[:n_side], rest[n_side:n_side + 1], rest[n_side + 1:2 * n_side + 1], rest[2 * n_side + 1:])
    first_step = pl.program_id(0) == 0
    down_copy = pltpu.make_async_copy(wdown_hbm, wdown_ref, sem.at[0])

    @pl.when(first_step)
    def _():
        down_copy.start()

    _side_cast(side_in, side_out)
    mod = mod_ref[0]
    gate_mix, shift, scale, gate_ffn = (_mod_chunk(mod, k) for k in (2, 3, 4, 5))
    if halo:
        xcat = jnp.concatenate([xp_ref[0], x_ref[...], xn_ref[0]], axis=0)
        mcat = jnp.concatenate([mp_ref[0], m_ref[...], mn_ref[0]], axis=0)
        main = slice(HALO, HALO + tm)
    else:
        xcat, mcat = x_ref[...], m_ref[...]
        main = slice(0, tm)
    r = xcat.shape[0]
    halves = (slice(0, r // 2), slice(r // 2, r))
    x1_parts = [xcat[rows] + gate_mix * jnp.dot(mcat[rows], wmo_ref[...], preferred_element_type=F32)
                for rows in halves]
    x1 = jnp.concatenate(x1_parts, axis=0)
    hcat = jnp.concatenate(
        [_rms_mod(part, gffn_ref[layer:layer + 1, :], shift, scale).astype(BF16)
         for part in x1_parts], axis=0)

    f = FFN_TILE
    if halo:
        blk = pl.program_id(0) % (seq // tm)
        hcat = jnp.concatenate([
            jnp.where(blk == 0, jnp.zeros((HALO, D_MODEL), BF16), hcat[:HALO]),
            hcat[main],
            jnp.where(blk == seq // tm - 1, jnp.zeros((HALO, D_MODEL), BF16), hcat[HALO + tm:]),
        ], axis=0)
    pieces = _conv_pieces(tm, seq, halo)
    sub = lax.broadcasted_iota(jnp.int32, (PAD, 1), 0)

    def up(c):
        return [jnp.dot(hcat, wup_ref[:, col:col + f], preferred_element_type=F32)
                for col in (c * f, FFN_DIM + c * f)]

    def conv(a, col):
        cols = slice(col, col + f)
        wc = wconv_ref[0, :, cols]
        out = []
        for (a0, n_rows, own0, own) in pieces:
            piece = a[a0:a0 + n_rows]
            own_rows = slice(own0, own0 + own)
            prev = pltpu.roll(piece, 1, 0)[own_rows]
            nxt = pltpu.roll(piece, n_rows - 1, 0)[own_rows]
            if not halo:
                prev = jnp.concatenate([jnp.where(sub == 0, 0.0, prev[:PAD]), prev[PAD:]], axis=0)
                nxt = jnp.concatenate([nxt[:-PAD], jnp.where(sub == PAD - 1, 0.0, nxt[-PAD:])], axis=0)
            out.append(prev * wc[0:1] + piece[own_rows] * wc[1:2] + nxt * wc[2:3]
                       + bconv_ref[layer:layer + 1, cols])
        return out[0] if len(out) == 1 else jnp.concatenate(out, axis=0)

    for c, (a_gate, a_val) in _pipelined(list(range(FFN_DIM // f)), up):
        act = jax.nn.silu(conv(a_gate, c * f)) * conv(a_val, FFN_DIM + c * f)
        act_scr[:, c * f:(c + 1) * f] = act.astype(BF16)

    @pl.when(first_step)
    def _():
        down_copy.wait()

    acc = jnp.dot(act_scr[...], wdown_ref[...], preferred_element_type=F32)
    out = x1[main] + gate_ffn * acc
    if final:
        out = out * lax.rsqrt(jnp.mean(out * out, axis=-1, keepdims=True) + EPS) * gfin_ref[...]
    o_ref[...] = out


def _mix_ffn(x, m, mods3, mod_row_fn, g_ffn, g_fin, w_mo, w_up, w_conv, b_conv, w_down,
             side=(), *, layer, tm, seq, final):
    n, d = x.shape
    km = m.shape[1]
    assert w_conv.shape[1] == CONV_W == 3
    halo = tm % seq != 0
    assert not halo or seq % tm == 0
    nh = n // HALO
    per = tm // HALO
    prev_map = lambda i: (jnp.maximum(i * per - 1, 0), 0, 0)
    next_map = lambda i: (jnp.minimum((i + 1) * per, nh - 1), 0, 0)
    side_in, side_out, side_shapes = _side_cast_specs(side, n // tm)
    kern = functools.partial(_mix_ffn_kernel, layer=layer, tm=tm, seq=seq, halo=halo, final=final,
                             n_side=len(side))
    return pl.pallas_call(
        kern,
        grid=(n // tm,),
        in_specs=[
            pl.BlockSpec((tm, d), lambda i: (i, 0)),
            pl.BlockSpec((1, HALO, d), prev_map),
            pl.BlockSpec((1, HALO, d), next_map),
            pl.BlockSpec((tm, km), lambda i: (i, 0)),
            pl.BlockSpec((1, HALO, km), prev_map),
            pl.BlockSpec((1, HALO, km), next_map),
            pl.BlockSpec((1, 1, 6 * d), lambda i: (mod_row_fn(i), 0, 0)),
            _resident(g_ffn.shape),
            _resident((1, d)),
            _resident(w_mo.shape),
            _resident(w_up.shape),
            _resident_layer(w_conv.shape, layer),
            _resident(b_conv.shape),
            pl.BlockSpec(memory_space=pl.ANY),
        ] + side_in,
        out_specs=[pl.BlockSpec((tm, d), lambda i: (i, 0))] + side_out,
        out_shape=[jax.ShapeDtypeStruct((n, d), F32)] + side_shapes,
        scratch_shapes=[pltpu.VMEM((tm, FFN_DIM), BF16), pltpu.VMEM(w_down.shape, BF16),
                        pltpu.SemaphoreType.DMA((1,))],
        compiler_params=_params(),
        name="mix_ffn",
    )(x, x.reshape(nh, HALO, d), x.reshape(nh, HALO, d),
      m, m.reshape(nh, HALO, km), m.reshape(nh, HALO, km),
      mods3, g_ffn, g_fin, w_mo, w_up, w_conv, b_conv, w_down, *(w for w, _ in side))


def kernel(x_prompt, x_sample, cache_k, cache_v, c, c_ctx, w_ada, b_ada, norm_mix_g,
           norm_ffn_g, norm_final_g, w_qkv, w_attn_out, rpb, w_gmlp_in, g_gmlp_v,
           w_spatial, b_spatial, w_gmlp_out, w_ffn_up, w_ffn_conv, b_ffn_conv, w_ffn_down):
    batch, seq, d = x_prompt.shape
    dec_batch, dec_seq, _ = x_sample.shape
    depth = w_ada.shape[0]
    assert depth == 2 and d == D_MODEL

    cond = jnp.zeros((MOD_ROWS, d), F32).at[0].set(c_ctx).at[1:1 + dec_batch].set(c)
    mods3 = _ada(cond, w_ada, b_ada)

    w_qkv_b = _to_bf16(w_qkv[0])
    w_sp = w_spatial[0]
    b_s_full = jnp.broadcast_to(b_spatial[0][:, :, None], (GMLP_GROUPS, CHUNK, CHUNK))
    conv_w = (w_ffn_conv, b_ffn_conv)
    g_mix, g_ffn = norm_mix_g, norm_ffn_g
    g_fin = norm_final_g.reshape(1, d)
    tm = 512

    def ctx_row(layer):
        return lambda i: layer * MOD_ROWS
    x = x_prompt.reshape(batch * seq, d)
    o, state_kt, state_vt, w_ao, w_up0, w_down0 = _ctx_attn(
        x, mods3, 0, g_mix, w_qkv_b, [(w_attn_out, 0), (w_ffn_up, 0), (w_ffn_down, 0)],
        layer=0, batch=batch, seq=seq)
    state_k, state_v = jnp.swapaxes(state_kt, 3, 4), jnp.swapaxes(state_vt, 3, 4)
    x, w_gin = _mix_ffn(x, o, mods3, ctx_row(0), g_ffn, g_fin, w_ao, w_up0, *conv_w, w_down0,
                        [(w_gmlp_in, 0)], layer=0, tm=tm, seq=seq, final=False)
    m, w_gout, w_up1, w_down1 = _gmlp(
        x, mods3, ctx_row(1), g_mix, w_gin, g_gmlp_v, w_sp, b_s_full,
        [(w_gmlp_out, 0), (w_ffn_up, 1), (w_ffn_down, 1)], layer=1, tm=tm)
    y_prompt, = _mix_ffn(x, m, mods3, ctx_row(1), g_ffn, g_fin, w_gout, w_up1, *conv_w, w_down1,
                         layer=1, tm=tm, seq=seq, final=True)

    def lat_row(layer, rows=tm):
        return lambda i: layer * MOD_ROWS + 1 + (i * rows) // dec_seq
    x = x_sample.reshape(dec_batch * dec_seq, d)
    qkv = _qkv(x, mods3, lat_row(0), g_mix, w_qkv_b, layer=0, tm=tm)
    o = _nattn(qkv.reshape(dec_batch, dec_seq, 3 * d), jnp.swapaxes(cache_k, 3, 4),
               jnp.swapaxes(cache_v, 3, 4), rpb[0], layer=0)
    x, = _mix_ffn(x, o.reshape(dec_batch * dec_seq, d), mods3, lat_row(0), g_ffn, g_fin,
                  w_ao, w_up0, *conv_w, w_down0, layer=0, tm=tm, seq=dec_seq, final=False)
    m, = _gmlp(x, mods3, lat_row(1, dec_seq), g_mix, w_gin, g_gmlp_v, w_sp, b_s_full,
               layer=1, tm=dec_seq)
    y_sample, = _mix_ffn(x, m, mods3, lat_row(1), g_ffn, g_fin, w_gout, w_up1, *conv_w, w_down1,
                         layer=1, tm=tm, seq=dec_seq, final=True)

    return (y_prompt.reshape(batch, seq, d), y_sample.reshape(dec_batch, dec_seq, d),
            state_k, state_v)
```
